```python
import math
import jax, jax.numpy as jnp
from jax import lax
import numpy as np

D_MODEL = 1024
BATCH = 4
SEQ = 4096
DEPTH = 1
DEC_BATCH = 32
DEC_SEQ = 8
PAST_LEN = 8192
PAGE_SIZE = 128

ATT_HEAD_DIM = 64
N_ATT_HEADS = (D_MODEL // 2) // ATT_HEAD_DIM
ATT_WIDTH = N_ATT_HEADS * ATT_HEAD_DIM
N_ML_HEADS = 4
ML_HEAD_DIM = (D_MODEL // 2) // N_ML_HEADS
ML_WIDTH = N_ML_HEADS * ML_HEAD_DIM
MIX_WIDTH = ATT_WIDTH + ML_WIDTH
PROJ_COLS = 3 * ATT_WIDTH + 4 * ML_WIDTH + 2 * N_ML_HEADS
MOBA_BLOCK = 256
MOBA_TOPK = 3
QUERY_BLOCK = 128
ML_CHUNK = 64
NUM_BUCKETS = 32
MAX_DISTANCE = 128
D_FF = 4 * D_MODEL
RMS_EPS = 1e-6

kernel_name = "hymba_moba_mlstm_decoder_step"


def rmsnorm(x, g):
    xf = x.astype(jnp.float32)
    y = xf * lax.rsqrt(jnp.mean(xf * xf, axis=-1, keepdims=True) + RMS_EPS) * g.astype(jnp.float32)
    return y.astype(x.dtype)


def t5_bucket(dist):
    n = jnp.maximum(dist, 0)
    max_exact = NUM_BUCKETS // 2
    nf = jnp.maximum(n, 1).astype(jnp.float32)
    large = max_exact + (jnp.log(nf / max_exact) / math.log(MAX_DISTANCE / max_exact)
                         * (NUM_BUCKETS - max_exact)).astype(jnp.int32)
    large = jnp.minimum(large, NUM_BUCKETS - 1)
    return jnp.where(n < max_exact, n, large)


def split_projection(xn, w_in, b_ig, b_fg):
    lead = xn.shape[:-1]
    p = xn @ w_in
    sizes = [ATT_WIDTH] * 3 + [ML_WIDTH] * 4 + [N_ML_HEADS] * 2
    cuts = np.cumsum(sizes)[:-1].tolist()
    aq, ak, av, mq, mk, mv, mo, ig, fg = jnp.split(p, cuts, axis=-1)
    att = lambda a: a.reshape(lead + (N_ATT_HEADS, ATT_HEAD_DIM))
    mh = lambda a: a.reshape(lead + (N_ML_HEADS, ML_HEAD_DIM))
    ig = ig.astype(jnp.float32) + b_ig.astype(jnp.float32)
    logf = jax.nn.log_sigmoid(fg.astype(jnp.float32) + b_fg.astype(jnp.float32))
    return (att(aq), att(ak), att(av), mh(mq), mh(mk) * (ML_HEAD_DIM ** -0.5), mh(mv), mo, ig, logf)


def key_blocks(k):
    B, T, H, dh = k.shape
    nb = -(-T // MOBA_BLOCK)
    k = jnp.pad(k, ((0, 0), (0, nb * MOBA_BLOCK - T), (0, 0), (0, 0)))
    return k.reshape(B, nb, MOBA_BLOCK, H, dh).transpose(0, 3, 1, 2, 4)


def moba_attend(q, k_blocks, v_blocks, k_means, q_pos, rel_bias):
    T, H = q.shape[0], q.shape[1]
    NB = k_blocks.shape[1]
    n_sel = min(MOBA_TOPK, NB)
    own = q_pos // MOBA_BLOCK
    scores = jnp.einsum('thd,hnd->thn', q.astype(jnp.float32), k_means)
    fully_past = jnp.arange(NB)[None, None, :] < own[:, None, None]
    scores = jnp.where(fully_past, scores, -jnp.inf)
    _, sel = lax.top_k(scores, n_sel)
    sel_ok = jnp.broadcast_to(jnp.arange(n_sel)[None, None, :] < own[:, None, None], (T, H, n_sel))
    idx = jnp.concatenate([sel, jnp.broadcast_to(own[:, None, None], (T, H, 1))], axis=-1)
    ok = jnp.concatenate([sel_ok, jnp.ones((T, H, 1), dtype=bool)], axis=-1)
    head = jnp.arange(H)[None, :, None]
    kg = k_blocks[head, idx]
    vg = v_blocks[head, idx]
    k_pos = idx[..., None] * MOBA_BLOCK + jnp.arange(MOBA_BLOCK)
    dist = q_pos[:, None, None, None] - k_pos
    bias = rel_bias[t5_bucket(dist), head[..., None]].astype(jnp.float32)
    logits = jnp.einsum('thd,thjpd->thjp', q, kg, preferred_element_type=jnp.float32) * (ATT_HEAD_DIM ** -0.5) + bias
    logits = jnp.where(ok[..., None] & (dist >= 0), logits, -jnp.inf)
    J = logits.shape[2]
    probs = jax.nn.softmax(logits.reshape(T, H, J * MOBA_BLOCK), axis=-1).reshape(T, H, J, MOBA_BLOCK)
    return jnp.einsum('thjp,thjpd->thd', probs.astype(vg.dtype), vg)


def moba_prompt(q, k, v, rel_bias):
    B, S, H, dh = q.shape
    kb, vb = key_blocks(k), key_blocks(v)
    km = jnp.mean(kb.astype(jnp.float32), axis=3)
    nq = S // QUERY_BLOCK
    qc = q.reshape(B * nq, QUERY_BLOCK, H, dh)
    bidx = jnp.repeat(jnp.arange(B), nq)
    p0 = jnp.tile(jnp.arange(nq) * QUERY_BLOCK, B)

    def step(args):
        qq, b, s0 = args
        return moba_attend(qq, kb[b], vb[b], km[b], s0 + jnp.arange(QUERY_BLOCK), rel_bias)

    return lax.map(step, (qc, bidx, p0)).reshape(B, S, H, dh)


def moba_sample(q, k, v, pool_k, pool_v, page_table, rel_bias):
    DB, T = q.shape[0], q.shape[1]
    past_k = pool_k[page_table].reshape(DB, -1, N_ATT_HEADS, ATT_HEAD_DIM)
    past_v = pool_v[page_table].reshape(DB, -1, N_ATT_HEADS, ATT_HEAD_DIM)
    past = past_k.shape[1]
    kb = key_blocks(jnp.concatenate([past_k.astype(k.dtype), k], axis=1))
    vb = key_blocks(jnp.concatenate([past_v.astype(v.dtype), v], axis=1))
    km = jnp.mean(kb.astype(jnp.float32), axis=3)
    q_pos = past + jnp.arange(T)
    return lax.map(lambda a: moba_attend(a[0], a[1], a[2], a[3], q_pos, rel_bias), (q, kb, vb, km))


def mlstm_chunk(carry, inp):
    C, n, m = carry
    q, k, v, ig, lf = (a.astype(jnp.float32) for a in inp)
    L = q.shape[1]
    b = jnp.cumsum(lf, axis=1)
    a = b + m[:, None, :]
    logD = b[:, :, None, :] - b[:, None, :, :] + ig[:, None, :, :]
    causal = jnp.tril(jnp.ones((L, L), dtype=bool))[None, :, :, None]
    logD = jnp.where(causal, logD, -jnp.inf)
    m_t = jnp.maximum(a, jnp.max(logD, axis=2))
    D = jnp.exp(logD - m_t[:, :, None, :])
    w_state = jnp.exp(a - m_t)
    s = jnp.einsum('bthd,bjhd->btjh', q, k) * D
    num = w_state[..., None] * jnp.einsum('bthd,bhde->bthe', q, C) + jnp.einsum('btjh,bjhe->bthe', s, v)
    den = w_state * jnp.einsum('bthd,bhd->bth', q, n) + jnp.sum(s, axis=2)
    h = num / jnp.maximum(jnp.abs(den), jnp.exp(-m_t))[..., None]
    bL = b[:, -1, :]
    m_new = m_t[:, -1, :]
    g_state = jnp.exp(bL + m - m_new)
    g_tok = jnp.exp(bL[:, None, :] - b + ig - m_new[:, None, :])
    C_new = g_state[..., None, None] * C + jnp.einsum('bjh,bjhd,bjhe->bhde', g_tok, k, v)
    n_new = g_state[..., None] * n + jnp.einsum('bjh,bjhd->bhd', g_tok, k)
    return (C_new, n_new, m_new), h


def mlstm_prompt(q, k, v, ig, lf):
    B, S, H, d = q.shape
    L = min(ML_CHUNK, S)
    nc = S // L
    chunks = lambda a: jnp.moveaxis(a.reshape((B, nc, L) + a.shape[2:]), 1, 0)
    init = (jnp.zeros((B, H, d, d), jnp.float32), jnp.zeros((B, H, d), jnp.float32), jnp.zeros((B, H), jnp.float32))
    state, hs = lax.scan(mlstm_chunk, init, (chunks(q), chunks(k), chunks(v), chunks(ig), chunks(lf)))
    return jnp.moveaxis(hs, 0, 1).reshape(B, S, H, d), state


def mlstm_sample(q, k, v, ig, lf, C, n, m):
    state, h = mlstm_chunk((C.astype(jnp.float32), n.astype(jnp.float32), m.astype(jnp.float32)), (q, k, v, ig, lf))
    return h, state


def trunk_layer(x, attend, recur, norm_mix, w_in, b_ig, b_fg, ml_norm, w_out, norm_ffn, w_up, w_down):
    lead = x.shape[:-1]
    xn = rmsnorm(x, norm_mix)
    aq, ak, av, mq, mk, mv, mo, ig, lf = split_projection(xn, w_in, b_ig, b_fg)
    att = attend(aq, ak, av).reshape(lead + (ATT_WIDTH,))
    h, ml_state = recur(mq, mk, mv, ig, lf)
    hn = rmsnorm(h, ml_norm.reshape(N_ML_HEADS, ML_HEAD_DIM)).reshape(lead + (ML_WIDTH,))
    ml = hn * jax.nn.sigmoid(mo.astype(jnp.float32))
    mix = jnp.concatenate([att.astype(x.dtype), ml.astype(x.dtype)], axis=-1)
    x = x + mix @ w_out
    u = jax.nn.relu(rmsnorm(x, norm_ffn) @ w_up)
    x = x + (u * u) @ w_down
    return x, (ak, av) + tuple(ml_state)


def setup_inputs(seed: int = 0) -> dict:
    key = jax.random.key(seed)
    ks = jax.random.split(key, 24)
    n_pages = PAST_LEN // PAGE_SIZE
    n_used = DEC_BATCH * n_pages
    n_phys = n_used + max(1, n_used // 4)
    f32 = jnp.float32
    nrm = lambda k, shape, s: jax.random.normal(k, shape, f32) * s
    page_table = jax.random.permutation(ks[0], n_phys)[:n_used].reshape(DEC_BATCH, n_pages).astype(jnp.int32)
    b_fg = jnp.linspace(3.0, 6.0, N_ML_HEADS, dtype=f32)[None, :] + nrm(ks[13], (DEPTH, N_ML_HEADS), 0.1)
    return {
        "x_prompt": nrm(ks[1], (BATCH, SEQ, D_MODEL), 1.0),
        "x_sample": nrm(ks[2], (DEC_BATCH, DEC_SEQ, D_MODEL), 1.0),
        "cache_k": nrm(ks[3], (DEPTH, n_phys, PAGE_SIZE, N_ATT_HEADS, ATT_HEAD_DIM), 1.0),
        "cache_v": nrm(ks[4], (DEPTH, n_phys, PAGE_SIZE, N_ATT_HEADS, ATT_HEAD_DIM), 1.0),
        "state_C": nrm(ks[5], (DEPTH, DEC_BATCH, N_ML_HEADS, ML_HEAD_DIM, ML_HEAD_DIM), 0.1),
        "state_n": nrm(ks[6], (DEPTH, DEC_BATCH, N_ML_HEADS, ML_HEAD_DIM), 0.1),
        "state_m": nrm(ks[7], (DEPTH, DEC_BATCH, N_ML_HEADS), 1.0),
        "page_table": page_table,
        "rel_bias": nrm(ks[8], (NUM_BUCKETS, N_ATT_HEADS), 0.5),
        "norm_mix": 1.0 + nrm(ks[9], (DEPTH, D_MODEL), 0.1),
        "w_in": nrm(ks[10], (DEPTH, D_MODEL, PROJ_COLS), D_MODEL ** -0.5),
        "b_ig": nrm(ks[11], (DEPTH, N_ML_HEADS), 0.1),
        "b_fg": b_fg,
        "ml_norm": 1.0 + nrm(ks[12], (DEPTH, ML_WIDTH), 0.1),
        "w_out": nrm(ks[14], (DEPTH, MIX_WIDTH, D_MODEL), MIX_WIDTH ** -0.5),
        "norm_ffn": 1.0 + nrm(ks[15], (DEPTH, D_MODEL), 0.1),
        "w_up": nrm(ks[16], (DEPTH, D_MODEL, D_FF), D_MODEL ** -0.5),
        "w_down": nrm(ks[17], (DEPTH, D_FF, D_MODEL), D_FF ** -0.5),
        "norm_final": 1.0 + nrm(ks[18], (D_MODEL,), 0.1),
    }


def reference(x_prompt, x_sample, cache_k, cache_v, state_C, state_n, state_m, page_table, rel_bias,
              norm_mix, w_in, b_ig, b_fg, ml_norm, w_out, norm_ffn, w_up, w_down, norm_final):
    yp, ys = x_prompt, x_sample
    sp, ss = [], []
    for l in range(DEPTH):
        wl = (norm_mix[l], w_in[l], b_ig[l], b_fg[l], ml_norm[l], w_out[l], norm_ffn[l], w_up[l], w_down[l])
        yp, st_p = trunk_layer(
            yp,
            lambda q, k, v: moba_prompt(q, k, v, rel_bias),
            mlstm_prompt,
            *wl)
        ys, st_s = trunk_layer(
            ys,
            lambda q, k, v: moba_sample(q, k, v, cache_k[l], cache_v[l], page_table, rel_bias),
            lambda q, k, v, ig, lf: mlstm_sample(q, k, v, ig, lf, state_C[l], state_n[l], state_m[l]),
            *wl)
        sp.append(st_p)
        ss.append(st_s)
    y_prompt = rmsnorm(yp, norm_final)
    y_sample = rmsnorm(ys, norm_final)
    new_k_prompt = jnp.stack([s[0] for s in sp])
    new_v_prompt = jnp.stack([s[1] for s in sp])
    new_C_prompt = jnp.stack([s[2] for s in sp])
    new_n_prompt = jnp.stack([s[3] for s in sp])
    new_m_prompt = jnp.stack([s[4] for s in sp])
    new_k_sample = jnp.stack([s[0] for s in ss])
    new_v_sample = jnp.stack([s[1] for s in ss])
    new_C_sample = jnp.stack([s[2] for s in ss])
    new_n_sample = jnp.stack([s[3] for s in ss])
    new_m_sample = jnp.stack([s[4] for s in ss])
    return (y_prompt, y_sample, new_k_prompt, new_v_prompt, new_C_prompt, new_n_prompt, new_m_prompt,
            new_k_sample, new_v_sample, new_C_sample, new_n_sample, new_m_sample)
```

```python
import functools
import math

import numpy as np
import jax
import jax.numpy as jnp
from jax import lax
from jax.experimental import pallas as pl
from jax.experimental.pallas import tpu as pltpu

F32 = jnp.float32
BF16 = jnp.bfloat16
HIGHEST = lax.Precision.HIGHEST

D_MODEL = 1024
ATT_HEAD_DIM = 64
N_ATT_HEADS = 8
ATT_WIDTH = 512
N_ML_HEADS = 4
ML_HEAD_DIM = 128
ML_WIDTH = 512
N_SLABS = 7
GATE_COL0 = N_SLABS * 512
MOBA_BLOCK = 256
MOBA_TOPK = 3
PAGE_SIZE = 128
NUM_BUCKETS = 32
MAX_DISTANCE = 128
D_FF = 4 * D_MODEL
RMS_EPS = 1e-6
NEG = -1e30
LANES = 128
VMEM_LIMIT = 56 * 1024 * 1024


def _nt_dot(a, b, precision=None):
    return lax.dot_general(a, b, (((1,), (1,)), ((), ())), preferred_element_type=F32, precision=precision)


def _tn_dot(a, b, precision=None):
    return lax.dot_general(a, b, (((0,), (0,)), ((), ())), preferred_element_type=F32, precision=precision)


def _bucket_upper_bounds():
    n = np.arange(0, MAX_DISTANCE + 1)
    max_exact = NUM_BUCKETS // 2
    nf = np.maximum(n, 1).astype(np.float32)
    large = max_exact + (np.log(nf / np.float32(max_exact)) / np.float32(math.log(MAX_DISTANCE / max_exact))
                         * np.float32(NUM_BUCKETS - max_exact)).astype(np.int32)
    large = np.minimum(large, NUM_BUCKETS - 1)
    bucket = np.where(n < max_exact, n, large)
    return np.array([int(n[bucket <= b].max()) for b in range(NUM_BUCKETS - 1)], dtype=np.int32)


def _bias_of_dist(dist, h, rb_ref, ub_ref):
    def body(i, acc):
        b = NUM_BUCKETS - 2 - i
        return jnp.where(dist <= ub_ref[b], rb_ref[b, h], acc)
    init = jnp.full(dist.shape, rb_ref[NUM_BUCKETS - 1, h], F32)
    return lax.fori_loop(0, NUM_BUCKETS - 1, body, init)


def _bias_tiles_kernel(rb_ref, ub_ref, t0_ref, t1_ref, s_own_ref, s_near_ref, s_far_ref, *, dec_seq):
    h = pl.program_id(0)
    r = lax.broadcasted_iota(jnp.int32, (MOBA_BLOCK, MOBA_BLOCK), 0)
    c = lax.broadcasted_iota(jnp.int32, (MOBA_BLOCK, MOBA_BLOCK), 1)
    d0 = r - c
    t0_ref[0] = jnp.where(d0 >= 0, _bias_of_dist(jnp.maximum(d0, 0), h, rb_ref, ub_ref), NEG)
    t1_ref[0] = _bias_of_dist(d0 + MOBA_BLOCK, h, rb_ref, ub_ref)
    t = lax.broadcasted_iota(jnp.int32, (dec_seq, LANES), 0)
    cc = lax.broadcasted_iota(jnp.int32, (dec_seq, LANES), 1)
    ds = t - cc
    s_own_ref[0] = jnp.where((ds >= 0) & (cc < dec_seq), _bias_of_dist(jnp.maximum(ds, 0), h, rb_ref, ub_ref), NEG)
    s_near_ref[0] = _bias_of_dist(ds + PAGE_SIZE, h, rb_ref, ub_ref)
    s_far_ref[0] = jnp.full((dec_seq, LANES), rb_ref[NUM_BUCKETS - 1, h], F32)


def _bias_tiles(rel_bias, dec_seq):
    ub = jnp.asarray(_bucket_upper_bounds())
    smem = pl.BlockSpec(memory_space=pltpu.SMEM)
    big = pl.BlockSpec((1, MOBA_BLOCK, MOBA_BLOCK), lambda h: (h, 0, 0))
    small = pl.BlockSpec((1, dec_seq, LANES), lambda h: (h, 0, 0))
    return pl.pallas_call(
        functools.partial(_bias_tiles_kernel, dec_seq=dec_seq),
        grid=(N_ATT_HEADS,),
        in_specs=[smem, smem],
        out_specs=[big, big, small, small, small],
        out_shape=[jax.ShapeDtypeStruct((N_ATT_HEADS, MOBA_BLOCK, MOBA_BLOCK), F32)] * 2
        + [jax.ShapeDtypeStruct((N_ATT_HEADS, dec_seq, LANES), F32)] * 3,
        name="bias_tiles",
    )(rel_bias, ub)


def _rms(x, g):
    return x * lax.rsqrt(jnp.mean(x * x, axis=-1, keepdims=True) + RMS_EPS) * g


def _log_sigmoid(z):
    return jnp.minimum(z, 0.0) - jnp.log1p(jnp.exp(-jnp.abs(z)))


def _proj_kernel(x_ref, g_ref, w_ref, wg_ref, bg_ref, aq_ref, ak_ref, av_ref, mq_ref, mk_ref, mv_ref, mo_ref,
                 gate_ref):
    xn = _rms(x_ref[...], g_ref[...])
    xb = xn.astype(BF16)
    outs = (aq_ref, ak_ref, av_ref, mq_ref, mk_ref, mv_ref, mo_ref)
    scales = (ATT_HEAD_DIM ** -0.5, None, None, None, ML_HEAD_DIM ** -0.5, None, None)
    for i, (o_ref, s) in enumerate(zip(outs, scales)):
        r = jnp.dot(xb, w_ref[:, i * 512:(i + 1) * 512], preferred_element_type=F32)
        o_ref[...] = r if s is None else r * s
    z = jnp.dot(xn, wg_ref[...], preferred_element_type=F32, precision=HIGHEST) + bg_ref[...]
    lane = lax.broadcasted_iota(jnp.int32, z.shape, 1)
    gate_ref[...] = jnp.where((lane >= N_ML_HEADS) & (lane < 2 * N_ML_HEADS), _log_sigmoid(z), z)


def _project(x2, norm_g, w_main, w_gate, b_gate, tm):
    n = x2.shape[0]
    const = lambda shape: pl.BlockSpec(shape, lambda i: (0, 0), pipeline_mode=pl.Buffered(1))
    slab = pl.BlockSpec((tm, 512), lambda i: (i, 0))
    return pl.pallas_call(
        _proj_kernel,
        grid=(n // tm,),
        in_specs=[pl.BlockSpec((tm, D_MODEL), lambda i: (i, 0)), const((1, D_MODEL)),
                  const((D_MODEL, N_SLABS * 512)), const((D_MODEL, LANES)), const((1, LANES))],
        out_specs=[slab] * N_SLABS + [pl.BlockSpec((tm, LANES), lambda i: (i, 0))],
        out_shape=[jax.ShapeDtypeStruct((n, 512), F32)] * N_SLABS + [jax.ShapeDtypeStruct((n, LANES), F32)],
        compiler_params=pltpu.CompilerParams(dimension_semantics=("parallel",), vmem_limit_bytes=VMEM_LIMIT),
        name="rms_in_proj",
    )(x2, norm_g, w_main, w_gate, b_gate)


def _flash_update(s, v, m_ref, l_ref, acc_ref):
    m_prev = m_ref[...]
    m_new = jnp.maximum(m_prev, jnp.max(s, axis=-1, keepdims=True))
    reps = s.shape[1] // LANES
    m_wide = m_new if reps == 1 else jnp.concatenate([m_new] * reps, axis=1)
    p = jnp.exp(s - m_wide)
    alpha = jnp.exp(m_prev - m_new)
    l_ref[...] = alpha * l_ref[...] + jnp.sum(p, axis=-1, keepdims=True)
    m_ref[...] = m_new
    pv = jnp.dot(p.astype(BF16), v, preferred_element_type=F32)
    wreps = acc_ref.shape[1] // LANES
    a_wide = alpha if wreps == 1 else jnp.concatenate([alpha] * wreps, axis=1)
    acc_ref[...] = a_wide * acc_ref[...] + pv


def _topk_penalty(scores, n_valid):
    lane = lax.broadcasted_iota(jnp.int32, scores.shape, 1)

    def body(n, rank):
        col = jnp.sum(jnp.where(lane == n, scores, 0.0), axis=-1, keepdims=True)
        beats = (col > scores) | ((col == scores) & (lane > n))
        return rank + jnp.where(beats, 1, 0)

    rank = lax.fori_loop(0, n_valid, body, jnp.zeros(scores.shape, jnp.int32))
    return jnp.where((rank < MOBA_TOPK) & (lane < n_valid), 0.0, NEG)


def _lane_column(x, n):
    lane = lax.broadcasted_iota(jnp.int32, x.shape, 1)
    return jnp.sum(jnp.where(lane == n, x, 0.0), axis=-1, keepdims=True)


def _moba_prompt_kernel(rb_ref, q_ref, k_ref, v_ref, t0_ref, t1_ref, o_ref, km_ref, m_ref, l_ref, acc_ref,
                        *, n_blocks):
    hp = pl.program_id(1)
    own = pl.program_id(2)
    blk = MOBA_BLOCK

    @pl.when(own == 0)
    def _():
        km_ref[...] = jnp.zeros(km_ref.shape, F32)
        for n in range(n_blocks):
            km_ref[n:n + 1, :] = jnp.mean(k_ref[0, n * blk:(n + 1) * blk, :], axis=0, keepdims=True)

    q2 = q_ref[0]
    lane = lax.broadcasted_iota(jnp.int32, q2.shape, 1)
    outs = []
    for hh in range(2):
        in_head = (lane // ATT_HEAD_DIM) == hh
        qm = jnp.where(in_head, q2, 0.0)
        qb = qm.astype(BF16)
        pen = _topk_penalty(_nt_dot(qm, km_ref[...], precision=HIGHEST), own)
        far_bias = rb_ref[NUM_BUCKETS - 1, 2 * hp + hh]
        m_ref[...] = jnp.full(m_ref.shape, NEG, F32)
        l_ref[...] = jnp.zeros(l_ref.shape, F32)
        acc_ref[...] = jnp.zeros(acc_ref.shape, F32)

        def step(j, bias):
            kj = k_ref[0, pl.ds(pl.multiple_of(j * blk, blk), blk), :].astype(BF16)
            vj = v_ref[0, pl.ds(pl.multiple_of(j * blk, blk), blk), :].astype(BF16)
            _flash_update(_nt_dot(qb, kj) + bias, vj, m_ref, l_ref, acc_ref)

        step(own, t0_ref[hh])

        @pl.when(own >= 1)
        def _():
            step(own - 1, t1_ref[hh] + _lane_column(pen, own - 1))

        def far(j, carry):
            step(j, _lane_column(pen, j) + far_bias)
            return carry

        lax.fori_loop(0, jnp.maximum(own - 1, 0), far, 0)
        outs.append(acc_ref[...] / l_ref[...])
    o_ref[0] = jnp.where((lane // ATT_HEAD_DIM) == 0, outs[0], outs[1])


def _moba_prompt(aq, ak, av, t0, t1, rel_bias):
    b, s, _ = aq.shape
    nb = s // MOBA_BLOCK
    n_pairs = N_ATT_HEADS // 2
    qspec = pl.BlockSpec((1, MOBA_BLOCK, LANES), lambda bi, hp, qi: (bi, qi, hp))
    kvspec = pl.BlockSpec((1, s, LANES), lambda bi, hp, qi: (bi, 0, hp))
    tspec = pl.BlockSpec((2, MOBA_BLOCK, MOBA_BLOCK), lambda bi, hp, qi: (hp, 0, 0))
    return pl.pallas_call(
        functools.partial(_moba_prompt_kernel, n_blocks=nb),
        grid=(b, n_pairs, nb),
        in_specs=[pl.BlockSpec(memory_space=pltpu.SMEM), qspec, kvspec, kvspec, tspec, tspec],
        out_specs=qspec,
        out_shape=jax.ShapeDtypeStruct((b, s, ATT_WIDTH), F32),
        scratch_shapes=[pltpu.VMEM((LANES, LANES), F32), pltpu.VMEM((MOBA_BLOCK, LANES), F32),
                        pltpu.VMEM((MOBA_BLOCK, LANES), F32), pltpu.VMEM((MOBA_BLOCK, LANES), F32)],
        compiler_params=pltpu.CompilerParams(dimension_semantics=("parallel", "parallel", "arbitrary"),
                                             vmem_limit_bytes=VMEM_LIMIT),
        name="moba_prompt",
    )(rel_bias, aq, ak, av, t0, t1)


def _page_specs(n_sub, pages_per_step):
    def spec(i):
        return pl.BlockSpec((1, PAGE_SIZE, ATT_WIDTH), lambda b, c, pt: (pt[b, c * pages_per_step + i], 0, 0))
    return [spec(i) for i in range(pages_per_step)] * n_sub


def _block_sum_kernel(pt_ref, *refs, pages_per_step):
    page_refs, o_ref = refs[:pages_per_step], refs[pages_per_step]
    c = pl.program_id(1)
    per_block = MOBA_BLOCK // PAGE_SIZE
    for i in range(pages_per_step // per_block):
        acc = jnp.sum(page_refs[per_block * i][0], axis=0, keepdims=True)
        for j in range(1, per_block):
            acc = acc + jnp.sum(page_refs[per_block * i + j][0], axis=0, keepdims=True)
        o_ref[0, pl.ds(c * (pages_per_step // per_block) + i, 1), :] = acc


def _past_block_sums(pool_k, page_table, pages_per_step):
    db, n_pages = page_table.shape
    nb = n_pages * PAGE_SIZE // MOBA_BLOCK
    grid_spec = pltpu.PrefetchScalarGridSpec(
        num_scalar_prefetch=1, grid=(db, n_pages // pages_per_step),
        in_specs=_page_specs(1, pages_per_step),
        out_specs=pl.BlockSpec((1, nb, ATT_WIDTH), lambda b, c, pt: (b, 0, 0)))
    return pl.pallas_call(
        functools.partial(_block_sum_kernel, pages_per_step=pages_per_step),
        grid_spec=grid_spec,
        out_shape=jax.ShapeDtypeStruct((db, nb, ATT_WIDTH), F32),
        compiler_params=pltpu.CompilerParams(dimension_semantics=("parallel", "arbitrary"),
                                             vmem_limit_bytes=VMEM_LIMIT),
        name="past_block_sums",
    )(page_table, *([pool_k] * pages_per_step))


def _moba_sample_kernel(pt_ref, *refs, pages_per_step, n_pages, dec_seq):
    pps = pages_per_step
    k_pages, v_pages = refs[:pps], refs[pps:2 * pps]
    (q_ref, ksum_ref, kn_ref, vn_ref, s_own_ref, s_near_ref, s_far_ref, o_ref,
     qbd_ref, pen_ref, m_ref, l_ref, acc_ref) = refs[2 * pps:]
    c = pl.program_id(1)
    rows = N_ATT_HEADS * dec_seq
    nb_past = n_pages * PAGE_SIZE // MOBA_BLOCK
    per_block = MOBA_BLOCK // PAGE_SIZE
    row_head = lax.broadcasted_iota(jnp.int32, (rows, ATT_WIDTH), 0) // dec_seq
    lane_head = lax.broadcasted_iota(jnp.int32, (rows, ATT_WIDTH), 1) // ATT_HEAD_DIM

    @pl.when(c == 0)
    def _():
        qrep = jnp.concatenate([q_ref[0]] * N_ATT_HEADS, axis=0)
        qbd = jnp.where(row_head == lane_head, qrep, 0.0)
        qbd_ref[...] = qbd
        kmean = jnp.concatenate([ksum_ref[0] * (1.0 / MOBA_BLOCK),
                                 jnp.zeros((LANES - nb_past, ATT_WIDTH), F32)], axis=0)
        pen_ref[...] = _topk_penalty(_nt_dot(qbd, kmean, precision=HIGHEST), nb_past)
        m_ref[...] = jnp.full(m_ref.shape, NEG, F32)
        l_ref[...] = jnp.zeros(l_ref.shape, F32)
        acc_ref[...] = jnp.zeros(acc_ref.shape, F32)
        pad = jnp.zeros((LANES - dec_seq, ATT_WIDTH), F32)
        kn = jnp.concatenate([kn_ref[0], pad], axis=0).astype(BF16)
        vn = jnp.concatenate([vn_ref[0], pad], axis=0).astype(BF16)
        s_own = s_own_ref[...].reshape(rows, LANES)
        _flash_update(_nt_dot(qbd.astype(BF16), kn) + s_own, vn, m_ref, l_ref, acc_ref)

    qb = qbd_ref[...].astype(BF16)
    s_near = s_near_ref[...].reshape(rows, LANES)
    s_far = s_far_ref[...].reshape(rows, LANES)
    pen = pen_ref[...]
    for i in range(pps):
        page = c * pps + i
        bias = jnp.where(page == n_pages - 1, s_near, s_far) + _lane_column(pen, page // per_block)
        kp = k_pages[i][0].astype(BF16)
        vp = v_pages[i][0].astype(BF16)
        _flash_update(_nt_dot(qb, kp) + bias, vp, m_ref, l_ref, acc_ref)

    @pl.when(c == pl.num_programs(1) - 1)
    def _():
        l_wide = jnp.concatenate([l_ref[...]] * (ATT_WIDTH // LANES), axis=1)
        full = jnp.where(row_head == lane_head, acc_ref[...] / l_wide, 0.0)
        out = full[0:dec_seq]
        for h in range(1, N_ATT_HEADS):
            out = out + full[h * dec_seq:(h + 1) * dec_seq]
        o_ref[0] = out


def _moba_sample(aq, ak, av, pool_k, pool_v, page_table, ksum, s_own, s_near, s_far, pages_per_step):
    db, dec_seq, _ = aq.shape
    n_pages = page_table.shape[1]
    nb = ksum.shape[1]
    rows = N_ATT_HEADS * dec_seq
    per_b = lambda shape: pl.BlockSpec(shape, lambda b, c, pt: (b, 0, 0))
    tile = pl.BlockSpec((N_ATT_HEADS, dec_seq, LANES), lambda b, c, pt: (0, 0, 0))
    grid_spec = pltpu.PrefetchScalarGridSpec(
        num_scalar_prefetch=1, grid=(db, n_pages // pages_per_step),
        in_specs=_page_specs(2, pages_per_step)
        + [per_b((1, dec_seq, ATT_WIDTH)), per_b((1, nb, ATT_WIDTH)), per_b((1, dec_seq, ATT_WIDTH)),
           per_b((1, dec_seq, ATT_WIDTH)), tile, tile, tile],
        out_specs=per_b((1, dec_seq, ATT_WIDTH)),
        scratch_shapes=[pltpu.VMEM((rows, ATT_WIDTH), F32), pltpu.VMEM((rows, LANES), F32),
                        pltpu.VMEM((rows, LANES), F32), pltpu.VMEM((rows, LANES), F32),
                        pltpu.VMEM((rows, ATT_WIDTH), F32)])
    return pl.pallas_call(
        functools.partial(_moba_sample_kernel, pages_per_step=pages_per_step, n_pages=n_pages, dec_seq=dec_seq),
        grid_spec=grid_spec,
        out_shape=jax.ShapeDtypeStruct((db, dec_seq, ATT_WIDTH), F32),
        compiler_params=pltpu.CompilerParams(dimension_semantics=("parallel", "arbitrary"),
                                             vmem_limit_bytes=VMEM_LIMIT),
        name="moba_sample",
    )(page_table, *([pool_k] * pages_per_step), *([pool_v] * pages_per_step),
      aq, ksum, ak, av, s_own, s_near, s_far)


def _mlstm_kernel(q_ref, k_ref, v_ref, g_ref, mo_ref, nw_ref, c0_ref, n0_ref, m0_ref,
                  h_ref, c_out, n_out, m_out, c_s, n_s, m_s, *, rows, chunk):
    ci = pl.program_id(1)
    L = chunk

    @pl.when(ci == 0)
    def _():
        c_s[...] = c0_ref[0]
        n_s[...] = n0_ref[0]
        m_s[...] = m0_ref[0]

    def padded(ref, fill=None):
        x = ref[0]
        if rows == L:
            return x
        tail = jnp.zeros((L - rows, x.shape[1]), F32) if fill is None else fill
        return jnp.concatenate([x, tail], axis=0)

    lane = lax.broadcasted_iota(jnp.int32, (L, LANES), 1)
    pad_lane = lax.broadcasted_iota(jnp.int32, (max(L - rows, 1), LANES), 1)
    g = padded(g_ref, jnp.where(pad_lane < N_ML_HEADS, NEG, 0.0))
    q_all, k_all, v_all, mo_all = padded(q_ref), padded(k_ref), padded(v_ref), padded(mo_ref)

    is_lf = (lane >= N_ML_HEADS) & (lane < 2 * N_ML_HEADS)
    tri_r = lax.broadcasted_iota(jnp.int32, (L, L), 0)
    tri_c = lax.broadcasted_iota(jnp.int32, (L, L), 1)
    causal = tri_c <= tri_r
    bcum = jnp.dot(causal.astype(F32), jnp.where(is_lf, g, 0.0), preferred_element_type=F32, precision=HIGHEST)
    g_t = g.T
    b_t = bcum.T

    for h in range(N_ML_HEADS):
        sl = slice(h * ML_HEAD_DIM, (h + 1) * ML_HEAD_DIM)
        q, k, v = q_all[:, sl], k_all[:, sl], v_all[:, sl]
        ig_col = g[:, h:h + 1]
        b_col = bcum[:, N_ML_HEADS + h:N_ML_HEADS + h + 1]
        ig_row = g_t[h:h + 1, :]
        b_row = b_t[N_ML_HEADS + h:N_ML_HEADS + h + 1, :]
        m_prev = m_s[h][:, 0:1]
        C = c_s[h]
        n_row = n_s[h]

        a_col = b_col + m_prev
        log_d = jnp.where(causal, b_col - b_row + ig_row, NEG)
        m_t = jnp.maximum(a_col, jnp.max(log_d, axis=-1, keepdims=True))
        d = jnp.exp(log_d - m_t)
        w_state = jnp.exp(a_col - m_t)
        qb, kb, vb = q.astype(BF16), k.astype(BF16), v.astype(BF16)
        s = _nt_dot(qb, kb) * d
        num = w_state * jnp.dot(qb, C.astype(BF16), preferred_element_type=F32) \
            + jnp.dot(s.astype(BF16), vb, preferred_element_type=F32)
        den = w_state * jnp.sum(q * n_row, axis=-1, keepdims=True) + jnp.sum(s, axis=-1, keepdims=True)
        hid = num / jnp.maximum(jnp.abs(den), jnp.exp(-m_t))

        hn = _rms(hid, nw_ref[:, sl])
        h_ref[0, :, sl] = (hn * jax.nn.sigmoid(mo_all[:, sl]))[:rows]

        b_last = b_col[L - 1:L, :]
        m_new = m_t[L - 1:L, :]
        g_state = jnp.exp(b_last + m_prev - m_new)
        g_tok = jnp.exp(b_last - b_col + ig_col - m_new)
        kg = k * g_tok
        c_s[h] = g_state * C + _tn_dot(kg.astype(BF16), vb)
        n_s[h] = g_state * n_row + jnp.sum(kg, axis=0, keepdims=True)
        m_s[h] = jnp.broadcast_to(m_new, (1, LANES))

    @pl.when(ci == pl.num_programs(1) - 1)
    def _():
        c_out[0] = c_s[...]
        n_out[0] = n_s[...]
        m_out[0] = m_s[...]


def _mlstm(mq, mk, mv, gates, mo, ml_norm, c0, n0, m0, chunk):
    b, s, _ = mq.shape
    rows = min(s, chunk)
    n_chunks = s // rows
    tok = lambda w: pl.BlockSpec((1, rows, w), lambda bi, ci: (bi, ci, 0))
    st4 = lambda shape: pl.BlockSpec(shape, lambda bi, ci: (bi, 0, 0, 0))
    c_shape, v_shape = (1, N_ML_HEADS, ML_HEAD_DIM, ML_HEAD_DIM), (1, N_ML_HEADS, 1, LANES)
    return pl.pallas_call(
        functools.partial(_mlstm_kernel, rows=rows, chunk=chunk),
        grid=(b, n_chunks),
        in_specs=[tok(ML_WIDTH), tok(ML_WIDTH), tok(ML_WIDTH), tok(LANES), tok(ML_WIDTH),
                  pl.BlockSpec((1, ML_WIDTH), lambda bi, ci: (0, 0)),
                  st4(c_shape), st4(v_shape), st4(v_shape)],
        out_specs=[tok(ML_WIDTH), st4(c_shape), st4(v_shape), st4(v_shape)],
        out_shape=[jax.ShapeDtypeStruct((b, s, ML_WIDTH), F32),
                   jax.ShapeDtypeStruct((b,) + c_shape[1:], F32),
                   jax.ShapeDtypeStruct((b,) + v_shape[1:], F32),
                   jax.ShapeDtypeStruct((b,) + v_shape[1:], F32)],
        scratch_shapes=[pltpu.VMEM(c_shape[1:], F32), pltpu.VMEM(v_shape[1:], F32), pltpu.VMEM(v_shape[1:], F32)],
        compiler_params=pltpu.CompilerParams(dimension_semantics=("parallel", "arbitrary"),
                                             vmem_limit_bytes=VMEM_LIMIT),
        name="mlstm",
    )(mq, mk, mv, gates, mo, ml_norm, c0, n0, m0)


def _out_ffn_kernel(x_ref, att_ref, ml_ref, wo_ref, gf_ref, wu_ref, wd_ref, gl_ref, y_ref, *, ff_chunk):
    x1 = x_ref[...] \
        + jnp.dot(att_ref[...].astype(BF16), wo_ref[0:ATT_WIDTH, :], preferred_element_type=F32) \
        + jnp.dot(ml_ref[...].astype(BF16), wo_ref[ATT_WIDTH:, :], preferred_element_type=F32)
    xb = _rms(x1, gf_ref[...]).astype(BF16)
    ffn = None
    for c in range(D_FF // ff_chunk):
        sl = slice(c * ff_chunk, (c + 1) * ff_chunk)
        u = jnp.maximum(jnp.dot(xb, wu_ref[:, sl], preferred_element_type=F32), 0.0)
        d = jnp.dot((u * u).astype(BF16), wd_ref[sl, :], preferred_element_type=F32)
        ffn = d if ffn is None else ffn + d
    y_ref[...] = _rms(x1 + ffn, gl_ref[...])


def _out_ffn(x2, att, ml, w_out, norm_ffn, w_up, w_down, norm_final, tm, ff_chunk):
    n = x2.shape[0]
    const = lambda shape: pl.BlockSpec(shape, lambda i: (0, 0), pipeline_mode=pl.Buffered(1))
    row = lambda w: pl.BlockSpec((tm, w), lambda i: (i, 0))
    return pl.pallas_call(
        functools.partial(_out_ffn_kernel, ff_chunk=ff_chunk),
        grid=(n // tm,),
        in_specs=[row(D_MODEL), row(ATT_WIDTH), row(ML_WIDTH), const((D_MODEL, D_MODEL)), const((1, D_MODEL)),
                  const((D_MODEL, D_FF)), const((D_FF, D_MODEL)), const((1, D_MODEL))],
        out_specs=row(D_MODEL),
        out_shape=jax.ShapeDtypeStruct((n, D_MODEL), F32),
        compiler_params=pltpu.CompilerParams(dimension_semantics=("parallel",), vmem_limit_bytes=VMEM_LIMIT),
        name="out_proj_ffn",
    )(x2, att, ml, w_out, norm_ffn, w_up, w_down, norm_final)


def _layer_weights(l, norm_mix, w_in, b_ig, b_fg, ml_norm, w_out, norm_ffn, w_up, w_down):
    w_main = w_in[l, :, :GATE_COL0].astype(BF16)
    w_gate = jnp.pad(w_in[l, :, GATE_COL0:], ((0, 0), (0, LANES - 2 * N_ML_HEADS)))
    b_gate = jnp.pad(jnp.concatenate([b_ig[l], b_fg[l]]), (0, LANES - 2 * N_ML_HEADS))[None, :]
    return dict(norm_mix=norm_mix[l][None, :], w_main=w_main, w_gate=w_gate, b_gate=b_gate,
                ml_norm=ml_norm[l][None, :], w_out=w_out[l].astype(BF16), norm_ffn=norm_ffn[l][None, :],
                w_up=w_up[l].astype(BF16), w_down=w_down[l].astype(BF16))


def _trunk_layer(x, w, attend, state0, norm_final, tm, ml_chunk):
    b, s, _ = x.shape
    x2 = x.reshape(b * s, D_MODEL)
    aq, ak, av, mq, mk, mv, mo, gates = _project(x2, w["norm_mix"], w["w_main"], w["w_gate"], w["b_gate"], tm)
    r3 = lambda a: a.reshape(b, s, a.shape[-1])
    att = attend(r3(aq), r3(ak), r3(av))
    c0, n0, m0 = state0
    ml, c_new, n_new, m_new = _mlstm(r3(mq), r3(mk), r3(mv), r3(gates), r3(mo), w["ml_norm"], c0, n0, m0, ml_chunk)
    y = _out_ffn(x2, att.reshape(b * s, ATT_WIDTH), ml.reshape(b * s, ML_WIDTH), w["w_out"], w["norm_ffn"],
                 w["w_up"], w["w_down"], norm_final, tm, 1024)
    return (y.reshape(b, s, D_MODEL), ak.reshape(b, s, N_ATT_HEADS, ATT_HEAD_DIM),
            av.reshape(b, s, N_ATT_HEADS, ATT_HEAD_DIM), c_new, n_new[:, :, 0, :], m_new[:, :, 0, 0])


def kernel(x_prompt, x_sample, cache_k, cache_v, state_C, state_n, state_m, page_table, rel_bias, norm_mix, w_in,
           b_ig, b_fg, ml_norm, w_out, norm_ffn, w_up, w_down, norm_final):
    depth = w_in.shape[0]
    assert depth == 1, "the fused final RMSNorm assumes a single layer"
    bp, sp, _ = x_prompt.shape
    db, dec_seq, _ = x_sample.shape
    n_phys = cache_k.shape[1]
    t0, t1, s_own, s_near, s_far = _bias_tiles(rel_bias, dec_seq)
    gl = norm_final[None, :]
    l = 0
    w = _layer_weights(l, norm_mix, w_in, b_ig, b_fg, ml_norm, w_out, norm_ffn, w_up, w_down)

    zero_state = (jnp.zeros((bp, N_ML_HEADS, ML_HEAD_DIM, ML_HEAD_DIM), F32),
                  jnp.zeros((bp, N_ML_HEADS, 1, LANES), F32), jnp.zeros((bp, N_ML_HEADS, 1, LANES), F32))
    yp, kp, vp, cp, np_, mp = _trunk_layer(
        x_prompt, w, lambda q, k, v: _moba_prompt(q, k, v, t0, t1, rel_bias), zero_state, gl, 512, 256)

    pool_k = cache_k.reshape(depth * n_phys, PAGE_SIZE, ATT_WIDTH)
    pool_v = cache_v.reshape(depth * n_phys, PAGE_SIZE, ATT_WIDTH)
    pt = page_table + l * n_phys
    pps = 8

    def attend_sample(q, k, v):
        ksum = _past_block_sums(pool_k, pt, pps)
        return _moba_sample(q, k, v, pool_k, pool_v, pt, ksum, s_own, s_near, s_far, pps)

    state0 = (state_C[l], state_n[l][:, :, None, :],
              jnp.broadcast_to(state_m[l][:, :, None, None], (db, N_ML_HEADS, 1, LANES)))
    ys, ks, vs, cs, ns, ms = _trunk_layer(x_sample, w, attend_sample, state0, gl, 256, 128)

    st = lambda a: a[None]
    return (yp, ys, st(kp), st(vp), st(cp), st(np_), st(mp), st(ks), st(vs), st(cs), st(ns), st(ms))
```

```python
import functools
import math

import numpy as np
import jax
import jax.numpy as jnp
from jax import lax
from jax.experimental import pallas as pl
from jax.experimental.pallas import tpu as pltpu

F32 = jnp.float32
BF16 = jnp.bfloat16
HIGHEST = lax.Precision.HIGHEST

D_MODEL = 1024
ATT_HEAD_DIM = 64
N_ATT_HEADS = 8
ATT_WIDTH = 512
N_ML_HEADS = 4
ML_HEAD_DIM = 128
ML_WIDTH = 512
N_SLABS = 7
GATE_COL0 = N_SLABS * 512
MOBA_BLOCK = 256
MOBA_TOPK = 3
PAGE_SIZE = 128
NUM_BUCKETS = 32
MAX_DISTANCE = 128
D_FF = 4 * D_MODEL
RMS_EPS = 1e-6
NEG = -1e30
LANES = 128
VMEM_LIMIT = 56 * 1024 * 1024


def _nt_dot(a, b, precision=None):
    return lax.dot_general(a, b, (((1,), (1,)), ((), ())), preferred_element_type=F32, precision=precision)


def _tn_dot(a, b, precision=None):
    return lax.dot_general(a, b, (((0,), (0,)), ((), ())), preferred_element_type=F32, precision=precision)


def _bucket_upper_bounds():
    n = np.arange(0, MAX_DISTANCE + 1)
    max_exact = NUM_BUCKETS // 2
    nf = np.maximum(n, 1).astype(np.float32)
    large = max_exact + (np.log(nf / np.float32(max_exact)) / np.float32(math.log(MAX_DISTANCE / max_exact))
                         * np.float32(NUM_BUCKETS - max_exact)).astype(np.int32)
    large = np.minimum(large, NUM_BUCKETS - 1)
    bucket = np.where(n < max_exact, n, large)
    return np.array([int(n[bucket <= b].max()) for b in range(NUM_BUCKETS - 1)], dtype=np.int32)


def _bias_of_dist(dist, h, rb_ref, ub_ref):
    def body(i, acc):
        b = NUM_BUCKETS - 2 - i
        return jnp.where(dist <= ub_ref[b], rb_ref[b, h], acc)
    init = jnp.full(dist.shape, rb_ref[NUM_BUCKETS - 1, h], F32)
    return lax.fori_loop(0, NUM_BUCKETS - 1, body, init)


def _bias_tiles_kernel(rb_ref, ub_ref, t0_ref, t1_ref, s_own_ref, s_near_ref, s_far_ref, *, dec_seq):
    h = pl.program_id(0)
    kk = lax.broadcasted_iota(jnp.int32, (MOBA_BLOCK, MOBA_BLOCK), 0)
    qq = lax.broadcasted_iota(jnp.int32, (MOBA_BLOCK, MOBA_BLOCK), 1)
    d0 = qq - kk
    t0_ref[0] = jnp.where(d0 >= 0, _bias_of_dist(jnp.maximum(d0, 0), h, rb_ref, ub_ref), NEG)
    t1_ref[0] = _bias_of_dist(d0 + MOBA_BLOCK, h, rb_ref, ub_ref)
    t = lax.broadcasted_iota(jnp.int32, (dec_seq, LANES), 0)
    cc = lax.broadcasted_iota(jnp.int32, (dec_seq, LANES), 1)
    ds = t - cc
    s_own_ref[0] = jnp.where((ds >= 0) & (cc < dec_seq), _bias_of_dist(jnp.maximum(ds, 0), h, rb_ref, ub_ref), NEG)
    s_near_ref[0] = _bias_of_dist(ds + PAGE_SIZE, h, rb_ref, ub_ref)
    s_far_ref[0] = jnp.full((dec_seq, LANES), rb_ref[NUM_BUCKETS - 1, h], F32)


def _bias_tiles(rel_bias, dec_seq):
    ub = jnp.asarray(_bucket_upper_bounds())
    smem = pl.BlockSpec(memory_space=pltpu.SMEM)
    big = pl.BlockSpec((1, MOBA_BLOCK, MOBA_BLOCK), lambda h: (h, 0, 0))
    small = pl.BlockSpec((1, dec_seq, LANES), lambda h: (h, 0, 0))
    return pl.pallas_call(
        functools.partial(_bias_tiles_kernel, dec_seq=dec_seq),
        grid=(N_ATT_HEADS,),
        in_specs=[smem, smem],
        out_specs=[big, big, small, small, small],
        out_shape=[jax.ShapeDtypeStruct((N_ATT_HEADS, MOBA_BLOCK, MOBA_BLOCK), F32)] * 2
        + [jax.ShapeDtypeStruct((N_ATT_HEADS, dec_seq, LANES), F32)] * 3,
        name="bias_tiles",
    )(rel_bias, ub)


def _rms(x, g):
    return x * lax.rsqrt(jnp.mean(x * x, axis=-1, keepdims=True) + RMS_EPS) * g


def _log_sigmoid(z):
    return jnp.minimum(z, 0.0) - jnp.log1p(jnp.exp(-jnp.abs(z)))


def _proj_kernel(x_ref, g_ref, w_ref, wg_ref, bg_ref, aq_ref, ak_ref, av_ref, mq_ref, mk_ref, mv_ref, mo_ref,
                 gate_ref):
    xn = _rms(x_ref[...], g_ref[...])
    xb = xn.astype(BF16)
    outs = (aq_ref, ak_ref, av_ref, mq_ref, mk_ref, mv_ref, mo_ref)
    scales = (ATT_HEAD_DIM ** -0.5, None, None, None, ML_HEAD_DIM ** -0.5, None, None)
    for i, (o_ref, s) in enumerate(zip(outs, scales)):
        r = jnp.dot(xb, w_ref[:, i * 512:(i + 1) * 512], preferred_element_type=F32)
        o_ref[...] = r if s is None else r * s
    z = jnp.dot(xn, wg_ref[...], preferred_element_type=F32, precision=HIGHEST) + bg_ref[...]
    lane = lax.broadcasted_iota(jnp.int32, z.shape, 1)
    gate_ref[...] = jnp.where((lane >= N_ML_HEADS) & (lane < 2 * N_ML_HEADS), _log_sigmoid(z), z)


def _project(x2, norm_g, w_main, w_gate, b_gate, tm):
    n = x2.shape[0]
    const = lambda shape: pl.BlockSpec(shape, lambda i: (0, 0), pipeline_mode=pl.Buffered(1))
    slab = pl.BlockSpec((tm, 512), lambda i: (i, 0))
    return pl.pallas_call(
        _proj_kernel,
        grid=(n // tm,),
        in_specs=[pl.BlockSpec((tm, D_MODEL), lambda i: (i, 0)), const((1, D_MODEL)),
                  const((D_MODEL, N_SLABS * 512)), const((D_MODEL, LANES)), const((1, LANES))],
        out_specs=[slab] * N_SLABS + [pl.BlockSpec((tm, LANES), lambda i: (i, 0))],
        out_shape=[jax.ShapeDtypeStruct((n, 512), F32)] * N_SLABS + [jax.ShapeDtypeStruct((n, LANES), F32)],
        compiler_params=pltpu.CompilerParams(dimension_semantics=("parallel",), vmem_limit_bytes=VMEM_LIMIT),
        name="rms_in_proj",
    )(x2, norm_g, w_main, w_gate, b_gate)


def _flash_update(s, v, m_ref, l_ref, acc_ref):
    m_prev = m_ref[...]
    m_new = jnp.maximum(m_prev, jnp.max(s, axis=-1, keepdims=True))
    reps = s.shape[1] // LANES
    m_wide = m_new if reps == 1 else jnp.concatenate([m_new] * reps, axis=1)
    p = jnp.exp(s - m_wide)
    alpha = jnp.exp(m_prev - m_new)
    l_ref[...] = alpha * l_ref[...] + jnp.sum(p, axis=-1, keepdims=True)
    m_ref[...] = m_new
    pv = jnp.dot(p.astype(BF16), v, preferred_element_type=F32)
    wreps = acc_ref.shape[1] // LANES
    a_wide = alpha if wreps == 1 else jnp.concatenate([alpha] * wreps, axis=1)
    acc_ref[...] = a_wide * acc_ref[...] + pv


def _topk_penalty(scores, n_valid):
    lane = lax.broadcasted_iota(jnp.int32, scores.shape, 1)

    def body(n, rank):
        col = jnp.sum(jnp.where(lane == n, scores, 0.0), axis=-1, keepdims=True)
        beats = (col > scores) | ((col == scores) & (lane > n))
        return rank + jnp.where(beats, 1, 0)

    rank = lax.fori_loop(0, n_valid, body, jnp.zeros(scores.shape, jnp.int32))
    return jnp.where((rank < MOBA_TOPK) & (lane < n_valid), 0.0, NEG)


def _lane_column(x, n):
    lane = lax.broadcasted_iota(jnp.int32, x.shape, 1)
    return jnp.sum(jnp.where(lane == n, x, 0.0), axis=-1, keepdims=True)


def _moba_prompt_kernel(rb_ref, q_ref, k_ref, v_ref, t0_ref, t1_ref, o_ref, km_ref, kb_ref, vt_ref, sc_ref,
                        pen_ref, s_ref, so_ref, sn_ref, *, n_blocks, group):
    hp = pl.program_id(1)
    own = pl.program_id(2)
    blk = MOBA_BLOCK
    nbp = km_ref.shape[0]

    @pl.when(own == 0)
    def _():
        km_ref[...] = jnp.zeros(km_ref.shape, F32)
        for n in range(n_blocks):
            kn = k_ref[0, n * blk:(n + 1) * blk, :]
            km_ref[n:n + 1, :] = jnp.mean(kn, axis=0, keepdims=True)
            kb_ref[n] = kn.astype(BF16)
            vt_ref[n] = v_ref[0, n * blk:(n + 1) * blk, :].T.astype(BF16)

    q2 = q_ref[0]
    lane = lax.broadcasted_iota(jnp.int32, q2.shape, 1)
    qcat = jnp.concatenate([jnp.where((lane // ATT_HEAD_DIM) == hh, q2, 0.0) for hh in range(2)], axis=0)
    qcat_b = qcat.astype(BF16)
    wide = 2 * blk
    sc_ref[...] = _nt_dot(km_ref[...], qcat, precision=HIGHEST)

    sub = lax.broadcasted_iota(jnp.int32, (nbp, wide), 0)

    def rank_body(n, rank):
        sc = sc_ref[...]
        row = sc_ref[pl.ds(n, 1), :]
        return rank + jnp.where((row > sc) | ((row == sc) & (sub > n)), 1, 0)

    rank = lax.fori_loop(0, own, rank_body, jnp.zeros((nbp, wide), jnp.int32))
    pen_ref[...] = jnp.where((rank < MOBA_TOPK) & (sub < own), 0.0, NEG)

    col_head = lax.broadcasted_iota(jnp.int32, (1, wide), 1) // blk
    far_bias = jnp.where(col_head == 0, rb_ref[NUM_BUCKETS - 1, 2 * hp], rb_ref[NUM_BUCKETS - 1, 2 * hp + 1])

    def fold8(x, op):
        return op(x.reshape(blk // 8, 8, x.shape[1]), axis=0)

    def far_row(j):
        return jnp.where(j < own - 1, pen_ref[pl.ds(j, 1), :] + far_bias, NEG)

    grp = group
    n_groups = (jnp.maximum(own - 1, 0) + grp - 1) // grp
    jn = jnp.maximum(own - 1, 0)
    near_row = jnp.where(own >= 1, pen_ref[pl.ds(jn, 1), :], NEG)

    s_own = _nt_dot(kb_ref[own], qcat_b) + jnp.concatenate([t0_ref[0], t0_ref[1]], axis=1)
    s_near = _nt_dot(kb_ref[jn], qcat_b) + jnp.concatenate([t1_ref[0], t1_ref[1]], axis=1)
    so_ref[...] = s_own
    sn_ref[...] = s_near
    m8 = jnp.maximum(fold8(s_own, jnp.max), fold8(s_near, jnp.max) + near_row)

    def max_body(g, m8c):
        j0 = pl.multiple_of(g * grp, grp)
        sg = _nt_dot(kb_ref[pl.ds(j0, grp)].reshape(grp * blk, LANES), qcat_b)
        s_ref[pl.ds(j0, grp)] = sg.reshape(grp, blk, wide)
        for i in range(grp):
            m8c = jnp.maximum(m8c, fold8(sg[i * blk:(i + 1) * blk], jnp.max) + far_row(j0 + i))
        return m8c

    m8 = lax.fori_loop(0, n_groups, max_body, m8)
    m = jnp.max(m8, axis=0, keepdims=True)

    def accumulate(j, p, l8, accs):
        pb = p.astype(BF16)
        new = tuple(accs[hh] + jnp.dot(vt_ref[j, hh * ATT_HEAD_DIM:(hh + 1) * ATT_HEAD_DIM, :],
                                       pb[:, hh * blk:(hh + 1) * blk], preferred_element_type=F32)
                    for hh in range(2))
        return l8 + fold8(p, jnp.sum), new

    zero_acc = jnp.zeros((ATT_HEAD_DIM, blk), F32)
    l8, accs = accumulate(own, jnp.exp(so_ref[...] - m), jnp.zeros((8, wide), F32), (zero_acc, zero_acc))
    l8, accs = accumulate(jn, jnp.exp(sn_ref[...] + (near_row - m)), l8, accs)

    def exp_body(g, st):
        l8c, accc = st
        j0 = pl.multiple_of(g * grp, grp)
        for i in range(grp):
            l8c, accc = accumulate(j0 + i, jnp.exp(s_ref[j0 + i] + (far_row(j0 + i) - m)), l8c, accc)
        return l8c, accc

    l8, accs = lax.fori_loop(0, n_groups, exp_body, (l8, accs))
    l = jnp.sum(l8, axis=0, keepdims=True)
    out_t = jnp.concatenate([accs[hh] / l[:, hh * blk:(hh + 1) * blk] for hh in range(2)], axis=0)
    o_ref[0] = out_t.T


def _moba_prompt(aq, ak, av, t0, t1, rel_bias):
    b, s, _ = aq.shape
    nb = s // MOBA_BLOCK
    nbp = -(-nb // 8) * 8
    n_pairs = N_ATT_HEADS // 2
    wide = 2 * MOBA_BLOCK
    group = 4
    assert nb % group == 0
    qspec = pl.BlockSpec((1, MOBA_BLOCK, LANES), lambda bi, hp, qi: (bi, qi, hp))
    kvspec = pl.BlockSpec((1, s, LANES), lambda bi, hp, qi: (bi, 0, hp))
    tspec = pl.BlockSpec((2, MOBA_BLOCK, MOBA_BLOCK), lambda bi, hp, qi: (hp, 0, 0))
    return pl.pallas_call(
        functools.partial(_moba_prompt_kernel, n_blocks=nb, group=group),
        grid=(b, n_pairs, nb),
        in_specs=[pl.BlockSpec(memory_space=pltpu.SMEM), qspec, kvspec, kvspec, tspec, tspec],
        out_specs=qspec,
        out_shape=jax.ShapeDtypeStruct((b, s, ATT_WIDTH), F32),
        scratch_shapes=[pltpu.VMEM((nbp, LANES), F32), pltpu.VMEM((nb, MOBA_BLOCK, LANES), BF16),
                        pltpu.VMEM((nb, LANES, MOBA_BLOCK), BF16),
                        pltpu.VMEM((nbp, wide), F32), pltpu.VMEM((nbp, wide), F32),
                        pltpu.VMEM((nb, MOBA_BLOCK, wide), F32), pltpu.VMEM((MOBA_BLOCK, wide), F32),
                        pltpu.VMEM((MOBA_BLOCK, wide), F32)],
        compiler_params=pltpu.CompilerParams(dimension_semantics=("parallel", "parallel", "arbitrary"),
                                             vmem_limit_bytes=VMEM_LIMIT),
        name="moba_prompt",
    )(rel_bias, aq, ak, av, t0, t1)


def _page_specs(n_sub, pages_per_step):
    def spec(i):
        return pl.BlockSpec((1, PAGE_SIZE, ATT_WIDTH), lambda b, c, pt: (pt[b, c * pages_per_step + i], 0, 0))
    return [spec(i) for i in range(pages_per_step)] * n_sub


def _block_sum_kernel(pt_ref, *refs, pages_per_step):
    page_refs, o_ref = refs[:pages_per_step], refs[pages_per_step]
    c = pl.program_id(1)
    per_block = MOBA_BLOCK // PAGE_SIZE
    for i in range(pages_per_step // per_block):
        acc = jnp.sum(page_refs[per_block * i][0], axis=0, keepdims=True)
        for j in range(1, per_block):
            acc = acc + jnp.sum(page_refs[per_block * i + j][0], axis=0, keepdims=True)
        o_ref[0, pl.ds(c * (pages_per_step // per_block) + i, 1), :] = acc


def _past_block_sums(pool_k, page_table, pages_per_step):
    db, n_pages = page_table.shape
    nb = n_pages * PAGE_SIZE // MOBA_BLOCK
    grid_spec = pltpu.PrefetchScalarGridSpec(
        num_scalar_prefetch=1, grid=(db, n_pages // pages_per_step),
        in_specs=_page_specs(1, pages_per_step),
        out_specs=pl.BlockSpec((1, nb, ATT_WIDTH), lambda b, c, pt: (b, 0, 0)))
    return pl.pallas_call(
        functools.partial(_block_sum_kernel, pages_per_step=pages_per_step),
        grid_spec=grid_spec,
        out_shape=jax.ShapeDtypeStruct((db, nb, ATT_WIDTH), F32),
        compiler_params=pltpu.CompilerParams(dimension_semantics=("parallel", "arbitrary"),
                                             vmem_limit_bytes=VMEM_LIMIT),
        name="past_block_sums",
    )(page_table, *([pool_k] * pages_per_step))


def _moba_sample_kernel(pt_ref, *refs, pages_per_step, n_pages, dec_seq):
    pps = pages_per_step
    k_pages, v_pages = refs[:pps], refs[pps:2 * pps]
    (q_ref, ksum_ref, kn_ref, vn_ref, s_own_ref, s_near_ref, s_far_ref, o_ref,
     qbd_ref, pen_ref, m_ref, l_ref, acc_ref) = refs[2 * pps:]
    c = pl.program_id(1)
    rows = N_ATT_HEADS * dec_seq
    nb_past = n_pages * PAGE_SIZE // MOBA_BLOCK
    per_block = MOBA_BLOCK // PAGE_SIZE
    row_head = lax.broadcasted_iota(jnp.int32, (rows, ATT_WIDTH), 0) // dec_seq
    lane_head = lax.broadcasted_iota(jnp.int32, (rows, ATT_WIDTH), 1) // ATT_HEAD_DIM

    @pl.when(c == 0)
    def _():
        qrep = jnp.concatenate([q_ref[0]] * N_ATT_HEADS, axis=0)
        qbd = jnp.where(row_head == lane_head, qrep, 0.0)
        qbd_ref[...] = qbd
        kmean = jnp.concatenate([ksum_ref[0] * (1.0 / MOBA_BLOCK),
                                 jnp.zeros((LANES - nb_past, ATT_WIDTH), F32)], axis=0)
        pen_ref[...] = _topk_penalty(_nt_dot(qbd, kmean, precision=HIGHEST), nb_past)
        m_ref[...] = jnp.full(m_ref.shape, NEG, F32)
        l_ref[...] = jnp.zeros(l_ref.shape, F32)
        acc_ref[...] = jnp.zeros(acc_ref.shape, F32)
        pad = jnp.zeros((LANES - dec_seq, ATT_WIDTH), F32)
        kn = jnp.concatenate([kn_ref[0], pad], axis=0).astype(BF16)
        vn = jnp.concatenate([vn_ref[0], pad], axis=0).astype(BF16)
        s_own = s_own_ref[...].reshape(rows, LANES)
        _flash_update(_nt_dot(qbd.astype(BF16), kn) + s_own, vn, m_ref, l_ref, acc_ref)

    qb = qbd_ref[...].astype(BF16)
    s_near = s_near_ref[...].reshape(rows, LANES)
    s_far = s_far_ref[...].reshape(rows, LANES)
    pen = pen_ref[...]
    for i in range(pps):
        page = c * pps + i
        bias = jnp.where(page == n_pages - 1, s_near, s_far) + _lane_column(pen, page // per_block)
        kp = k_pages[i][0].astype(BF16)
        vp = v_pages[i][0].astype(BF16)
        _flash_update(_nt_dot(qb, kp) + bias, vp, m_ref, l_ref, acc_ref)

    @pl.when(c == pl.num_programs(1) - 1)
    def _():
        l_wide = jnp.concatenate([l_ref[...]] * (ATT_WIDTH // LANES), axis=1)
        full = jnp.where(row_head == lane_head, acc_ref[...] / l_wide, 0.0)
        out = full[0:dec_seq]
        for h in range(1, N_ATT_HEADS):
            out = out + full[h * dec_seq:(h + 1) * dec_seq]
        o_ref[0] = out


def _moba_sample(aq, ak, av, pool_k, pool_v, page_table, ksum, s_own, s_near, s_far, pages_per_step):
    db, dec_seq, _ = aq.shape
    n_pages = page_table.shape[1]
    nb = ksum.shape[1]
    rows = N_ATT_HEADS * dec_seq
    per_b = lambda shape: pl.BlockSpec(shape, lambda b, c, pt: (b, 0, 0))
    tile = pl.BlockSpec((N_ATT_HEADS, dec_seq, LANES), lambda b, c, pt: (0, 0, 0))
    grid_spec = pltpu.PrefetchScalarGridSpec(
        num_scalar_prefetch=1, grid=(db, n_pages // pages_per_step),
        in_specs=_page_specs(2, pages_per_step)
        + [per_b((1, dec_seq, ATT_WIDTH)), per_b((1, nb, ATT_WIDTH)), per_b((1, dec_seq, ATT_WIDTH)),
           per_b((1, dec_seq, ATT_WIDTH)), tile, tile, tile],
        out_specs=per_b((1, dec_seq, ATT_WIDTH)),
        scratch_shapes=[pltpu.VMEM((rows, ATT_WIDTH), F32), pltpu.VMEM((rows, LANES), F32),
                        pltpu.VMEM((rows, LANES), F32), pltpu.VMEM((rows, LANES), F32),
                        pltpu.VMEM((rows, ATT_WIDTH), F32)])
    return pl.pallas_call(
        functools.partial(_moba_sample_kernel, pages_per_step=pages_per_step, n_pages=n_pages, dec_seq=dec_seq),
        grid_spec=grid_spec,
        out_shape=jax.ShapeDtypeStruct((db, dec_seq, ATT_WIDTH), F32),
        compiler_params=pltpu.CompilerParams(dimension_semantics=("parallel", "arbitrary"),
                                             vmem_limit_bytes=VMEM_LIMIT),
        name="moba_sample",
    )(page_table, *([pool_k] * pages_per_step), *([pool_v] * pages_per_step),
      aq, ksum, ak, av, s_own, s_near, s_far)


def _mlstm_kernel(q_ref, k_ref, v_ref, g_ref, mo_ref, nw_ref, c0_ref, n0_ref, m0_ref,
                  h_ref, c_out, n_out, m_out, c_s, n_s, m_s, *, rows, chunk):
    ci = pl.program_id(1)
    L = chunk

    @pl.when(ci == 0)
    def _():
        c_s[...] = c0_ref[0]
        n_s[...] = n0_ref[0]
        m_s[...] = m0_ref[0]

    def padded(ref, fill=None):
        x = ref[0]
        if rows == L:
            return x
        tail = jnp.zeros((L - rows, x.shape[1]), F32) if fill is None else fill
        return jnp.concatenate([x, tail], axis=0)

    lane = lax.broadcasted_iota(jnp.int32, (L, LANES), 1)
    pad_lane = lax.broadcasted_iota(jnp.int32, (max(L - rows, 1), LANES), 1)
    g = padded(g_ref, jnp.where(pad_lane < N_ML_HEADS, NEG, 0.0))
    q_all, k_all, v_all, mo_all = padded(q_ref), padded(k_ref), padded(v_ref), padded(mo_ref)

    is_lf = (lane >= N_ML_HEADS) & (lane < 2 * N_ML_HEADS)
    tri_r = lax.broadcasted_iota(jnp.int32, (L, L), 0)
    tri_c = lax.broadcasted_iota(jnp.int32, (L, L), 1)
    causal = tri_c <= tri_r
    bcum = jnp.dot(causal.astype(F32), jnp.where(is_lf, g, 0.0), preferred_element_type=F32, precision=HIGHEST)
    g_t = g.T
    b_t = bcum.T

    for h in range(N_ML_HEADS):
        sl = slice(h * ML_HEAD_DIM, (h + 1) * ML_HEAD_DIM)
        q, k, v = q_all[:, sl], k_all[:, sl], v_all[:, sl]
        ig_col = g[:, h:h + 1]
        b_col = bcum[:, N_ML_HEADS + h:N_ML_HEADS + h + 1]
        ig_row = g_t[h:h + 1, :]
        b_row = b_t[N_ML_HEADS + h:N_ML_HEADS + h + 1, :]
        m_prev = m_s[h][:, 0:1]
        C = c_s[h]
        n_row = n_s[h]

        a_col = b_col + m_prev
        log_d = jnp.where(causal, b_col - b_row + ig_row, NEG)
        m_t = jnp.maximum(a_col, jnp.max(log_d, axis=-1, keepdims=True))
        d = jnp.exp(log_d - m_t)
        w_state = jnp.exp(a_col - m_t)
        qb, kb, vb = q.astype(BF16), k.astype(BF16), v.astype(BF16)
        s = _nt_dot(qb, kb) * d
        num = w_state * jnp.dot(qb, C.astype(BF16), preferred_element_type=F32) \
            + jnp.dot(s.astype(BF16), vb, preferred_element_type=F32)
        den = w_state * jnp.sum(q * n_row, axis=-1, keepdims=True) + jnp.sum(s, axis=-1, keepdims=True)
        hid = num / jnp.maximum(jnp.abs(den), jnp.exp(-m_t))

        hn = _rms(hid, nw_ref[:, sl])
        h_ref[0, :, sl] = (hn * jax.nn.sigmoid(mo_all[:, sl]))[:rows]

        b_last = b_col[L - 1:L, :]
        m_new = m_t[L - 1:L, :]
        g_state = jnp.exp(b_last + m_prev - m_new)
        g_tok = jnp.exp(b_last - b_col + ig_col - m_new)
        kg = k * g_tok
        c_s[h] = g_state * C + _tn_dot(kg.astype(BF16), vb)
        n_s[h] = g_state * n_row + jnp.sum(kg, axis=0, keepdims=True)
        m_s[h] = jnp.broadcast_to(m_new, (1, LANES))

    @pl.when(ci == pl.num_programs(1) - 1)
    def _():
        c_out[0] = c_s[...]
        n_out[0] = n_s[...]
        m_out[0] = m_s[...]


def _mlstm(mq, mk, mv, gates, mo, ml_norm, c0, n0, m0, chunk):
    b, s, _ = mq.shape
    rows = min(s, chunk)
    n_chunks = s // rows
    tok = lambda w: pl.BlockSpec((1, rows, w), lambda bi, ci: (bi, ci, 0))
    st4 = lambda shape: pl.BlockSpec(shape, lambda bi, ci: (bi, 0, 0, 0))
    c_shape, v_shape = (1, N_ML_HEADS, ML_HEAD_DIM, ML_HEAD_DIM), (1, N_ML_HEADS, 1, LANES)
    return pl.pallas_call(
        functools.partial(_mlstm_kernel, rows=rows, chunk=chunk),
        grid=(b, n_chunks),
        in_specs=[tok(ML_WIDTH), tok(ML_WIDTH), tok(ML_WIDTH), tok(LANES), tok(ML_WIDTH),
                  pl.BlockSpec((1, ML_WIDTH), lambda bi, ci: (0, 0)),
                  st4(c_shape), st4(v_shape), st4(v_shape)],
        out_specs=[tok(ML_WIDTH), st4(c_shape), st4(v_shape), st4(v_shape)],
        out_shape=[jax.ShapeDtypeStruct((b, s, ML_WIDTH), F32),
                   jax.ShapeDtypeStruct((b,) + c_shape[1:], F32),
                   jax.ShapeDtypeStruct((b,) + v_shape[1:], F32),
                   jax.ShapeDtypeStruct((b,) + v_shape[1:], F32)],
        scratch_shapes=[pltpu.VMEM(c_shape[1:], F32), pltpu.VMEM(v_shape[1:], F32), pltpu.VMEM(v_shape[1:], F32)],
        compiler_params=pltpu.CompilerParams(dimension_semantics=("parallel", "arbitrary"),
                                             vmem_limit_bytes=VMEM_LIMIT),
        name="mlstm",
    )(mq, mk, mv, gates, mo, ml_norm, c0, n0, m0)


def _out_ffn_kernel(x_ref, att_ref, ml_ref, wo_ref, gf_ref, wu_ref, wd_ref, gl_ref, y_ref, *, ff_chunk):
    x1 = x_ref[...] \
        + jnp.dot(att_ref[...].astype(BF16), wo_ref[0:ATT_WIDTH, :], preferred_element_type=F32) \
        + jnp.dot(ml_ref[...].astype(BF16), wo_ref[ATT_WIDTH:, :], preferred_element_type=F32)
    xb = _rms(x1, gf_ref[...]).astype(BF16)
    ffn = None
    for c in range(D_FF // ff_chunk):
        sl = slice(c * ff_chunk, (c + 1) * ff_chunk)
        u = jnp.maximum(jnp.dot(xb, wu_ref[:, sl], preferred_element_type=F32), 0.0)
        d = jnp.dot((u * u).astype(BF16), wd_ref[sl, :], preferred_element_type=F32)
        ffn = d if ffn is None else ffn + d
    y_ref[...] = _rms(x1 + ffn, gl_ref[...])


def _out_ffn(x2, att, ml, w_out, norm_ffn, w_up, w_down, norm_final, tm, ff_chunk):
    n = x2.shape[0]
    const = lambda shape: pl.BlockSpec(shape, lambda i: (0, 0), pipeline_mode=pl.Buffered(1))
    row = lambda w: pl.BlockSpec((tm, w), lambda i: (i, 0))
    return pl.pallas_call(
        functools.partial(_out_ffn_kernel, ff_chunk=ff_chunk),
        grid=(n // tm,),
        in_specs=[row(D_MODEL), row(ATT_WIDTH), row(ML_WIDTH), const((D_MODEL, D_MODEL)), const((1, D_MODEL)),
                  const((D_MODEL, D_FF)), const((D_FF, D_MODEL)), const((1, D_MODEL))],
        out_specs=row(D_MODEL),
        out_shape=jax.ShapeDtypeStruct((n, D_MODEL), F32),
        compiler_params=pltpu.CompilerParams(dimension_semantics=("parallel",), vmem_limit_bytes=VMEM_LIMIT),
        name="out_proj_ffn",
    )(x2, att, ml, w_out, norm_ffn, w_up, w_down, norm_final)


def _layer_weights(l, norm_mix, w_in, b_ig, b_fg, ml_norm, w_out, norm_ffn, w_up, w_down):
    w_main = w_in[l, :, :GATE_COL0].astype(BF16)
    w_gate = jnp.pad(w_in[l, :, GATE_COL0:], ((0, 0), (0, LANES - 2 * N_ML_HEADS)))
    b_gate = jnp.pad(jnp.concatenate([b_ig[l], b_fg[l]]), (0, LANES - 2 * N_ML_HEADS))[None, :]
    return dict(norm_mix=norm_mix[l][None, :], w_main=w_main, w_gate=w_gate, b_gate=b_gate,
                ml_norm=ml_norm[l][None, :], w_out=w_out[l].astype(BF16), norm_ffn=norm_ffn[l][None, :],
                w_up=w_up[l].astype(BF16), w_down=w_down[l].astype(BF16))


def _trunk_layer(x, w, attend, state0, norm_final, tm, ml_chunk):
    b, s, _ = x.shape
    x2 = x.reshape(b * s, D_MODEL)
    aq, ak, av, mq, mk, mv, mo, gates = _project(x2, w["norm_mix"], w["w_main"], w["w_gate"], w["b_gate"], tm)
    r3 = lambda a: a.reshape(b, s, a.shape[-1])
    att = attend(r3(aq), r3(ak), r3(av))
    c0, n0, m0 = state0
    ml, c_new, n_new, m_new = _mlstm(r3(mq), r3(mk), r3(mv), r3(gates), r3(mo), w["ml_norm"], c0, n0, m0, ml_chunk)
    y = _out_ffn(x2, att.reshape(b * s, ATT_WIDTH), ml.reshape(b * s, ML_WIDTH), w["w_out"], w["norm_ffn"],
                 w["w_up"], w["w_down"], norm_final, tm, 1024)
    return (y.reshape(b, s, D_MODEL), ak.reshape(b, s, N_ATT_HEADS, ATT_HEAD_DIM),
            av.reshape(b, s, N_ATT_HEADS, ATT_HEAD_DIM), c_new, n_new[:, :, 0, :], m_new[:, :, 0, 0])


def kernel(x_prompt, x_sample, cache_k, cache_v, state_C, state_n, state_m, page_table, rel_bias, norm_mix, w_in,
           b_ig, b_fg, ml_norm, w_out, norm_ffn, w_up, w_down, norm_final):
    depth = w_in.shape[0]
    assert depth == 1, "the fused final RMSNorm assumes a single layer"
    bp, sp, _ = x_prompt.shape
    db, dec_seq, _ = x_sample.shape
    n_phys = cache_k.shape[1]
    t0, t1, s_own, s_near, s_far = _bias_tiles(rel_bias, dec_seq)
    gl = norm_final[None, :]
    l = 0
    w = _layer_weights(l, norm_mix, w_in, b_ig, b_fg, ml_norm, w_out, norm_ffn, w_up, w_down)

    zero_state = (jnp.zeros((bp, N_ML_HEADS, ML_HEAD_DIM, ML_HEAD_DIM), F32),
                  jnp.zeros((bp, N_ML_HEADS, 1, LANES), F32), jnp.zeros((bp, N_ML_HEADS, 1, LANES), F32))
    yp, kp, vp, cp, np_, mp = _trunk_layer(
        x_prompt, w, lambda q, k, v: _moba_prompt(q, k, v, t0, t1, rel_bias), zero_state, gl, 512, 256)

    pool_k = cache_k.reshape(depth * n_phys, PAGE_SIZE, ATT_WIDTH)
    pool_v = cache_v.reshape(depth * n_phys, PAGE_SIZE, ATT_WIDTH)
    pt = page_table + l * n_phys
    pps = 8

    def attend_sample(q, k, v):
        ksum = _past_block_sums(pool_k, pt, pps)
        return _moba_sample(q, k, v, pool_k, pool_v, pt, ksum, s_own, s_near, s_far, pps)

    state0 = (state_C[l], state_n[l][:, :, None, :],
              jnp.broadcast_to(state_m[l][:, :, None, None], (db, N_ML_HEADS, 1, LANES)))
    ys, ks, vs, cs, ns, ms = _trunk_layer(x_sample, w, attend_sample, state0, gl, 256, 128)

    st = lambda a: a[None]
    return (yp, ys, st(kp), st(vp), st(cp), st(np_), st(mp), st(ks), st(vs), st(cs), st(ns), st(ms))
```

```python
import functools
import math

import numpy as np
import jax
import jax.numpy as jnp
from jax import lax
from jax.experimental import pallas as pl
from jax.experimental.pallas import tpu as pltpu

F32 = jnp.float32
BF16 = jnp.bfloat16
HIGHEST = lax.Precision.HIGHEST

D_MODEL = 1024
ATT_HEAD_DIM = 64
N_ATT_HEADS = 8
ATT_WIDTH = 512
N_ML_HEADS = 4
ML_HEAD_DIM = 128
ML_WIDTH = 512
N_SLABS = 7
GATE_COL0 = N_SLABS * 512
MOBA_BLOCK = 256
MOBA_TOPK = 3
PAGE_SIZE = 128
NUM_BUCKETS = 32
MAX_DISTANCE = 128
D_FF = 4 * D_MODEL
RMS_EPS = 1e-6
NEG = -1e30
LANES = 128
VMEM_LIMIT = 56 * 1024 * 1024


def _nt_dot(a, b, precision=None):
    return lax.dot_general(a, b, (((1,), (1,)), ((), ())), preferred_element_type=F32, precision=precision)


def _tn_dot(a, b, precision=None):
    return lax.dot_general(a, b, (((0,), (0,)), ((), ())), preferred_element_type=F32, precision=precision)


def _bucket_upper_bounds():
    n = np.arange(0, MAX_DISTANCE + 1)
    max_exact = NUM_BUCKETS // 2
    nf = np.maximum(n, 1).astype(np.float32)
    large = max_exact + (np.log(nf / np.float32(max_exact)) / np.float32(math.log(MAX_DISTANCE / max_exact))
                         * np.float32(NUM_BUCKETS - max_exact)).astype(np.int32)
    large = np.minimum(large, NUM_BUCKETS - 1)
    bucket = np.where(n < max_exact, n, large)
    return np.array([int(n[bucket <= b].max()) for b in range(NUM_BUCKETS - 1)], dtype=np.int32)


def _bias_of_dist(dist, h, rb_ref, ub_ref):
    def body(i, acc):
        b = NUM_BUCKETS - 2 - i
        return jnp.where(dist <= ub_ref[b], rb_ref[b, h], acc)
    init = jnp.full(dist.shape, rb_ref[NUM_BUCKETS - 1, h], F32)
    return lax.fori_loop(0, NUM_BUCKETS - 1, body, init)


def _bias_tiles_kernel(rb_ref, ub_ref, t0_ref, t1_ref, s_own_ref, s_near_ref, s_far_ref, *, dec_seq):
    h = pl.program_id(0)
    kk = lax.broadcasted_iota(jnp.int32, (MOBA_BLOCK, MOBA_BLOCK), 0)
    qq = lax.broadcasted_iota(jnp.int32, (MOBA_BLOCK, MOBA_BLOCK), 1)
    d0 = qq - kk
    t0_ref[0] = jnp.where(d0 >= 0, _bias_of_dist(jnp.maximum(d0, 0), h, rb_ref, ub_ref), NEG)
    t1_ref[0] = _bias_of_dist(d0 + MOBA_BLOCK, h, rb_ref, ub_ref)
    t = lax.broadcasted_iota(jnp.int32, (dec_seq, LANES), 0)
    cc = lax.broadcasted_iota(jnp.int32, (dec_seq, LANES), 1)
    ds = t - cc
    s_own_ref[0] = jnp.where((ds >= 0) & (cc < dec_seq), _bias_of_dist(jnp.maximum(ds, 0), h, rb_ref, ub_ref), NEG)
    s_near_ref[0] = _bias_of_dist(ds + PAGE_SIZE, h, rb_ref, ub_ref)
    s_far_ref[0] = jnp.full((dec_seq, LANES), rb_ref[NUM_BUCKETS - 1, h], F32)


def _bias_tiles(rel_bias, dec_seq):
    ub = jnp.asarray(_bucket_upper_bounds())
    smem = pl.BlockSpec(memory_space=pltpu.SMEM)
    big = pl.BlockSpec((1, MOBA_BLOCK, MOBA_BLOCK), lambda h: (h, 0, 0))
    small = pl.BlockSpec((1, dec_seq, LANES), lambda h: (h, 0, 0))
    return pl.pallas_call(
        functools.partial(_bias_tiles_kernel, dec_seq=dec_seq),
        grid=(N_ATT_HEADS,),
        in_specs=[smem, smem],
        out_specs=[big, big, small, small, small],
        out_shape=[jax.ShapeDtypeStruct((N_ATT_HEADS, MOBA_BLOCK, MOBA_BLOCK), F32)] * 2
        + [jax.ShapeDtypeStruct((N_ATT_HEADS, dec_seq, LANES), F32)] * 3,
        name="bias_tiles",
    )(rel_bias, ub)


def _rms(x, g):
    return x * lax.rsqrt(jnp.mean(x * x, axis=-1, keepdims=True) + RMS_EPS) * g


def _log_sigmoid(z):
    return jnp.minimum(z, 0.0) - jnp.log1p(jnp.exp(-jnp.abs(z)))


def _proj_kernel(x_ref, g_ref, w_ref, wg_ref, bg_ref, aq_ref, ak_ref, av_ref, mq_ref, mk_ref, mv_ref, mo_ref,
                 gate_ref):
    xn = _rms(x_ref[...], g_ref[...])
    xb = xn.astype(BF16)
    outs = (aq_ref, ak_ref, av_ref, mq_ref, mk_ref, mv_ref, mo_ref)
    scales = (ATT_HEAD_DIM ** -0.5, None, None, None, ML_HEAD_DIM ** -0.5, None, None)
    for i, (o_ref, s) in enumerate(zip(outs, scales)):
        r = jnp.dot(xb, w_ref[:, i * 512:(i + 1) * 512], preferred_element_type=F32)
        o_ref[...] = r if s is None else r * s
    z = jnp.dot(xn, wg_ref[...], preferred_element_type=F32, precision=HIGHEST) + bg_ref[...]
    lane = lax.broadcasted_iota(jnp.int32, z.shape, 1)
    gate_ref[...] = jnp.where((lane >= N_ML_HEADS) & (lane < 2 * N_ML_HEADS), _log_sigmoid(z), z)


def _project(x2, norm_g, w_main, w_gate, b_gate, tm):
    n = x2.shape[0]
    const = lambda shape: pl.BlockSpec(shape, lambda i: (0, 0), pipeline_mode=pl.Buffered(1))
    slab = pl.BlockSpec((tm, 512), lambda i: (i, 0))
    return pl.pallas_call(
        _proj_kernel,
        grid=(n // tm,),
        in_specs=[pl.BlockSpec((tm, D_MODEL), lambda i: (i, 0)), const((1, D_MODEL)),
                  const((D_MODEL, N_SLABS * 512)), const((D_MODEL, LANES)), const((1, LANES))],
        out_specs=[slab] * N_SLABS + [pl.BlockSpec((tm, LANES), lambda i: (i, 0))],
        out_shape=[jax.ShapeDtypeStruct((n, 512), F32)] * N_SLABS + [jax.ShapeDtypeStruct((n, LANES), F32)],
        compiler_params=pltpu.CompilerParams(dimension_semantics=("parallel",), vmem_limit_bytes=VMEM_LIMIT),
        name="rms_in_proj",
    )(x2, norm_g, w_main, w_gate, b_gate)


def _flash_update(s, v, m_ref, l_ref, acc_ref):
    m_prev = m_ref[...]
    m_new = jnp.maximum(m_prev, jnp.max(s, axis=-1, keepdims=True))
    reps = s.shape[1] // LANES
    m_wide = m_new if reps == 1 else jnp.concatenate([m_new] * reps, axis=1)
    p = jnp.exp(s - m_wide)
    alpha = jnp.exp(m_prev - m_new)
    l_ref[...] = alpha * l_ref[...] + jnp.sum(p, axis=-1, keepdims=True)
    m_ref[...] = m_new
    pv = jnp.dot(p.astype(BF16), v, preferred_element_type=F32)
    wreps = acc_ref.shape[1] // LANES
    a_wide = alpha if wreps == 1 else jnp.concatenate([alpha] * wreps, axis=1)
    acc_ref[...] = a_wide * acc_ref[...] + pv


def _topk_penalty(scores, n_valid):
    lane = lax.broadcasted_iota(jnp.int32, scores.shape, 1)

    def body(n, rank):
        col = jnp.sum(jnp.where(lane == n, scores, 0.0), axis=-1, keepdims=True)
        beats = (col > scores) | ((col == scores) & (lane > n))
        return rank + jnp.where(beats, 1, 0)

    rank = lax.fori_loop(0, n_valid, body, jnp.zeros(scores.shape, jnp.int32))
    return jnp.where((rank < MOBA_TOPK) & (lane < n_valid), 0.0, NEG)


def _lane_column(x, n):
    lane = lax.broadcasted_iota(jnp.int32, x.shape, 1)
    return jnp.sum(jnp.where(lane == n, x, 0.0), axis=-1, keepdims=True)


def _moba_prompt_kernel(rb_ref, q_ref, k_ref, v_ref, t0_ref, t1_ref, o_ref, km_ref, kb_ref, vt_ref, sc_ref,
                        pen_ref, s_ref, so_ref, sn_ref, *, n_blocks, group):
    hp = pl.program_id(1)
    own = pl.program_id(2)
    blk = MOBA_BLOCK
    nbp = km_ref.shape[0]

    @pl.when(own == 0)
    def _():
        km_ref[...] = jnp.zeros(km_ref.shape, F32)
        for n in range(n_blocks):
            kn = k_ref[0, n * blk:(n + 1) * blk, :]
            km_ref[n:n + 1, :] = jnp.mean(kn, axis=0, keepdims=True)
            kb_ref[n] = kn.astype(BF16)
            vt_ref[n] = v_ref[0, n * blk:(n + 1) * blk, :].T.astype(BF16)

    q2 = q_ref[0]
    lane = lax.broadcasted_iota(jnp.int32, q2.shape, 1)
    qcat = jnp.concatenate([jnp.where((lane // ATT_HEAD_DIM) == hh, q2, 0.0) for hh in range(2)], axis=0)
    qcat_b = qcat.astype(BF16)
    wide = 2 * blk
    sc_ref[...] = _nt_dot(km_ref[...], qcat, precision=HIGHEST)

    sub = lax.broadcasted_iota(jnp.int32, (nbp, wide), 0)

    def rank_body(n, rank):
        sc = sc_ref[...]
        row = sc_ref[pl.ds(n, 1), :]
        return rank + jnp.where((row > sc) | ((row == sc) & (sub > n)), 1, 0)

    rank = lax.fori_loop(0, own, rank_body, jnp.zeros((nbp, wide), jnp.int32))
    pen_ref[...] = jnp.where((rank < MOBA_TOPK) & (sub < own), 0.0, NEG)

    col_head = lax.broadcasted_iota(jnp.int32, (1, wide), 1) // blk
    far_bias = jnp.where(col_head == 0, rb_ref[NUM_BUCKETS - 1, 2 * hp], rb_ref[NUM_BUCKETS - 1, 2 * hp + 1])

    def fold8(x, op):
        return op(x.reshape(blk // 8, 8, x.shape[1]), axis=0)

    def far_row(j):
        return jnp.where(j < own - 1, pen_ref[pl.ds(j, 1), :] + far_bias, NEG)

    grp = group
    n_groups = (jnp.maximum(own - 1, 0) + grp - 1) // grp
    jn = jnp.maximum(own - 1, 0)
    near_row = jnp.where(own >= 1, pen_ref[pl.ds(jn, 1), :], NEG)

    s_own = _nt_dot(kb_ref[own], qcat_b) + jnp.concatenate([t0_ref[0], t0_ref[1]], axis=1)
    s_near = _nt_dot(kb_ref[jn], qcat_b) + jnp.concatenate([t1_ref[0], t1_ref[1]], axis=1)
    so_ref[...] = s_own
    sn_ref[...] = s_near
    m8 = jnp.maximum(fold8(s_own, jnp.max), fold8(s_near, jnp.max) + near_row)

    def max_body(g, m8c):
        j0 = pl.multiple_of(g * grp, grp)
        sg = _nt_dot(kb_ref[pl.ds(j0, grp)].reshape(grp * blk, LANES), qcat_b)
        s_ref[pl.ds(j0, grp)] = sg.reshape(grp, blk, wide)
        for i in range(grp):
            m8c = jnp.maximum(m8c, fold8(sg[i * blk:(i + 1) * blk], jnp.max) + far_row(j0 + i))
        return m8c

    m8 = lax.fori_loop(0, n_groups, max_body, m8)
    m = jnp.max(m8, axis=0, keepdims=True)

    def accumulate(j, p, l8, accs):
        pb = p.astype(BF16)
        new = tuple(accs[hh] + jnp.dot(vt_ref[j, hh * ATT_HEAD_DIM:(hh + 1) * ATT_HEAD_DIM, :],
                                       pb[:, hh * blk:(hh + 1) * blk], preferred_element_type=F32)
                    for hh in range(2))
        return l8 + fold8(p, jnp.sum), new

    zero_acc = jnp.zeros((ATT_HEAD_DIM, blk), F32)
    l8, accs = accumulate(own, jnp.exp(so_ref[...] - m), jnp.zeros((8, wide), F32), (zero_acc, zero_acc))
    l8, accs = accumulate(jn, jnp.exp(sn_ref[...] + (near_row - m)), l8, accs)

    def exp_body(g, st):
        l8c, accc = st
        j0 = pl.multiple_of(g * grp, grp)
        for i in range(grp):
            l8c, accc = accumulate(j0 + i, jnp.exp(s_ref[j0 + i] + (far_row(j0 + i) - m)), l8c, accc)
        return l8c, accc

    l8, accs = lax.fori_loop(0, n_groups, exp_body, (l8, accs))
    l = jnp.sum(l8, axis=0, keepdims=True)
    out_t = jnp.concatenate([accs[hh] / l[:, hh * blk:(hh + 1) * blk] for hh in range(2)], axis=0)
    o_ref[0] = out_t.T


def _moba_prompt(aq, ak, av, t0, t1, rel_bias):
    b, s, _ = aq.shape
    nb = s // MOBA_BLOCK
    nbp = -(-nb // 8) * 8
    n_pairs = N_ATT_HEADS // 2
    wide = 2 * MOBA_BLOCK
    group = 4
    assert nb % group == 0
    qspec = pl.BlockSpec((1, MOBA_BLOCK, LANES), lambda bi, hp, qi: (bi, qi, hp))
    kvspec = pl.BlockSpec((1, s, LANES), lambda bi, hp, qi: (bi, 0, hp))
    tspec = pl.BlockSpec((2, MOBA_BLOCK, MOBA_BLOCK), lambda bi, hp, qi: (hp, 0, 0))
    return pl.pallas_call(
        functools.partial(_moba_prompt_kernel, n_blocks=nb, group=group),
        grid=(b, n_pairs, nb),
        in_specs=[pl.BlockSpec(memory_space=pltpu.SMEM), qspec, kvspec, kvspec, tspec, tspec],
        out_specs=qspec,
        out_shape=jax.ShapeDtypeStruct((b, s, ATT_WIDTH), F32),
        scratch_shapes=[pltpu.VMEM((nbp, LANES), F32), pltpu.VMEM((nb, MOBA_BLOCK, LANES), BF16),
                        pltpu.VMEM((nb, LANES, MOBA_BLOCK), BF16),
                        pltpu.VMEM((nbp, wide), F32), pltpu.VMEM((nbp, wide), F32),
                        pltpu.VMEM((nb, MOBA_BLOCK, wide), F32), pltpu.VMEM((MOBA_BLOCK, wide), F32),
                        pltpu.VMEM((MOBA_BLOCK, wide), F32)],
        compiler_params=pltpu.CompilerParams(dimension_semantics=("parallel", "parallel", "arbitrary"),
                                             vmem_limit_bytes=VMEM_LIMIT),
        name="moba_prompt",
    )(rel_bias, aq, ak, av, t0, t1)


def _gather_pages_kernel(pt_ref, *refs, pages_per_step):
    pps = pages_per_step
    k_pages, v_pages = refs[:pps], refs[pps:2 * pps]
    kc_ref, vc_ref, ksum_ref = refs[2 * pps:]
    c = pl.program_id(1)
    per_block = MOBA_BLOCK // PAGE_SIZE
    sums = []
    for i in range(pps):
        rows = slice(i * PAGE_SIZE, (i + 1) * PAGE_SIZE)
        kp = k_pages[i][0].reshape(PAGE_SIZE, ATT_WIDTH)
        kc_ref[0, rows, :] = kp.astype(BF16)
        vc_ref[0, rows, :] = v_pages[i][0].reshape(PAGE_SIZE, ATT_WIDTH).astype(BF16)
        sums.append(jnp.sum(kp, axis=0, keepdims=True))
    for i in range(pps // per_block):
        acc = sums[per_block * i]
        for j in range(1, per_block):
            acc = acc + sums[per_block * i + j]
        ksum_ref[0, pl.ds(c * (pps // per_block) + i, 1), :] = acc


def _gather_pages(pool_k, pool_v, page_table, pages_per_step):
    db, n_pages = page_table.shape
    past = n_pages * PAGE_SIZE
    nb = past // MOBA_BLOCK

    def page_spec(i):
        return pl.BlockSpec((1, PAGE_SIZE, N_ATT_HEADS, ATT_HEAD_DIM),
                            lambda b, c, pt: (pt[b, c * pages_per_step + i], 0, 0, 0))

    rows = pages_per_step * PAGE_SIZE
    out_rows = pl.BlockSpec((1, rows, ATT_WIDTH), lambda b, c, pt: (b, c, 0))
    grid_spec = pltpu.PrefetchScalarGridSpec(
        num_scalar_prefetch=1, grid=(db, n_pages // pages_per_step),
        in_specs=[page_spec(i) for i in range(pages_per_step)] * 2,
        out_specs=[out_rows, out_rows, pl.BlockSpec((1, nb, ATT_WIDTH), lambda b, c, pt: (b, 0, 0))])
    return pl.pallas_call(
        functools.partial(_gather_pages_kernel, pages_per_step=pages_per_step),
        grid_spec=grid_spec,
        out_shape=[jax.ShapeDtypeStruct((db, past, ATT_WIDTH), BF16), jax.ShapeDtypeStruct((db, past, ATT_WIDTH), BF16),
                   jax.ShapeDtypeStruct((db, nb, ATT_WIDTH), F32)],
        compiler_params=pltpu.CompilerParams(dimension_semantics=("parallel", "arbitrary"),
                                             vmem_limit_bytes=VMEM_LIMIT),
        name="gather_pages",
    )(page_table, *([pool_k] * pages_per_step), *([pool_v] * pages_per_step))


def _moba_sample_kernel(q_ref, ksum_ref, kn_ref, vn_ref, s_own_ref, s_near_ref, s_far_ref, kc_ref, vc_ref, o_ref,
                        qbd_ref, pen_ref, m_ref, l_ref, acc_ref, *, chunk, n_past, dec_seq):
    c = pl.program_id(1)
    rows = N_ATT_HEADS * dec_seq
    nb_past = n_past // MOBA_BLOCK
    blocks_per_chunk = chunk // MOBA_BLOCK
    row_head = lax.broadcasted_iota(jnp.int32, (rows, ATT_WIDTH), 0) // dec_seq
    lane_head = lax.broadcasted_iota(jnp.int32, (rows, ATT_WIDTH), 1) // ATT_HEAD_DIM

    @pl.when(c == 0)
    def _():
        qrep = jnp.concatenate([q_ref[0]] * N_ATT_HEADS, axis=0)
        qbd = jnp.where(row_head == lane_head, qrep, 0.0)
        qbd_ref[...] = qbd
        kmean = jnp.concatenate([ksum_ref[0] * (1.0 / MOBA_BLOCK),
                                 jnp.zeros((LANES - nb_past, ATT_WIDTH), F32)], axis=0)
        pen_ref[...] = _topk_penalty(_nt_dot(qbd, kmean, precision=HIGHEST), nb_past)
        m_ref[...] = jnp.full(m_ref.shape, NEG, F32)
        l_ref[...] = jnp.zeros(l_ref.shape, F32)
        acc_ref[...] = jnp.zeros(acc_ref.shape, F32)
        pad = jnp.zeros((LANES - dec_seq, ATT_WIDTH), F32)
        kn = jnp.concatenate([kn_ref[0], pad], axis=0).astype(BF16)
        vn = jnp.concatenate([vn_ref[0], pad], axis=0).astype(BF16)
        s_own = s_own_ref[...].reshape(rows, LANES)
        _flash_update(_nt_dot(qbd.astype(BF16), kn) + s_own, vn, m_ref, l_ref, acc_ref)

    qb = qbd_ref[...].astype(BF16)
    s_near = s_near_ref[...].reshape(rows, LANES)
    s_far = s_far_ref[...].reshape(rows, LANES)
    far_col = s_far[:, 0:1]
    pen = pen_ref[...]
    is_last = c == pl.num_programs(1) - 1
    bias = jnp.concatenate([jnp.broadcast_to(far_col, (rows, chunk - PAGE_SIZE)),
                            jnp.where(is_last, s_near, s_far)], axis=1)
    block_pen = jnp.concatenate(
        [jnp.broadcast_to(_lane_column(pen, c * blocks_per_chunk + i), (rows, MOBA_BLOCK))
         for i in range(blocks_per_chunk)], axis=1)
    _flash_update(_nt_dot(qb, kc_ref[0]) + bias + block_pen, vc_ref[0], m_ref, l_ref, acc_ref)

    @pl.when(is_last)
    def _():
        l_wide = jnp.concatenate([l_ref[...]] * (ATT_WIDTH // LANES), axis=1)
        full = jnp.where(row_head == lane_head, acc_ref[...] / l_wide, 0.0)
        out = full[0:dec_seq]
        for h in range(1, N_ATT_HEADS):
            out = out + full[h * dec_seq:(h + 1) * dec_seq]
        o_ref[0] = out


def _moba_sample(aq, ak, av, kc, vc, ksum, s_own, s_near, s_far, chunk):
    db, dec_seq, _ = aq.shape
    n_past = kc.shape[1]
    nb = ksum.shape[1]
    rows = N_ATT_HEADS * dec_seq
    per_b = lambda shape: pl.BlockSpec(shape, lambda b, c: (b, 0, 0))
    tile = pl.BlockSpec((N_ATT_HEADS, dec_seq, LANES), lambda b, c: (0, 0, 0))
    kv = pl.BlockSpec((1, chunk, ATT_WIDTH), lambda b, c: (b, c, 0))
    return pl.pallas_call(
        functools.partial(_moba_sample_kernel, chunk=chunk, n_past=n_past, dec_seq=dec_seq),
        grid=(db, n_past // chunk),
        in_specs=[per_b((1, dec_seq, ATT_WIDTH)), per_b((1, nb, ATT_WIDTH)), per_b((1, dec_seq, ATT_WIDTH)),
                  per_b((1, dec_seq, ATT_WIDTH)), tile, tile, tile, kv, kv],
        out_specs=per_b((1, dec_seq, ATT_WIDTH)),
        out_shape=jax.ShapeDtypeStruct((db, dec_seq, ATT_WIDTH), F32),
        scratch_shapes=[pltpu.VMEM((rows, ATT_WIDTH), F32), pltpu.VMEM((rows, LANES), F32),
                        pltpu.VMEM((rows, LANES), F32), pltpu.VMEM((rows, LANES), F32),
                        pltpu.VMEM((rows, ATT_WIDTH), F32)],
        compiler_params=pltpu.CompilerParams(dimension_semantics=("parallel", "arbitrary"),
                                             vmem_limit_bytes=VMEM_LIMIT),
        name="moba_sample",
    )(aq, ksum, ak, av, s_own, s_near, s_far, kc, vc)


def _mlstm_kernel(q_ref, k_ref, v_ref, g_ref, mo_ref, nw_ref, c0_ref, n0_ref, m0_ref,
                  h_ref, c_out, n_out, m_out, c_s, n_s, m_s, *, rows, chunk):
    ci = pl.program_id(1)
    L = chunk

    @pl.when(ci == 0)
    def _():
        c_s[...] = c0_ref[0]
        n_s[...] = n0_ref[0]
        m_s[...] = m0_ref[0]

    def padded(ref, fill=None):
        x = ref[0]
        if rows == L:
            return x
        tail = jnp.zeros((L - rows, x.shape[1]), F32) if fill is None else fill
        return jnp.concatenate([x, tail], axis=0)

    lane = lax.broadcasted_iota(jnp.int32, (L, LANES), 1)
    pad_lane = lax.broadcasted_iota(jnp.int32, (max(L - rows, 1), LANES), 1)
    g = padded(g_ref, jnp.where(pad_lane < N_ML_HEADS, NEG, 0.0))
    q_all, k_all, v_all, mo_all = padded(q_ref), padded(k_ref), padded(v_ref), padded(mo_ref)

    is_lf = (lane >= N_ML_HEADS) & (lane < 2 * N_ML_HEADS)
    tri_r = lax.broadcasted_iota(jnp.int32, (L, L), 0)
    tri_c = lax.broadcasted_iota(jnp.int32, (L, L), 1)
    causal = tri_c <= tri_r
    bcum = jnp.dot(causal.astype(F32), jnp.where(is_lf, g, 0.0), preferred_element_type=F32, precision=HIGHEST)
    g_t = g.T
    b_t = bcum.T

    for h in range(N_ML_HEADS):
        sl = slice(h * ML_HEAD_DIM, (h + 1) * ML_HEAD_DIM)
        q, k, v = q_all[:, sl], k_all[:, sl], v_all[:, sl]
        ig_col = g[:, h:h + 1]
        b_col = bcum[:, N_ML_HEADS + h:N_ML_HEADS + h + 1]
        ig_row = g_t[h:h + 1, :]
        b_row = b_t[N_ML_HEADS + h:N_ML_HEADS + h + 1, :]
        m_prev = m_s[h][:, 0:1]
        C = c_s[h]
        n_row = n_s[h]

        a_col = b_col + m_prev
        log_d = jnp.where(causal, b_col - b_row + ig_row, NEG)
        m_t = jnp.maximum(a_col, jnp.max(log_d, axis=-1, keepdims=True))
        d = jnp.exp(log_d - m_t)
        w_state = jnp.exp(a_col - m_t)
        qb, kb, vb = q.astype(BF16), k.astype(BF16), v.astype(BF16)
        s = _nt_dot(qb, kb) * d
        num = w_state * jnp.dot(qb, C.astype(BF16), preferred_element_type=F32) \
            + jnp.dot(s.astype(BF16), vb, preferred_element_type=F32)
        den = w_state * jnp.sum(q * n_row, axis=-1, keepdims=True) + jnp.sum(s, axis=-1, keepdims=True)
        hid = num / jnp.maximum(jnp.abs(den), jnp.exp(-m_t))

        hn = _rms(hid, nw_ref[:, sl])
        h_ref[0, :, sl] = (hn * jax.nn.sigmoid(mo_all[:, sl]))[:rows]

        b_last = b_col[L - 1:L, :]
        m_new = m_t[L - 1:L, :]
        g_state = jnp.exp(b_last + m_prev - m_new)
        g_tok = jnp.exp(b_last - b_col + ig_col - m_new)
        kg = k * g_tok
        c_s[h] = g_state * C + _tn_dot(kg.astype(BF16), vb)
        n_s[h] = g_state * n_row + jnp.sum(kg, axis=0, keepdims=True)
        m_s[h] = jnp.broadcast_to(m_new, (1, LANES))

    @pl.when(ci == pl.num_programs(1) - 1)
    def _():
        c_out[0] = c_s[...]
        n_out[0] = n_s[...]
        m_out[0] = m_s[...]


def _mlstm(mq, mk, mv, gates, mo, ml_norm, c0, n0, m0, chunk):
    b, s, _ = mq.shape
    rows = min(s, chunk)
    n_chunks = s // rows
    tok = lambda w: pl.BlockSpec((1, rows, w), lambda bi, ci: (bi, ci, 0))
    st4 = lambda shape: pl.BlockSpec(shape, lambda bi, ci: (bi, 0, 0, 0))
    c_shape, v_shape = (1, N_ML_HEADS, ML_HEAD_DIM, ML_HEAD_DIM), (1, N_ML_HEADS, 1, LANES)
    return pl.pallas_call(
        functools.partial(_mlstm_kernel, rows=rows, chunk=chunk),
        grid=(b, n_chunks),
        in_specs=[tok(ML_WIDTH), tok(ML_WIDTH), tok(ML_WIDTH), tok(LANES), tok(ML_WIDTH),
                  pl.BlockSpec((1, ML_WIDTH), lambda bi, ci: (0, 0)),
                  st4(c_shape), st4(v_shape), st4(v_shape)],
        out_specs=[tok(ML_WIDTH), st4(c_shape), st4(v_shape), st4(v_shape)],
        out_shape=[jax.ShapeDtypeStruct((b, s, ML_WIDTH), F32),
                   jax.ShapeDtypeStruct((b,) + c_shape[1:], F32),
                   jax.ShapeDtypeStruct((b,) + v_shape[1:], F32),
                   jax.ShapeDtypeStruct((b,) + v_shape[1:], F32)],
        scratch_shapes=[pltpu.VMEM(c_shape[1:], F32), pltpu.VMEM(v_shape[1:], F32), pltpu.VMEM(v_shape[1:], F32)],
        compiler_params=pltpu.CompilerParams(dimension_semantics=("parallel", "arbitrary"),
                                             vmem_limit_bytes=VMEM_LIMIT),
        name="mlstm",
    )(mq, mk, mv, gates, mo, ml_norm, c0, n0, m0)


def _out_ffn_kernel(x_ref, att_ref, ml_ref, wo_ref, gf_ref, wu_ref, wd_ref, gl_ref, y_ref, *, ff_chunk):
    x1 = x_ref[...] \
        + jnp.dot(att_ref[...].astype(BF16), wo_ref[0:ATT_WIDTH, :], preferred_element_type=F32) \
        + jnp.dot(ml_ref[...].astype(BF16), wo_ref[ATT_WIDTH:, :], preferred_element_type=F32)
    xb = _rms(x1, gf_ref[...]).astype(BF16)
    ffn = None
    for c in range(D_FF // ff_chunk):
        sl = slice(c * ff_chunk, (c + 1) * ff_chunk)
        u = jnp.maximum(jnp.dot(xb, wu_ref[:, sl], preferred_element_type=F32), 0.0)
        d = jnp.dot((u * u).astype(BF16), wd_ref[sl, :], preferred_element_type=F32)
        ffn = d if ffn is None else ffn + d
    y_ref[...] = _rms(x1 + ffn, gl_ref[...])


def _out_ffn(x2, att, ml, w_out, norm_ffn, w_up, w_down, norm_final, tm, ff_chunk):
    n = x2.shape[0]
    const = lambda shape: pl.BlockSpec(shape, lambda i: (0, 0), pipeline_mode=pl.Buffered(1))
    row = lambda w: pl.BlockSpec((tm, w), lambda i: (i, 0))
    return pl.pallas_call(
        functools.partial(_out_ffn_kernel, ff_chunk=ff_chunk),
        grid=(n // tm,),
        in_specs=[row(D_MODEL), row(ATT_WIDTH), row(ML_WIDTH), const((D_MODEL, D_MODEL)), const((1, D_MODEL)),
                  const((D_MODEL, D_FF)), const((D_FF, D_MODEL)), const((1, D_MODEL))],
        out_specs=row(D_MODEL),
        out_shape=jax.ShapeDtypeStruct((n, D_MODEL), F32),
        compiler_params=pltpu.CompilerParams(dimension_semantics=("parallel",), vmem_limit_bytes=VMEM_LIMIT),
        name="out_proj_ffn",
    )(x2, att, ml, w_out, norm_ffn, w_up, w_down, norm_final)


def _layer_weights(l, norm_mix, w_in, b_ig, b_fg, ml_norm, w_out, norm_ffn, w_up, w_down):
    w_main = w_in[l, :, :GATE_COL0].astype(BF16)
    w_gate = jnp.pad(w_in[l, :, GATE_COL0:], ((0, 0), (0, LANES - 2 * N_ML_HEADS)))
    b_gate = jnp.pad(jnp.concatenate([b_ig[l], b_fg[l]]), (0, LANES - 2 * N_ML_HEADS))[None, :]
    return dict(norm_mix=norm_mix[l][None, :], w_main=w_main, w_gate=w_gate, b_gate=b_gate,
                ml_norm=ml_norm[l][None, :], w_out=w_out[l].astype(BF16), norm_ffn=norm_ffn[l][None, :],
                w_up=w_up[l].astype(BF16), w_down=w_down[l].astype(BF16))


def _trunk_layer(x, w, attend, state0, norm_final, tm, ml_chunk):
    b, s, _ = x.shape
    x2 = x.reshape(b * s, D_MODEL)
    aq, ak, av, mq, mk, mv, mo, gates = _project(x2, w["norm_mix"], w["w_main"], w["w_gate"], w["b_gate"], tm)
    r3 = lambda a: a.reshape(b, s, a.shape[-1])
    att = attend(r3(aq), r3(ak), r3(av))
    c0, n0, m0 = state0
    ml, c_new, n_new, m_new = _mlstm(r3(mq), r3(mk), r3(mv), r3(gates), r3(mo), w["ml_norm"], c0, n0, m0, ml_chunk)
    y = _out_ffn(x2, att.reshape(b * s, ATT_WIDTH), ml.reshape(b * s, ML_WIDTH), w["w_out"], w["norm_ffn"],
                 w["w_up"], w["w_down"], norm_final, tm, 1024)
    return (y.reshape(b, s, D_MODEL), ak.reshape(b, s, N_ATT_HEADS, ATT_HEAD_DIM),
            av.reshape(b, s, N_ATT_HEADS, ATT_HEAD_DIM), c_new, n_new[:, :, 0, :], m_new[:, :, 0, 0])


def kernel(x_prompt, x_sample, cache_k, cache_v, state_C, state_n, state_m, page_table, rel_bias, norm_mix, w_in,
           b_ig, b_fg, ml_norm, w_out, norm_ffn, w_up, w_down, norm_final):
    depth = w_in.shape[0]
    assert depth == 1, "the fused final RMSNorm assumes a single layer"
    bp, sp, _ = x_prompt.shape
    db, dec_seq, _ = x_sample.shape
    n_phys = cache_k.shape[1]
    t0, t1, s_own, s_near, s_far = _bias_tiles(rel_bias, dec_seq)
    gl = norm_final[None, :]
    l = 0
    w = _layer_weights(l, norm_mix, w_in, b_ig, b_fg, ml_norm, w_out, norm_ffn, w_up, w_down)

    zero_state = (jnp.zeros((bp, N_ML_HEADS, ML_HEAD_DIM, ML_HEAD_DIM), F32),
                  jnp.zeros((bp, N_ML_HEADS, 1, LANES), F32), jnp.zeros((bp, N_ML_HEADS, 1, LANES), F32))
    yp, kp, vp, cp, np_, mp = _trunk_layer(
        x_prompt, w, lambda q, k, v: _moba_prompt(q, k, v, t0, t1, rel_bias), zero_state, gl, 512, 256)

    pool_k = cache_k.reshape(depth * n_phys, PAGE_SIZE, N_ATT_HEADS, ATT_HEAD_DIM)
    pool_v = cache_v.reshape(depth * n_phys, PAGE_SIZE, N_ATT_HEADS, ATT_HEAD_DIM)
    pt = page_table + l * n_phys

    def attend_sample(q, k, v):
        kc, vc, ksum = _gather_pages(pool_k, pool_v, pt, 8)
        return _moba_sample(q, k, v, kc, vc, ksum, s_own, s_near, s_far, 1024)

    state0 = (state_C[l], state_n[l][:, :, None, :],
              jnp.broadcast_to(state_m[l][:, :, None, None], (db, N_ML_HEADS, 1, LANES)))
    ys, ks, vs, cs, ns, ms = _trunk_layer(x_sample, w, attend_sample, state0, gl, 256, 128)

    st = lambda a: a[None]
    return (yp, ys, st(kp), st(vp), st(cp), st(np_), st(mp), st(ks), st(vs), st(cs), st(ns), st(ms))
```

```python
import functools
import math

import numpy as np
import jax
import jax.numpy as jnp
from jax import lax
from jax.experimental import pallas as pl
from jax.experimental.pallas import tpu as pltpu

F32 = jnp.float32
BF16 = jnp.bfloat16
HIGHEST = lax.Precision.HIGHEST

D_MODEL = 1024
ATT_HEAD_DIM = 64
N_ATT_HEADS = 8
ATT_WIDTH = 512
N_ML_HEADS = 4
ML_HEAD_DIM = 128
ML_WIDTH = 512
N_SLABS = 7
GATE_COL0 = N_SLABS * 512
MOBA_BLOCK = 256
MOBA_TOPK = 3
PAGE_SIZE = 128
NUM_BUCKETS = 32
MAX_DISTANCE = 128
D_FF = 4 * D_MODEL
RMS_EPS = 1e-6
NEG = -1e30
LANES = 128
VMEM_LIMIT = 56 * 1024 * 1024


def _nt_dot(a, b, precision=None):
    return lax.dot_general(a, b, (((1,), (1,)), ((), ())), preferred_element_type=F32, precision=precision)


def _tn_dot(a, b, precision=None):
    return lax.dot_general(a, b, (((0,), (0,)), ((), ())), preferred_element_type=F32, precision=precision)


def _bucket_upper_bounds():
    n = np.arange(0, MAX_DISTANCE + 1)
    max_exact = NUM_BUCKETS // 2
    nf = np.maximum(n, 1).astype(np.float32)
    large = max_exact + (np.log(nf / np.float32(max_exact)) / np.float32(math.log(MAX_DISTANCE / max_exact))
                         * np.float32(NUM_BUCKETS - max_exact)).astype(np.int32)
    large = np.minimum(large, NUM_BUCKETS - 1)
    bucket = np.where(n < max_exact, n, large)
    return np.array([int(n[bucket <= b].max()) for b in range(NUM_BUCKETS - 1)], dtype=np.int32)


def _bias_of_dist(dist, h, rb_ref, ub_ref):
    def body(i, acc):
        b = NUM_BUCKETS - 2 - i
        return jnp.where(dist <= ub_ref[b], rb_ref[b, h], acc)
    init = jnp.full(dist.shape, rb_ref[NUM_BUCKETS - 1, h], F32)
    return lax.fori_loop(0, NUM_BUCKETS - 1, body, init)


def _bias_tiles_kernel(rb_ref, ub_ref, t0_ref, t1_ref, s_own_ref, s_near_ref, s_far_ref, *, dec_seq):
    h = pl.program_id(0)
    kk = lax.broadcasted_iota(jnp.int32, (MOBA_BLOCK, MOBA_BLOCK), 0)
    qq = lax.broadcasted_iota(jnp.int32, (MOBA_BLOCK, MOBA_BLOCK), 1)
    d0 = qq - kk
    t0_ref[0] = jnp.where(d0 >= 0, _bias_of_dist(jnp.maximum(d0, 0), h, rb_ref, ub_ref), NEG)
    t1_ref[0] = _bias_of_dist(d0 + MOBA_BLOCK, h, rb_ref, ub_ref)
    t = lax.broadcasted_iota(jnp.int32, (dec_seq, LANES), 0)
    cc = lax.broadcasted_iota(jnp.int32, (dec_seq, LANES), 1)
    ds = t - cc
    s_own_ref[0] = jnp.where((ds >= 0) & (cc < dec_seq), _bias_of_dist(jnp.maximum(ds, 0), h, rb_ref, ub_ref), NEG)
    s_near_ref[0] = _bias_of_dist(ds + PAGE_SIZE, h, rb_ref, ub_ref)
    s_far_ref[0] = jnp.full((dec_seq, LANES), rb_ref[NUM_BUCKETS - 1, h], F32)


def _bias_tiles(rel_bias, dec_seq):
    ub = jnp.asarray(_bucket_upper_bounds())
    smem = pl.BlockSpec(memory_space=pltpu.SMEM)
    big = pl.BlockSpec((1, MOBA_BLOCK, MOBA_BLOCK), lambda h: (h, 0, 0))
    small = pl.BlockSpec((1, dec_seq, LANES), lambda h: (h, 0, 0))
    return pl.pallas_call(
        functools.partial(_bias_tiles_kernel, dec_seq=dec_seq),
        grid=(N_ATT_HEADS,),
        in_specs=[smem, smem],
        out_specs=[big, big, small, small, small],
        out_shape=[jax.ShapeDtypeStruct((N_ATT_HEADS, MOBA_BLOCK, MOBA_BLOCK), F32)] * 2
        + [jax.ShapeDtypeStruct((N_ATT_HEADS, dec_seq, LANES), F32)] * 3,
        name="bias_tiles",
    )(rel_bias, ub)


def _rms(x, g):
    return x * lax.rsqrt(jnp.mean(x * x, axis=-1, keepdims=True) + RMS_EPS) * g


def _log_sigmoid(z):
    return jnp.minimum(z, 0.0) - jnp.log1p(jnp.exp(-jnp.abs(z)))


def _proj_kernel(x_ref, g_ref, w_ref, wg_ref, bg_ref, aq_ref, ak_ref, av_ref, mq_ref, mk_ref, mv_ref, mo_ref,
                 gate_ref):
    xn = _rms(x_ref[...], g_ref[...])
    xb = xn.astype(BF16)
    outs = (aq_ref, ak_ref, av_ref, mq_ref, mk_ref, mv_ref, mo_ref)
    scales = (ATT_HEAD_DIM ** -0.5, None, None, None, ML_HEAD_DIM ** -0.5, None, None)
    for i, (o_ref, s) in enumerate(zip(outs, scales)):
        r = jnp.dot(xb, w_ref[:, i * 512:(i + 1) * 512], preferred_element_type=F32)
        o_ref[...] = r if s is None else r * s
    z = jnp.dot(xn, wg_ref[...], preferred_element_type=F32, precision=HIGHEST) + bg_ref[...]
    lane = lax.broadcasted_iota(jnp.int32, z.shape, 1)
    gate_ref[...] = jnp.where((lane >= N_ML_HEADS) & (lane < 2 * N_ML_HEADS), _log_sigmoid(z), z)


def _project(x2, norm_g, w_main, w_gate, b_gate, tm):
    n = x2.shape[0]
    const = lambda shape: pl.BlockSpec(shape, lambda i: (0, 0), pipeline_mode=pl.Buffered(1))
    slab = pl.BlockSpec((tm, 512), lambda i: (i, 0))
    return pl.pallas_call(
        _proj_kernel,
        grid=(n // tm,),
        in_specs=[pl.BlockSpec((tm, D_MODEL), lambda i: (i, 0)), const((1, D_MODEL)),
                  const((D_MODEL, N_SLABS * 512)), const((D_MODEL, LANES)), const((1, LANES))],
        out_specs=[slab] * N_SLABS + [pl.BlockSpec((tm, LANES), lambda i: (i, 0))],
        out_shape=[jax.ShapeDtypeStruct((n, 512), F32)] * N_SLABS + [jax.ShapeDtypeStruct((n, LANES), F32)],
        compiler_params=pltpu.CompilerParams(dimension_semantics=("parallel",), vmem_limit_bytes=VMEM_LIMIT),
        name="rms_in_proj",
    )(x2, norm_g, w_main, w_gate, b_gate)


def _flash_update(s, v, m_ref, l_ref, acc_ref):
    m_prev = m_ref[...]
    m_new = jnp.maximum(m_prev, jnp.max(s, axis=-1, keepdims=True))
    reps = s.shape[1] // LANES
    m_wide = m_new if reps == 1 else jnp.concatenate([m_new] * reps, axis=1)
    p = jnp.exp(s - m_wide)
    alpha = jnp.exp(m_prev - m_new)
    l_ref[...] = alpha * l_ref[...] + jnp.sum(p, axis=-1, keepdims=True)
    m_ref[...] = m_new
    pv = jnp.dot(p.astype(BF16), v, preferred_element_type=F32)
    wreps = acc_ref.shape[1] // LANES
    a_wide = alpha if wreps == 1 else jnp.concatenate([alpha] * wreps, axis=1)
    acc_ref[...] = a_wide * acc_ref[...] + pv


def _topk_penalty(scores, n_valid):
    lane = lax.broadcasted_iota(jnp.int32, scores.shape, 1)

    def body(n, rank):
        col = jnp.sum(jnp.where(lane == n, scores, 0.0), axis=-1, keepdims=True)
        beats = (col > scores) | ((col == scores) & (lane > n))
        return rank + jnp.where(beats, 1, 0)

    rank = lax.fori_loop(0, n_valid, body, jnp.zeros(scores.shape, jnp.int32))
    return jnp.where((rank < MOBA_TOPK) & (lane < n_valid), 0.0, NEG)


def _lane_column(x, n):
    lane = lax.broadcasted_iota(jnp.int32, x.shape, 1)
    return jnp.sum(jnp.where(lane == n, x, 0.0), axis=-1, keepdims=True)


def _moba_prompt_kernel(rb_ref, q_ref, k_ref, v_ref, t0_ref, t1_ref, o_ref, km_ref, kb_ref, vt_ref, sc_ref,
                        pen_ref, s_ref, so_ref, sn_ref, *, n_blocks, group):
    hp = pl.program_id(1)
    own = pl.program_id(2)
    blk = MOBA_BLOCK
    nbp = km_ref.shape[0]

    @pl.when(own == 0)
    def _():
        km_ref[...] = jnp.zeros(km_ref.shape, F32)
        for n in range(n_blocks):
            kn = k_ref[0, n * blk:(n + 1) * blk, :]
            km_ref[n:n + 1, :] = jnp.mean(kn, axis=0, keepdims=True)
            kb_ref[n] = kn.astype(BF16)
            vt_ref[n] = v_ref[0, n * blk:(n + 1) * blk, :].T.astype(BF16)

    q2 = q_ref[0]
    lane = lax.broadcasted_iota(jnp.int32, q2.shape, 1)
    qcat = jnp.concatenate([jnp.where((lane // ATT_HEAD_DIM) == hh, q2, 0.0) for hh in range(2)], axis=0)
    qcat_b = qcat.astype(BF16)
    wide = 2 * blk
    sc_ref[...] = _nt_dot(km_ref[...], qcat, precision=HIGHEST)

    sub = lax.broadcasted_iota(jnp.int32, (nbp, wide), 0)

    def rank_body(n, rank):
        sc = sc_ref[...]
        row = sc_ref[pl.ds(n, 1), :]
        return rank + jnp.where((row > sc) | ((row == sc) & (sub > n)), 1, 0)

    rank = lax.fori_loop(0, own, rank_body, jnp.zeros((nbp, wide), jnp.int32))
    pen_ref[...] = jnp.where((rank < MOBA_TOPK) & (sub < own), 0.0, NEG)

    col_head = lax.broadcasted_iota(jnp.int32, (1, wide), 1) // blk
    far_bias = jnp.where(col_head == 0, rb_ref[NUM_BUCKETS - 1, 2 * hp], rb_ref[NUM_BUCKETS - 1, 2 * hp + 1])

    def fold8(x, op):
        return op(x.reshape(blk // 8, 8, x.shape[1]), axis=0)

    def far_row(j):
        return jnp.where(j < own - 1, pen_ref[pl.ds(j, 1), :] + far_bias, NEG)

    grp = group
    n_groups = (jnp.maximum(own - 1, 0) + grp - 1) // grp
    jn = jnp.maximum(own - 1, 0)
    near_row = jnp.where(own >= 1, pen_ref[pl.ds(jn, 1), :], NEG)

    s_own = _nt_dot(kb_ref[own], qcat_b) + jnp.concatenate([t0_ref[0], t0_ref[1]], axis=1)
    s_near = _nt_dot(kb_ref[jn], qcat_b) + jnp.concatenate([t1_ref[0], t1_ref[1]], axis=1)
    so_ref[...] = s_own
    sn_ref[...] = s_near
    m8 = jnp.maximum(fold8(s_own, jnp.max), fold8(s_near, jnp.max) + near_row)

    def max_body(g, m8c):
        j0 = pl.multiple_of(g * grp, grp)
        sg = _nt_dot(kb_ref[pl.ds(j0, grp)].reshape(grp * blk, LANES), qcat_b)
        s_ref[pl.ds(j0, grp)] = sg.reshape(grp, blk, wide)
        for i in range(grp):
            m8c = jnp.maximum(m8c, fold8(sg[i * blk:(i + 1) * blk], jnp.max) + far_row(j0 + i))
        return m8c

    m8 = lax.fori_loop(0, n_groups, max_body, m8)
    m = jnp.max(m8, axis=0, keepdims=True)

    def accumulate(j, p, l8, accs):
        pb = p.astype(BF16)
        new = tuple(accs[hh] + jnp.dot(vt_ref[j, hh * ATT_HEAD_DIM:(hh + 1) * ATT_HEAD_DIM, :],
                                       pb[:, hh * blk:(hh + 1) * blk], preferred_element_type=F32)
                    for hh in range(2))
        return l8 + fold8(p, jnp.sum), new

    zero_acc = jnp.zeros((ATT_HEAD_DIM, blk), F32)
    l8, accs = accumulate(own, jnp.exp(so_ref[...] - m), jnp.zeros((8, wide), F32), (zero_acc, zero_acc))
    l8, accs = accumulate(jn, jnp.exp(sn_ref[...] + (near_row - m)), l8, accs)

    def exp_body(g, st):
        l8c, accc = st
        j0 = pl.multiple_of(g * grp, grp)
        for i in range(grp):
            l8c, accc = accumulate(j0 + i, jnp.exp(s_ref[j0 + i] + (far_row(j0 + i) - m)), l8c, accc)
        return l8c, accc

    l8, accs = lax.fori_loop(0, n_groups, exp_body, (l8, accs))
    l = jnp.sum(l8, axis=0, keepdims=True)
    out_t = jnp.concatenate([accs[hh] / l[:, hh * blk:(hh + 1) * blk] for hh in range(2)], axis=0)
    o_ref[0] = out_t.T


def _moba_prompt(aq, ak, av, t0, t1, rel_bias):
    b, s, _ = aq.shape
    nb = s // MOBA_BLOCK
    nbp = -(-nb // 8) * 8
    n_pairs = N_ATT_HEADS // 2
    wide = 2 * MOBA_BLOCK
    group = 4
    assert nb % group == 0
    qspec = pl.BlockSpec((1, MOBA_BLOCK, LANES), lambda bi, hp, qi: (bi, qi, hp))
    kvspec = pl.BlockSpec((1, s, LANES), lambda bi, hp, qi: (bi, 0, hp))
    tspec = pl.BlockSpec((2, MOBA_BLOCK, MOBA_BLOCK), lambda bi, hp, qi: (hp, 0, 0))
    return pl.pallas_call(
        functools.partial(_moba_prompt_kernel, n_blocks=nb, group=group),
        grid=(b, n_pairs, nb),
        in_specs=[pl.BlockSpec(memory_space=pltpu.SMEM), qspec, kvspec, kvspec, tspec, tspec],
        out_specs=qspec,
        out_shape=jax.ShapeDtypeStruct((b, s, ATT_WIDTH), F32),
        scratch_shapes=[pltpu.VMEM((nbp, LANES), F32), pltpu.VMEM((nb, MOBA_BLOCK, LANES), BF16),
                        pltpu.VMEM((nb, LANES, MOBA_BLOCK), BF16),
                        pltpu.VMEM((nbp, wide), F32), pltpu.VMEM((nbp, wide), F32),
                        pltpu.VMEM((nb, MOBA_BLOCK, wide), F32), pltpu.VMEM((MOBA_BLOCK, wide), F32),
                        pltpu.VMEM((MOBA_BLOCK, wide), F32)],
        compiler_params=pltpu.CompilerParams(dimension_semantics=("parallel", "parallel", "arbitrary"),
                                             vmem_limit_bytes=VMEM_LIMIT),
        name="moba_prompt",
    )(rel_bias, aq, ak, av, t0, t1)


def _moba_sample_kernel(pt_ref, *refs, blocks_per_step, n_blocks, dec_seq):
    ppb = MOBA_BLOCK // PAGE_SIZE
    pps = blocks_per_step * ppb
    k_pages, v_pages = refs[:pps], refs[pps:2 * pps]
    (q_ref, kn_ref, vn_ref, s_own_ref, s_near_ref, s_far_ref, o_ref,
     qbd_ref, kmt_ref, mm_ref, ll_ref, oo_ref, own_m_ref, own_l_ref, own_o_ref) = refs[2 * pps:]
    c = pl.program_id(1)
    rows = N_ATT_HEADS * dec_seq
    row_head = lax.broadcasted_iota(jnp.int32, (rows, ATT_WIDTH), 0) // dec_seq
    lane_head = lax.broadcasted_iota(jnp.int32, (rows, ATT_WIDTH), 1) // ATT_HEAD_DIM
    lane_r = lax.broadcasted_iota(jnp.int32, (rows, LANES), 1)
    lane_k = lax.broadcasted_iota(jnp.int32, (ATT_WIDTH, LANES), 1)

    def local_softmax(s):
        m = jnp.max(s, axis=-1, keepdims=True)
        p = jnp.exp(s - m)
        return m, jnp.sum(p, axis=-1, keepdims=True), p.astype(BF16)

    @pl.when(c == 0)
    def _():
        qrep = jnp.concatenate([q_ref[0]] * N_ATT_HEADS, axis=0)
        qbd = jnp.where(row_head == lane_head, qrep, 0.0)
        qbd_ref[...] = qbd
        kmt_ref[...] = jnp.zeros(kmt_ref.shape, F32)
        mm_ref[...] = jnp.zeros(mm_ref.shape, F32)
        ll_ref[...] = jnp.zeros(ll_ref.shape, F32)
        pad = jnp.zeros((LANES - dec_seq, ATT_WIDTH), F32)
        kn = jnp.concatenate([kn_ref[0], pad], axis=0).astype(BF16)
        vn = jnp.concatenate([vn_ref[0], pad], axis=0).astype(BF16)
        m, l, p = local_softmax(_nt_dot(qbd.astype(BF16), kn) + s_own_ref[...].reshape(rows, LANES))
        own_m_ref[...] = jnp.broadcast_to(m, (rows, LANES))
        own_l_ref[...] = jnp.broadcast_to(l, (rows, LANES))
        own_o_ref[...] = jnp.dot(p, vn, preferred_element_type=F32)

    qb = qbd_ref[...].astype(BF16)
    s_near = s_near_ref[...].reshape(rows, LANES)
    s_far = s_far_ref[...].reshape(rows, LANES)
    for g in range(blocks_per_step):
        n = c * blocks_per_step + g
        kt = jnp.concatenate([k_pages[ppb * g + i][0] for i in range(ppb)], axis=1)
        vt = jnp.concatenate([v_pages[ppb * g + i][0] for i in range(ppb)], axis=1).astype(BF16)
        kmt_ref[...] = jnp.where(lane_k == n, jnp.sum(kt, axis=-1, keepdims=True), kmt_ref[...])
        bias = jnp.concatenate([s_far] * (ppb - 1) + [jnp.where(n == n_blocks - 1, s_near, s_far)], axis=1)
        m, l, p = local_softmax(jnp.dot(qb, kt.astype(BF16), preferred_element_type=F32) + bias)
        oo_ref[n] = _nt_dot(p, vt)
        mm_ref[...] = jnp.where(lane_r == n, m, mm_ref[...])
        ll_ref[...] = jnp.where(lane_r == n, l, ll_ref[...])

    @pl.when(c == pl.num_programs(1) - 1)
    def _():
        kmean_t = kmt_ref[...] * (1.0 / MOBA_BLOCK)
        pen = _topk_penalty(jnp.dot(qbd_ref[...], kmean_t, preferred_element_type=F32, precision=HIGHEST), n_blocks)
        mp = mm_ref[...] + pen
        m_own = own_m_ref[...]
        m_tot = jnp.maximum(jnp.max(mp, axis=-1, keepdims=True), m_own)
        w = jnp.exp(mp - m_tot)
        w_own = jnp.exp(m_own - m_tot)
        den = jnp.sum(w * ll_ref[...], axis=-1, keepdims=True) + w_own * own_l_ref[...]

        def merge(nn, acc):
            return acc + _lane_column(w, nn) * oo_ref[nn]

        num = lax.fori_loop(0, n_blocks, merge, w_own[:, 0:1] * own_o_ref[...])
        full = jnp.where(row_head == lane_head, num / den[:, 0:1], 0.0)
        out = full[0:dec_seq]
        for h in range(1, N_ATT_HEADS):
            out = out + full[h * dec_seq:(h + 1) * dec_seq]
        o_ref[0] = out


def _moba_sample(aq, ak, av, pool_kt, pool_vt, page_table, s_own, s_near, s_far, blocks_per_step):
    db, dec_seq, _ = aq.shape
    n_pages = page_table.shape[1]
    ppb = MOBA_BLOCK // PAGE_SIZE
    nb = n_pages // ppb
    pps = blocks_per_step * ppb
    assert nb % blocks_per_step == 0 and nb <= LANES
    rows = N_ATT_HEADS * dec_seq

    def page_spec(i):
        return pl.BlockSpec((1, ATT_WIDTH, PAGE_SIZE), lambda b, c, pt: (pt[b, c * pps + i], 0, 0))

    per_b = pl.BlockSpec((1, dec_seq, ATT_WIDTH), lambda b, c, pt: (b, 0, 0))
    tile = pl.BlockSpec((N_ATT_HEADS, dec_seq, LANES), lambda b, c, pt: (0, 0, 0))
    grid_spec = pltpu.PrefetchScalarGridSpec(
        num_scalar_prefetch=1, grid=(db, nb // blocks_per_step),
        in_specs=[page_spec(i) for i in range(pps)] * 2 + [per_b, per_b, per_b, tile, tile, tile],
        out_specs=per_b,
        scratch_shapes=[pltpu.VMEM((rows, ATT_WIDTH), F32), pltpu.VMEM((ATT_WIDTH, LANES), F32),
                        pltpu.VMEM((rows, LANES), F32), pltpu.VMEM((rows, LANES), F32),
                        pltpu.VMEM((nb, rows, ATT_WIDTH), F32), pltpu.VMEM((rows, LANES), F32),
                        pltpu.VMEM((rows, LANES), F32), pltpu.VMEM((rows, ATT_WIDTH), F32)])
    return pl.pallas_call(
        functools.partial(_moba_sample_kernel, blocks_per_step=blocks_per_step, n_blocks=nb, dec_seq=dec_seq),
        grid_spec=grid_spec,
        out_shape=jax.ShapeDtypeStruct((db, dec_seq, ATT_WIDTH), F32),
        compiler_params=pltpu.CompilerParams(dimension_semantics=("parallel", "arbitrary"),
                                             vmem_limit_bytes=VMEM_LIMIT),
        name="moba_sample",
    )(page_table, *([pool_kt] * pps), *([pool_vt] * pps), aq, ak, av, s_own, s_near, s_far)


def _mlstm_kernel(q_ref, k_ref, v_ref, g_ref, mo_ref, nw_ref, c0_ref, n0_ref, m0_ref,
                  h_ref, c_out, n_out, m_out, c_s, n_s, m_s, *, rows, chunk):
    ci = pl.program_id(1)
    L = chunk

    @pl.when(ci == 0)
    def _():
        c_s[...] = c0_ref[0]
        n_s[...] = n0_ref[0]
        m_s[...] = m0_ref[0]

    def padded(ref, fill=None):
        x = ref[0]
        if rows == L:
            return x
        tail = jnp.zeros((L - rows, x.shape[1]), F32) if fill is None else fill
        return jnp.concatenate([x, tail], axis=0)

    lane = lax.broadcasted_iota(jnp.int32, (L, LANES), 1)
    pad_lane = lax.broadcasted_iota(jnp.int32, (max(L - rows, 1), LANES), 1)
    g = padded(g_ref, jnp.where(pad_lane < N_ML_HEADS, NEG, 0.0))
    q_all, k_all, v_all, mo_all = padded(q_ref), padded(k_ref), padded(v_ref), padded(mo_ref)

    is_lf = (lane >= N_ML_HEADS) & (lane < 2 * N_ML_HEADS)
    tri_r = lax.broadcasted_iota(jnp.int32, (L, L), 0)
    tri_c = lax.broadcasted_iota(jnp.int32, (L, L), 1)
    causal = tri_c <= tri_r
    bcum = jnp.dot(causal.astype(F32), jnp.where(is_lf, g, 0.0), preferred_element_type=F32, precision=HIGHEST)
    g_t = g.T
    b_t = bcum.T

    for h in range(N_ML_HEADS):
        sl = slice(h * ML_HEAD_DIM, (h + 1) * ML_HEAD_DIM)
        q, k, v = q_all[:, sl], k_all[:, sl], v_all[:, sl]
        ig_col = g[:, h:h + 1]
        b_col = bcum[:, N_ML_HEADS + h:N_ML_HEADS + h + 1]
        ig_row = g_t[h:h + 1, :]
        b_row = b_t[N_ML_HEADS + h:N_ML_HEADS + h + 1, :]
        m_prev = m_s[h][:, 0:1]
        C = c_s[h]
        n_row = n_s[h]

        a_col = b_col + m_prev
        log_d = jnp.where(causal, b_col - b_row + ig_row, NEG)
        m_t = jnp.maximum(a_col, jnp.max(log_d, axis=-1, keepdims=True))
        d = jnp.exp(log_d - m_t)
        w_state = jnp.exp(a_col - m_t)
        qb, kb, vb = q.astype(BF16), k.astype(BF16), v.astype(BF16)
        s = _nt_dot(qb, kb) * d
        num = w_state * jnp.dot(qb, C.astype(BF16), preferred_element_type=F32) \
            + jnp.dot(s.astype(BF16), vb, preferred_element_type=F32)
        den = w_state * jnp.sum(q * n_row, axis=-1, keepdims=True) + jnp.sum(s, axis=-1, keepdims=True)
        hid = num / jnp.maximum(jnp.abs(den), jnp.exp(-m_t))

        hn = _rms(hid, nw_ref[:, sl])
        h_ref[0, :, sl] = (hn * jax.nn.sigmoid(mo_all[:, sl]))[:rows]

        b_last = b_col[L - 1:L, :]
        m_new = m_t[L - 1:L, :]
        g_state = jnp.exp(b_last + m_prev - m_new)
        g_tok = jnp.exp(b_last - b_col + ig_col - m_new)
        kg = k * g_tok
        c_s[h] = g_state * C + _tn_dot(kg.astype(BF16), vb)
        n_s[h] = g_state * n_row + jnp.sum(kg, axis=0, keepdims=True)
        m_s[h] = jnp.broadcast_to(m_new, (1, LANES))

    @pl.when(ci == pl.num_programs(1) - 1)
    def _():
        c_out[0] = c_s[...]
        n_out[0] = n_s[...]
        m_out[0] = m_s[...]


def _mlstm(mq, mk, mv, gates, mo, ml_norm, c0, n0, m0, chunk):
    b, s, _ = mq.shape
    rows = min(s, chunk)
    n_chunks = s // rows
    tok = lambda w: pl.BlockSpec((1, rows, w), lambda bi, ci: (bi, ci, 0))
    st4 = lambda shape: pl.BlockSpec(shape, lambda bi, ci: (bi, 0, 0, 0))
    c_shape, v_shape = (1, N_ML_HEADS, ML_HEAD_DIM, ML_HEAD_DIM), (1, N_ML_HEADS, 1, LANES)
    return pl.pallas_call(
        functools.partial(_mlstm_kernel, rows=rows, chunk=chunk),
        grid=(b, n_chunks),
        in_specs=[tok(ML_WIDTH), tok(ML_WIDTH), tok(ML_WIDTH), tok(LANES), tok(ML_WIDTH),
                  pl.BlockSpec((1, ML_WIDTH), lambda bi, ci: (0, 0)),
                  st4(c_shape), st4(v_shape), st4(v_shape)],
        out_specs=[tok(ML_WIDTH), st4(c_shape), st4(v_shape), st4(v_shape)],
        out_shape=[jax.ShapeDtypeStruct((b, s, ML_WIDTH), F32),
                   jax.ShapeDtypeStruct((b,) + c_shape[1:], F32),
                   jax.ShapeDtypeStruct((b,) + v_shape[1:], F32),
                   jax.ShapeDtypeStruct((b,) + v_shape[1:], F32)],
        scratch_shapes=[pltpu.VMEM(c_shape[1:], F32), pltpu.VMEM(v_shape[1:], F32), pltpu.VMEM(v_shape[1:], F32)],
        compiler_params=pltpu.CompilerParams(dimension_semantics=("parallel", "arbitrary"),
                                             vmem_limit_bytes=VMEM_LIMIT),
        name="mlstm",
    )(mq, mk, mv, gates, mo, ml_norm, c0, n0, m0)


def _out_ffn_kernel(x_ref, att_ref, ml_ref, wo_ref, gf_ref, wu_ref, wd_ref, gl_ref, y_ref, *, ff_chunk):
    x1 = x_ref[...] \
        + jnp.dot(att_ref[...].astype(BF16), wo_ref[0:ATT_WIDTH, :], preferred_element_type=F32) \
        + jnp.dot(ml_ref[...].astype(BF16), wo_ref[ATT_WIDTH:, :], preferred_element_type=F32)
    xb = _rms(x1, gf_ref[...]).astype(BF16)
    ffn = None
    for c in range(D_FF // ff_chunk):
        sl = slice(c * ff_chunk, (c + 1) * ff_chunk)
        u = jnp.maximum(jnp.dot(xb, wu_ref[:, sl], preferred_element_type=F32), 0.0)
        d = jnp.dot((u * u).astype(BF16), wd_ref[sl, :], preferred_element_type=F32)
        ffn = d if ffn is None else ffn + d
    y_ref[...] = _rms(x1 + ffn, gl_ref[...])


def _out_ffn(x2, att, ml, w_out, norm_ffn, w_up, w_down, norm_final, tm, ff_chunk):
    n = x2.shape[0]
    const = lambda shape: pl.BlockSpec(shape, lambda i: (0, 0), pipeline_mode=pl.Buffered(1))
    row = lambda w: pl.BlockSpec((tm, w), lambda i: (i, 0))
    return pl.pallas_call(
        functools.partial(_out_ffn_kernel, ff_chunk=ff_chunk),
        grid=(n // tm,),
        in_specs=[row(D_MODEL), row(ATT_WIDTH), row(ML_WIDTH), const((D_MODEL, D_MODEL)), const((1, D_MODEL)),
                  const((D_MODEL, D_FF)), const((D_FF, D_MODEL)), const((1, D_MODEL))],
        out_specs=row(D_MODEL),
        out_shape=jax.ShapeDtypeStruct((n, D_MODEL), F32),
        compiler_params=pltpu.CompilerParams(dimension_semantics=("parallel",), vmem_limit_bytes=VMEM_LIMIT),
        name="out_proj_ffn",
    )(x2, att, ml, w_out, norm_ffn, w_up, w_down, norm_final)


def _layer_weights(l, norm_mix, w_in, b_ig, b_fg, ml_norm, w_out, norm_ffn, w_up, w_down):
    w_main = w_in[l, :, :GATE_COL0].astype(BF16)
    w_gate = jnp.pad(w_in[l, :, GATE_COL0:], ((0, 0), (0, LANES - 2 * N_ML_HEADS)))
    b_gate = jnp.pad(jnp.concatenate([b_ig[l], b_fg[l]]), (0, LANES - 2 * N_ML_HEADS))[None, :]
    return dict(norm_mix=norm_mix[l][None, :], w_main=w_main, w_gate=w_gate, b_gate=b_gate,
                ml_norm=ml_norm[l][None, :], w_out=w_out[l].astype(BF16), norm_ffn=norm_ffn[l][None, :],
                w_up=w_up[l].astype(BF16), w_down=w_down[l].astype(BF16))


def _trunk_layer(x, w, attend, state0, norm_final, tm, ml_chunk):
    b, s, _ = x.shape
    x2 = x.reshape(b * s, D_MODEL)
    aq, ak, av, mq, mk, mv, mo, gates = _project(x2, w["norm_mix"], w["w_main"], w["w_gate"], w["b_gate"], tm)
    r3 = lambda a: a.reshape(b, s, a.shape[-1])
    att = attend(r3(aq), r3(ak), r3(av))
    c0, n0, m0 = state0
    ml, c_new, n_new, m_new = _mlstm(r3(mq), r3(mk), r3(mv), r3(gates), r3(mo), w["ml_norm"], c0, n0, m0, ml_chunk)
    y = _out_ffn(x2, att.reshape(b * s, ATT_WIDTH), ml.reshape(b * s, ML_WIDTH), w["w_out"], w["norm_ffn"],
                 w["w_up"], w["w_down"], norm_final, tm, 1024)
    return (y.reshape(b, s, D_MODEL), ak.reshape(b, s, N_ATT_HEADS, ATT_HEAD_DIM),
            av.reshape(b, s, N_ATT_HEADS, ATT_HEAD_DIM), c_new, n_new[:, :, 0, :], m_new[:, :, 0, 0])


def kernel(x_prompt, x_sample, cache_k, cache_v, state_C, state_n, state_m, page_table, rel_bias, norm_mix, w_in,
           b_ig, b_fg, ml_norm, w_out, norm_ffn, w_up, w_down, norm_final):
    depth = w_in.shape[0]
    assert depth == 1, "the fused final RMSNorm assumes a single layer"
    bp, sp, _ = x_prompt.shape
    db, dec_seq, _ = x_sample.shape
    n_phys = cache_k.shape[1]
    t0, t1, s_own, s_near, s_far = _bias_tiles(rel_bias, dec_seq)
    gl = norm_final[None, :]
    l = 0
    w = _layer_weights(l, norm_mix, w_in, b_ig, b_fg, ml_norm, w_out, norm_ffn, w_up, w_down)

    zero_state = (jnp.zeros((bp, N_ML_HEADS, ML_HEAD_DIM, ML_HEAD_DIM), F32),
                  jnp.zeros((bp, N_ML_HEADS, 1, LANES), F32), jnp.zeros((bp, N_ML_HEADS, 1, LANES), F32))
    yp, kp, vp, cp, np_, mp = _trunk_layer(
        x_prompt, w, lambda q, k, v: _moba_prompt(q, k, v, t0, t1, rel_bias), zero_state, gl, 512, 256)

    to_pool = lambda a: jnp.transpose(a, (0, 1, 3, 4, 2)).reshape(depth * n_phys, ATT_WIDTH, PAGE_SIZE)
    pool_kt, pool_vt = to_pool(cache_k), to_pool(cache_v)
    pt = page_table + l * n_phys

    def attend_sample(q, k, v):
        return _moba_sample(q, k, v, pool_kt, pool_vt, pt, s_own, s_near, s_far, 4)

    state0 = (state_C[l], state_n[l][:, :, None, :],
              jnp.broadcast_to(state_m[l][:, :, None, None], (db, N_ML_HEADS, 1, LANES)))
    ys, ks, vs, cs, ns, ms = _trunk_layer(x_sample, w, attend_sample, state0, gl, 256, 128)

    st = lambda a: a[None]
    return (yp, ys, st(kp), st(vp), st(cp), st(np_), st(mp), st(ks), st(vs), st(cs), st(ns), st(ms))
```

```python
import functools
import math

import numpy as np
import jax
import jax.numpy as jnp
from jax import lax
from jax.experimental import pallas as pl
from jax.experimental.pallas import tpu as pltpu

F32 = jnp.float32
BF16 = jnp.bfloat16
HIGHEST = lax.Precision.HIGHEST

D_MODEL = 1024
ATT_HEAD_DIM = 64
N_ATT_HEADS = 8
ATT_WIDTH = 512
N_ML_HEADS = 4
ML_HEAD_DIM = 128
ML_WIDTH = 512
N_SLABS = 7
GATE_COL0 = N_SLABS * 512
MOBA_BLOCK = 256
MOBA_TOPK = 3
PAGE_SIZE = 128
NUM_BUCKETS = 32
MAX_DISTANCE = 128
D_FF = 4 * D_MODEL
RMS_EPS = 1e-6
NEG = -1e30
LANES = 128
VMEM_LIMIT = 56 * 1024 * 1024


def _nt_dot(a, b, precision=None):
    return lax.dot_general(a, b, (((1,), (1,)), ((), ())), preferred_element_type=F32, precision=precision)


def _tn_dot(a, b, precision=None):
    return lax.dot_general(a, b, (((0,), (0,)), ((), ())), preferred_element_type=F32, precision=precision)


def _bucket_upper_bounds():
    n = np.arange(0, MAX_DISTANCE + 1)
    max_exact = NUM_BUCKETS // 2
    nf = np.maximum(n, 1).astype(np.float32)
    large = max_exact + (np.log(nf / np.float32(max_exact)) / np.float32(math.log(MAX_DISTANCE / max_exact))
                         * np.float32(NUM_BUCKETS - max_exact)).astype(np.int32)
    large = np.minimum(large, NUM_BUCKETS - 1)
    bucket = np.where(n < max_exact, n, large)
    return np.array([int(n[bucket <= b].max()) for b in range(NUM_BUCKETS - 1)], dtype=np.int32)


def _bias_of_dist(dist, h, rb_ref, ub_ref):
    def body(i, acc):
        b = NUM_BUCKETS - 2 - i
        return jnp.where(dist <= ub_ref[b], rb_ref[b, h], acc)
    init = jnp.full(dist.shape, rb_ref[NUM_BUCKETS - 1, h], F32)
    return lax.fori_loop(0, NUM_BUCKETS - 1, body, init)


def _bias_tiles_kernel(rb_ref, ub_ref, t0_ref, t1_ref, s_own_ref, s_near_ref, s_far_ref, *, dec_seq):
    h = pl.program_id(0)
    kk = lax.broadcasted_iota(jnp.int32, (MOBA_BLOCK, MOBA_BLOCK), 0)
    qq = lax.broadcasted_iota(jnp.int32, (MOBA_BLOCK, MOBA_BLOCK), 1)
    d0 = qq - kk
    t0_ref[0] = jnp.where(d0 >= 0, _bias_of_dist(jnp.maximum(d0, 0), h, rb_ref, ub_ref), NEG)
    t1_ref[0] = _bias_of_dist(d0 + MOBA_BLOCK, h, rb_ref, ub_ref)
    t = lax.broadcasted_iota(jnp.int32, (dec_seq, LANES), 0)
    cc = lax.broadcasted_iota(jnp.int32, (dec_seq, LANES), 1)
    ds = t - cc
    s_own_ref[0] = jnp.where((ds >= 0) & (cc < dec_seq), _bias_of_dist(jnp.maximum(ds, 0), h, rb_ref, ub_ref), NEG)
    s_near_ref[0] = _bias_of_dist(ds + PAGE_SIZE, h, rb_ref, ub_ref)
    s_far_ref[0] = jnp.full((dec_seq, LANES), rb_ref[NUM_BUCKETS - 1, h], F32)


def _bias_tiles(rel_bias, dec_seq):
    ub = jnp.asarray(_bucket_upper_bounds())
    smem = pl.BlockSpec(memory_space=pltpu.SMEM)
    big = pl.BlockSpec((1, MOBA_BLOCK, MOBA_BLOCK), lambda h: (h, 0, 0))
    small = pl.BlockSpec((1, dec_seq, LANES), lambda h: (h, 0, 0))
    return pl.pallas_call(
        functools.partial(_bias_tiles_kernel, dec_seq=dec_seq),
        grid=(N_ATT_HEADS,),
        in_specs=[smem, smem],
        out_specs=[big, big, small, small, small],
        out_shape=[jax.ShapeDtypeStruct((N_ATT_HEADS, MOBA_BLOCK, MOBA_BLOCK), F32)] * 2
        + [jax.ShapeDtypeStruct((N_ATT_HEADS, dec_seq, LANES), F32)] * 3,
        name="bias_tiles",
    )(rel_bias, ub)


def _rms(x, g):
    return x * lax.rsqrt(jnp.mean(x * x, axis=-1, keepdims=True) + RMS_EPS) * g


def _log_sigmoid(z):
    return jnp.minimum(z, 0.0) - jnp.log1p(jnp.exp(-jnp.abs(z)))


def _proj_kernel(x_ref, g_ref, w_ref, wg_ref, bg_ref, aq_ref, ak_ref, av_ref, mq_ref, mk_ref, mv_ref, mo_ref,
                 gate_ref):
    xn = _rms(x_ref[...], g_ref[...])
    xb = xn.astype(BF16)
    outs = (aq_ref, ak_ref, av_ref, mq_ref, mk_ref, mv_ref, mo_ref)
    scales = (ATT_HEAD_DIM ** -0.5, None, None, None, ML_HEAD_DIM ** -0.5, None, None)
    for i, (o_ref, s) in enumerate(zip(outs, scales)):
        r = jnp.dot(xb, w_ref[:, i * 512:(i + 1) * 512], preferred_element_type=F32)
        o_ref[...] = r if s is None else r * s
    z = jnp.dot(xn, wg_ref[...], preferred_element_type=F32, precision=HIGHEST) + bg_ref[...]
    lane = lax.broadcasted_iota(jnp.int32, z.shape, 1)
    gate_ref[...] = jnp.where((lane >= N_ML_HEADS) & (lane < 2 * N_ML_HEADS), _log_sigmoid(z), z)


def _project(x2, norm_g, w_main, w_gate, b_gate, tm):
    n = x2.shape[0]
    const = lambda shape: pl.BlockSpec(shape, lambda i: (0, 0), pipeline_mode=pl.Buffered(1))
    slab = pl.BlockSpec((tm, 512), lambda i: (i, 0))
    return pl.pallas_call(
        _proj_kernel,
        grid=(n // tm,),
        in_specs=[pl.BlockSpec((tm, D_MODEL), lambda i: (i, 0)), const((1, D_MODEL)),
                  const((D_MODEL, N_SLABS * 512)), const((D_MODEL, LANES)), const((1, LANES))],
        out_specs=[slab] * N_SLABS + [pl.BlockSpec((tm, LANES), lambda i: (i, 0))],
        out_shape=[jax.ShapeDtypeStruct((n, 512), F32)] * N_SLABS + [jax.ShapeDtypeStruct((n, LANES), F32)],
        compiler_params=pltpu.CompilerParams(dimension_semantics=("parallel",), vmem_limit_bytes=VMEM_LIMIT),
        name="rms_in_proj",
    )(x2, norm_g, w_main, w_gate, b_gate)


def _topk_penalty(scores, n_valid):
    lane = lax.broadcasted_iota(jnp.int32, scores.shape, 1)
    rank = jnp.zeros(scores.shape, jnp.int32)
    for n in range(n_valid):
        col = scores[:, n:n + 1]
        rank = rank + jnp.where((col > scores) | ((col == scores) & (lane > n)), 1, 0)
    return jnp.where((rank < MOBA_TOPK) & (lane < n_valid), 0.0, NEG)


def _moba_prompt_kernel(rb_ref, q_ref, k_ref, v_ref, t0_ref, t1_ref, o_ref, km_ref, kb_ref, vt_ref, sc_ref,
                        pen_ref, s_ref, so_ref, sn_ref, *, n_blocks, group):
    hp = pl.program_id(1)
    own = pl.program_id(2)
    blk = MOBA_BLOCK
    nbp = km_ref.shape[0]

    @pl.when(own == 0)
    def _():
        km_ref[...] = jnp.zeros(km_ref.shape, F32)
        for n in range(n_blocks):
            kn = k_ref[0, n * blk:(n + 1) * blk, :]
            km_ref[n:n + 1, :] = jnp.mean(kn, axis=0, keepdims=True)
            kb_ref[n] = kn.astype(BF16)
            vt_ref[n] = v_ref[0, n * blk:(n + 1) * blk, :].T.astype(BF16)

    q2 = q_ref[0]
    lane = lax.broadcasted_iota(jnp.int32, q2.shape, 1)
    qcat = jnp.concatenate([jnp.where((lane // ATT_HEAD_DIM) == hh, q2, 0.0) for hh in range(2)], axis=0)
    qcat_b = qcat.astype(BF16)
    wide = 2 * blk
    sc_ref[...] = _nt_dot(km_ref[...], qcat, precision=HIGHEST)

    sub = lax.broadcasted_iota(jnp.int32, (nbp, wide), 0)

    def rank_body(n, rank):
        sc = sc_ref[...]
        row = sc_ref[pl.ds(n, 1), :]
        return rank + jnp.where((row > sc) | ((row == sc) & (sub > n)), 1, 0)

    rank = lax.fori_loop(0, own, rank_body, jnp.zeros((nbp, wide), jnp.int32))
    pen_ref[...] = jnp.where((rank < MOBA_TOPK) & (sub < own), 0.0, NEG)

    col_head = lax.broadcasted_iota(jnp.int32, (1, wide), 1) // blk
    far_bias = jnp.where(col_head == 0, rb_ref[NUM_BUCKETS - 1, 2 * hp], rb_ref[NUM_BUCKETS - 1, 2 * hp + 1])

    def fold8(x, op):
        return op(x.reshape(blk // 8, 8, x.shape[1]), axis=0)

    def far_row(j):
        return jnp.where(j < own - 1, pen_ref[pl.ds(j, 1), :] + far_bias, NEG)

    grp = group
    n_groups = (jnp.maximum(own - 1, 0) + grp - 1) // grp
    jn = jnp.maximum(own - 1, 0)
    near_row = jnp.where(own >= 1, pen_ref[pl.ds(jn, 1), :], NEG)

    s_own = _nt_dot(kb_ref[own], qcat_b) + jnp.concatenate([t0_ref[0], t0_ref[1]], axis=1)
    s_near = _nt_dot(kb_ref[jn], qcat_b) + jnp.concatenate([t1_ref[0], t1_ref[1]], axis=1)
    so_ref[...] = s_own
    sn_ref[...] = s_near
    m8 = jnp.maximum(fold8(s_own, jnp.max), fold8(s_near, jnp.max) + near_row)

    def max_body(g, m8c):
        j0 = pl.multiple_of(g * grp, grp)
        sg = _nt_dot(kb_ref[pl.ds(j0, grp)].reshape(grp * blk, LANES), qcat_b)
        s_ref[pl.ds(j0, grp)] = sg.reshape(grp, blk, wide)
        for i in range(grp):
            m8c = jnp.maximum(m8c, fold8(sg[i * blk:(i + 1) * blk], jnp.max) + far_row(j0 + i))
        return m8c

    m8 = lax.fori_loop(0, n_groups, max_body, m8)
    m = jnp.max(m8, axis=0, keepdims=True)

    def accumulate(j, p, l8, accs):
        pb = p.astype(BF16)
        new = tuple(accs[hh] + jnp.dot(vt_ref[j, hh * ATT_HEAD_DIM:(hh + 1) * ATT_HEAD_DIM, :],
                                       pb[:, hh * blk:(hh + 1) * blk], preferred_element_type=F32)
                    for hh in range(2))
        return l8 + fold8(p, jnp.sum), new

    zero_acc = jnp.zeros((ATT_HEAD_DIM, blk), F32)
    l8, accs = accumulate(own, jnp.exp(so_ref[...] - m), jnp.zeros((8, wide), F32), (zero_acc, zero_acc))
    l8, accs = accumulate(jn, jnp.exp(sn_ref[...] + (near_row - m)), l8, accs)

    def exp_body(g, st):
        l8c, accc = st
        j0 = pl.multiple_of(g * grp, grp)
        for i in range(grp):
            l8c, accc = accumulate(j0 + i, jnp.exp(s_ref[j0 + i] + (far_row(j0 + i) - m)), l8c, accc)
        return l8c, accc

    l8, accs = lax.fori_loop(0, n_groups, exp_body, (l8, accs))
    l = jnp.sum(l8, axis=0, keepdims=True)
    out_t = jnp.concatenate([accs[hh] / l[:, hh * blk:(hh + 1) * blk] for hh in range(2)], axis=0)
    o_ref[0] = out_t.T


def _moba_prompt(aq, ak, av, t0, t1, rel_bias):
    b, s, _ = aq.shape
    nb = s // MOBA_BLOCK
    nbp = -(-nb // 8) * 8
    n_pairs = N_ATT_HEADS // 2
    wide = 2 * MOBA_BLOCK
    group = 4
    assert nb % group == 0
    qspec = pl.BlockSpec((1, MOBA_BLOCK, LANES), lambda bi, hp, qi: (bi, qi, hp))
    kvspec = pl.BlockSpec((1, s, LANES), lambda bi, hp, qi: (bi, 0, hp))
    tspec = pl.BlockSpec((2, MOBA_BLOCK, MOBA_BLOCK), lambda bi, hp, qi: (hp, 0, 0))
    return pl.pallas_call(
        functools.partial(_moba_prompt_kernel, n_blocks=nb, group=group),
        grid=(b, n_pairs, nb),
        in_specs=[pl.BlockSpec(memory_space=pltpu.SMEM), qspec, kvspec, kvspec, tspec, tspec],
        out_specs=qspec,
        out_shape=jax.ShapeDtypeStruct((b, s, ATT_WIDTH), F32),
        scratch_shapes=[pltpu.VMEM((nbp, LANES), F32), pltpu.VMEM((nb, MOBA_BLOCK, LANES), BF16),
                        pltpu.VMEM((nb, LANES, MOBA_BLOCK), BF16),
                        pltpu.VMEM((nbp, wide), F32), pltpu.VMEM((nbp, wide), F32),
                        pltpu.VMEM((nb, MOBA_BLOCK, wide), F32), pltpu.VMEM((MOBA_BLOCK, wide), F32),
                        pltpu.VMEM((MOBA_BLOCK, wide), F32)],
        compiler_params=pltpu.CompilerParams(dimension_semantics=("parallel", "parallel", "arbitrary"),
                                             vmem_limit_bytes=VMEM_LIMIT),
        name="moba_prompt",
    )(rel_bias, aq, ak, av, t0, t1)


def _moba_sample_kernel(pt_ref, *refs, blocks_per_step, n_blocks, dec_seq):
    ppb = MOBA_BLOCK // PAGE_SIZE
    pps = blocks_per_step * ppb
    k_pages, v_pages = refs[:pps], refs[pps:2 * pps]
    (q_ref, kn_ref, vn_ref, s_own_ref, s_near_ref, s_far_ref, o_ref,
     qbd_ref, kmt_ref, mm_ref, ll_ref, oo_ref, own_m_ref, own_l_ref, own_o_ref) = refs[2 * pps:]
    c = pl.program_id(1)
    rows = N_ATT_HEADS * dec_seq
    row_head = lax.broadcasted_iota(jnp.int32, (rows, ATT_WIDTH), 0) // dec_seq
    lane_head = lax.broadcasted_iota(jnp.int32, (rows, ATT_WIDTH), 1) // ATT_HEAD_DIM
    lane_r = lax.broadcasted_iota(jnp.int32, (rows, LANES), 1)
    lane_k = lax.broadcasted_iota(jnp.int32, (ATT_WIDTH, LANES), 1)

    def local_softmax(s):
        m = jnp.max(s, axis=-1, keepdims=True)
        p = jnp.exp(s - m)
        return m, jnp.sum(p, axis=-1, keepdims=True), p.astype(BF16)

    @pl.when(c == 0)
    def _():
        qrep = jnp.concatenate([q_ref[0]] * N_ATT_HEADS, axis=0)
        qbd = jnp.where(row_head == lane_head, qrep, 0.0)
        qbd_ref[...] = qbd
        kmt_ref[...] = jnp.zeros(kmt_ref.shape, F32)
        mm_ref[...] = jnp.zeros(mm_ref.shape, F32)
        ll_ref[...] = jnp.zeros(ll_ref.shape, F32)
        pad = jnp.zeros((LANES - dec_seq, ATT_WIDTH), F32)
        kn = jnp.concatenate([kn_ref[0], pad], axis=0).astype(BF16)
        vn = jnp.concatenate([vn_ref[0], pad], axis=0).astype(BF16)
        m, l, p = local_softmax(_nt_dot(qbd.astype(BF16), kn) + s_own_ref[...].reshape(rows, LANES))
        own_m_ref[...] = jnp.broadcast_to(m, (rows, LANES))
        own_l_ref[...] = jnp.broadcast_to(l, (rows, LANES))
        own_o_ref[...] = jnp.dot(p, vn, preferred_element_type=F32)

    qb = qbd_ref[...].astype(BF16)
    s_near = s_near_ref[...].reshape(rows, LANES)
    s_far = s_far_ref[...].reshape(rows, LANES)
    n0 = c * blocks_per_step
    kt = jnp.concatenate([k_pages[i][0] for i in range(pps)], axis=1)
    s_all = jnp.dot(qb, kt.astype(BF16), preferred_element_type=F32)
    mm, ll, kmt = mm_ref[...], ll_ref[...], kmt_ref[...]
    for g in range(blocks_per_step):
        n = n0 + g
        bias = jnp.concatenate([s_far] * (ppb - 1) + [jnp.where(n == n_blocks - 1, s_near, s_far)], axis=1)
        m, l, p = local_softmax(s_all[:, g * MOBA_BLOCK:(g + 1) * MOBA_BLOCK] + bias)
        vt = jnp.concatenate([v_pages[ppb * g + i][0] for i in range(ppb)], axis=1).astype(BF16)
        oo_ref[n] = _nt_dot(p, vt)
        mm = jnp.where(lane_r == n, m, mm)
        ll = jnp.where(lane_r == n, l, ll)
        ksum = jnp.sum(kt[:, g * MOBA_BLOCK:(g + 1) * MOBA_BLOCK], axis=-1, keepdims=True)
        kmt = jnp.where(lane_k == n, ksum, kmt)
    mm_ref[...] = mm
    ll_ref[...] = ll
    kmt_ref[...] = kmt

    @pl.when(c == pl.num_programs(1) - 1)
    def _():
        kmean_t = kmt_ref[...] * (1.0 / MOBA_BLOCK)
        pen = _topk_penalty(jnp.dot(qbd_ref[...], kmean_t, preferred_element_type=F32, precision=HIGHEST), n_blocks)
        mp = mm_ref[...] + pen
        m_own = own_m_ref[...]
        m_tot = jnp.maximum(jnp.max(mp, axis=-1, keepdims=True), m_own)
        w = jnp.exp(mp - m_tot)
        w_own = jnp.exp(m_own - m_tot)
        den = jnp.sum(w * ll_ref[...], axis=-1, keepdims=True) + w_own * own_l_ref[...]

        num = w_own[:, 0:1] * own_o_ref[...]
        for nn in range(n_blocks):
            num = num + w[:, nn:nn + 1] * oo_ref[nn]
        full = jnp.where(row_head == lane_head, num / den[:, 0:1], 0.0)
        out = full[0:dec_seq]
        for h in range(1, N_ATT_HEADS):
            out = out + full[h * dec_seq:(h + 1) * dec_seq]
        o_ref[0] = out


def _moba_sample(aq, ak, av, pool_kt, pool_vt, page_table, s_own, s_near, s_far, blocks_per_step):
    db, dec_seq, _ = aq.shape
    n_pages = page_table.shape[1]
    ppb = MOBA_BLOCK // PAGE_SIZE
    nb = n_pages // ppb
    pps = blocks_per_step * ppb
    assert nb % blocks_per_step == 0 and nb <= LANES
    rows = N_ATT_HEADS * dec_seq

    def page_spec(i):
        return pl.BlockSpec((1, ATT_WIDTH, PAGE_SIZE), lambda b, c, pt: (pt[b, c * pps + i], 0, 0))

    per_b = pl.BlockSpec((1, dec_seq, ATT_WIDTH), lambda b, c, pt: (b, 0, 0))
    tile = pl.BlockSpec((N_ATT_HEADS, dec_seq, LANES), lambda b, c, pt: (0, 0, 0))
    grid_spec = pltpu.PrefetchScalarGridSpec(
        num_scalar_prefetch=1, grid=(db, nb // blocks_per_step),
        in_specs=[page_spec(i) for i in range(pps)] * 2 + [per_b, per_b, per_b, tile, tile, tile],
        out_specs=per_b,
        scratch_shapes=[pltpu.VMEM((rows, ATT_WIDTH), F32), pltpu.VMEM((ATT_WIDTH, LANES), F32),
                        pltpu.VMEM((rows, LANES), F32), pltpu.VMEM((rows, LANES), F32),
                        pltpu.VMEM((nb, rows, ATT_WIDTH), F32), pltpu.VMEM((rows, LANES), F32),
                        pltpu.VMEM((rows, LANES), F32), pltpu.VMEM((rows, ATT_WIDTH), F32)])
    return pl.pallas_call(
        functools.partial(_moba_sample_kernel, blocks_per_step=blocks_per_step, n_blocks=nb, dec_seq=dec_seq),
        grid_spec=grid_spec,
        out_shape=jax.ShapeDtypeStruct((db, dec_seq, ATT_WIDTH), F32),
        compiler_params=pltpu.CompilerParams(dimension_semantics=("parallel", "arbitrary"),
                                             vmem_limit_bytes=VMEM_LIMIT),
        name="moba_sample",
    )(page_table, *([pool_kt] * pps), *([pool_vt] * pps), aq, ak, av, s_own, s_near, s_far)


def _mlstm_kernel(q_ref, k_ref, v_ref, g_ref, mo_ref, nw_ref, c0_ref, n0_ref, m0_ref,
                  h_ref, c_out, n_out, m_out, c_s, n_s, m_s, *, rows, chunk):
    ci = pl.program_id(1)
    L = chunk

    @pl.when(ci == 0)
    def _():
        c_s[...] = c0_ref[0]
        n_s[...] = n0_ref[0]
        m_s[...] = m0_ref[0]

    def padded(ref, fill=None):
        x = ref[0]
        if rows == L:
            return x
        tail = jnp.zeros((L - rows, x.shape[1]), F32) if fill is None else fill
        return jnp.concatenate([x, tail], axis=0)

    lane = lax.broadcasted_iota(jnp.int32, (L, LANES), 1)
    pad_lane = lax.broadcasted_iota(jnp.int32, (max(L - rows, 1), LANES), 1)
    g = padded(g_ref, jnp.where(pad_lane < N_ML_HEADS, NEG, 0.0))
    q_all, k_all, v_all, mo_all = padded(q_ref), padded(k_ref), padded(v_ref), padded(mo_ref)

    is_lf = (lane >= N_ML_HEADS) & (lane < 2 * N_ML_HEADS)
    tri_r = lax.broadcasted_iota(jnp.int32, (L, L), 0)
    tri_c = lax.broadcasted_iota(jnp.int32, (L, L), 1)
    causal = tri_c <= tri_r
    bcum = jnp.dot(causal.astype(F32), jnp.where(is_lf, g, 0.0), preferred_element_type=F32, precision=HIGHEST)
    g_t = g.T
    b_t = bcum.T

    for h in range(N_ML_HEADS):
        sl = slice(h * ML_HEAD_DIM, (h + 1) * ML_HEAD_DIM)
        q, k, v = q_all[:, sl], k_all[:, sl], v_all[:, sl]
        ig_col = g[:, h:h + 1]
        b_col = bcum[:, N_ML_HEADS + h:N_ML_HEADS + h + 1]
        ig_row = g_t[h:h + 1, :]
        b_row = b_t[N_ML_HEADS + h:N_ML_HEADS + h + 1, :]
        m_prev = m_s[h][:, 0:1]
        C = c_s[h]
        n_row = n_s[h]

        a_col = b_col + m_prev
        log_d = jnp.where(causal, b_col - b_row + ig_row, NEG)
        m_t = jnp.maximum(a_col, jnp.max(log_d, axis=-1, keepdims=True))
        d = jnp.exp(log_d - m_t)
        w_state = jnp.exp(a_col - m_t)
        qb, kb, vb = q.astype(BF16), k.astype(BF16), v.astype(BF16)
        s = _nt_dot(qb, kb) * d
        num = w_state * jnp.dot(qb, C.astype(BF16), preferred_element_type=F32) \
            + jnp.dot(s.astype(BF16), vb, preferred_element_type=F32)
        den = w_state * jnp.sum(q * n_row, axis=-1, keepdims=True) + jnp.sum(s, axis=-1, keepdims=True)
        hid = num / jnp.maximum(jnp.abs(den), jnp.exp(-m_t))

        hn = _rms(hid, nw_ref[:, sl])
        h_ref[0, :, sl] = (hn * jax.nn.sigmoid(mo_all[:, sl]))[:rows]

        b_last = b_col[L - 1:L, :]
        m_new = m_t[L - 1:L, :]
        g_state = jnp.exp(b_last + m_prev - m_new)
        g_tok = jnp.exp(b_last - b_col + ig_col - m_new)
        kg = k * g_tok
        c_s[h] = g_state * C + _tn_dot(kg.astype(BF16), vb)
        n_s[h] = g_state * n_row + jnp.sum(kg, axis=0, keepdims=True)
        m_s[h] = jnp.broadcast_to(m_new, (1, LANES))

    @pl.when(ci == pl.num_programs(1) - 1)
    def _():
        c_out[0] = c_s[...]
        n_out[0] = n_s[...]
        m_out[0] = m_s[...]


def _mlstm(mq, mk, mv, gates, mo, ml_norm, c0, n0, m0, chunk):
    b, s, _ = mq.shape
    rows = min(s, chunk)
    n_chunks = s // rows
    tok = lambda w: pl.BlockSpec((1, rows, w), lambda bi, ci: (bi, ci, 0))
    st4 = lambda shape: pl.BlockSpec(shape, lambda bi, ci: (bi, 0, 0, 0))
    c_shape, v_shape = (1, N_ML_HEADS, ML_HEAD_DIM, ML_HEAD_DIM), (1, N_ML_HEADS, 1, LANES)
    return pl.pallas_call(
        functools.partial(_mlstm_kernel, rows=rows, chunk=chunk),
        grid=(b, n_chunks),
        in_specs=[tok(ML_WIDTH), tok(ML_WIDTH), tok(ML_WIDTH), tok(LANES), tok(ML_WIDTH),
                  pl.BlockSpec((1, ML_WIDTH), lambda bi, ci: (0, 0)),
                  st4(c_shape), st4(v_shape), st4(v_shape)],
        out_specs=[tok(ML_WIDTH), st4(c_shape), st4(v_shape), st4(v_shape)],
        out_shape=[jax.ShapeDtypeStruct((b, s, ML_WIDTH), F32),
                   jax.ShapeDtypeStruct((b,) + c_shape[1:], F32),
                   jax.ShapeDtypeStruct((b,) + v_shape[1:], F32),
                   jax.ShapeDtypeStruct((b,) + v_shape[1:], F32)],
        scratch_shapes=[pltpu.VMEM(c_shape[1:], F32), pltpu.VMEM(v_shape[1:], F32), pltpu.VMEM(v_shape[1:], F32)],
        compiler_params=pltpu.CompilerParams(dimension_semantics=("parallel", "arbitrary"),
                                             vmem_limit_bytes=VMEM_LIMIT),
        name="mlstm",
    )(mq, mk, mv, gates, mo, ml_norm, c0, n0, m0)


def _out_ffn_kernel(x_ref, att_ref, ml_ref, wo_ref, gf_ref, wu_ref, wd_ref, gl_ref, y_ref, *, ff_chunk):
    x1 = x_ref[...] \
        + jnp.dot(att_ref[...].astype(BF16), wo_ref[0:ATT_WIDTH, :], preferred_element_type=F32) \
        + jnp.dot(ml_ref[...].astype(BF16), wo_ref[ATT_WIDTH:, :], preferred_element_type=F32)
    xb = _rms(x1, gf_ref[...]).astype(BF16)
    ffn = None
    for c in range(D_FF // ff_chunk):
        sl = slice(c * ff_chunk, (c + 1) * ff_chunk)
        u = jnp.maximum(jnp.dot(xb, wu_ref[:, sl], preferred_element_type=F32), 0.0)
        d = jnp.dot((u * u).astype(BF16), wd_ref[sl, :], preferred_element_type=F32)
        ffn = d if ffn is None else ffn + d
    y_ref[...] = _rms(x1 + ffn, gl_ref[...])


def _out_ffn(x2, att, ml, w_out, norm_ffn, w_up, w_down, norm_final, tm, ff_chunk):
    n = x2.shape[0]
    const = lambda shape: pl.BlockSpec(shape, lambda i: (0, 0), pipeline_mode=pl.Buffered(1))
    row = lambda w: pl.BlockSpec((tm, w), lambda i: (i, 0))
    return pl.pallas_call(
        functools.partial(_out_ffn_kernel, ff_chunk=ff_chunk),
        grid=(n // tm,),
        in_specs=[row(D_MODEL), row(ATT_WIDTH), row(ML_WIDTH), const((D_MODEL, D_MODEL)), const((1, D_MODEL)),
                  const((D_MODEL, D_FF)), const((D_FF, D_MODEL)), const((1, D_MODEL))],
        out_specs=row(D_MODEL),
        out_shape=jax.ShapeDtypeStruct((n, D_MODEL), F32),
        compiler_params=pltpu.CompilerParams(dimension_semantics=("parallel",), vmem_limit_bytes=VMEM_LIMIT),
        name="out_proj_ffn",
    )(x2, att, ml, w_out, norm_ffn, w_up, w_down, norm_final)


def _layer_weights(l, norm_mix, w_in, b_ig, b_fg, ml_norm, w_out, norm_ffn, w_up, w_down):
    w_main = w_in[l, :, :GATE_COL0].astype(BF16)
    w_gate = jnp.pad(w_in[l, :, GATE_COL0:], ((0, 0), (0, LANES - 2 * N_ML_HEADS)))
    b_gate = jnp.pad(jnp.concatenate([b_ig[l], b_fg[l]]), (0, LANES - 2 * N_ML_HEADS))[None, :]
    return dict(norm_mix=norm_mix[l][None, :], w_main=w_main, w_gate=w_gate, b_gate=b_gate,
                ml_norm=ml_norm[l][None, :], w_out=w_out[l].astype(BF16), norm_ffn=norm_ffn[l][None, :],
                w_up=w_up[l].astype(BF16), w_down=w_down[l].astype(BF16))


def _trunk_layer(x, w, attend, state0, norm_final, tm, ml_chunk):
    b, s, _ = x.shape
    x2 = x.reshape(b * s, D_MODEL)
    aq, ak, av, mq, mk, mv, mo, gates = _project(x2, w["norm_mix"], w["w_main"], w["w_gate"], w["b_gate"], tm)
    r3 = lambda a: a.reshape(b, s, a.shape[-1])
    att = attend(r3(aq), r3(ak), r3(av))
    c0, n0, m0 = state0
    ml, c_new, n_new, m_new = _mlstm(r3(mq), r3(mk), r3(mv), r3(gates), r3(mo), w["ml_norm"], c0, n0, m0, ml_chunk)
    y = _out_ffn(x2, att.reshape(b * s, ATT_WIDTH), ml.reshape(b * s, ML_WIDTH), w["w_out"], w["norm_ffn"],
                 w["w_up"], w["w_down"], norm_final, tm, 1024)
    return (y.reshape(b, s, D_MODEL), ak.reshape(b, s, N_ATT_HEADS, ATT_HEAD_DIM),
            av.reshape(b, s, N_ATT_HEADS, ATT_HEAD_DIM), c_new, n_new[:, :, 0, :], m_new[:, :, 0, 0])


def kernel(x_prompt, x_sample, cache_k, cache_v, state_C, state_n, state_m, page_table, rel_bias, norm_mix, w_in,
           b_ig, b_fg, ml_norm, w_out, norm_ffn, w_up, w_down, norm_final):
    depth = w_in.shape[0]
    assert depth == 1, "the fused final RMSNorm assumes a single layer"
    bp, sp, _ = x_prompt.shape
    db, dec_seq, _ = x_sample.shape
    n_phys = cache_k.shape[1]
    t0, t1, s_own, s_near, s_far = _bias_tiles(rel_bias, dec_seq)
    gl = norm_final[None, :]
    l = 0
    w = _layer_weights(l, norm_mix, w_in, b_ig, b_fg, ml_norm, w_out, norm_ffn, w_up, w_down)

    zero_state = (jnp.zeros((bp, N_ML_HEADS, ML_HEAD_DIM, ML_HEAD_DIM), F32),
                  jnp.zeros((bp, N_ML_HEADS, 1, LANES), F32), jnp.zeros((bp, N_ML_HEADS, 1, LANES), F32))
    yp, kp, vp, cp, np_, mp = _trunk_layer(
        x_prompt, w, lambda q, k, v: _moba_prompt(q, k, v, t0, t1, rel_bias), zero_state, gl, 512, 256)

    to_pool = lambda a: jnp.transpose(a, (0, 1, 3, 4, 2)).reshape(depth * n_phys, ATT_WIDTH, PAGE_SIZE)
    pool_kt, pool_vt = to_pool(cache_k), to_pool(cache_v)
    pt = page_table + l * n_phys

    def attend_sample(q, k, v):
        return _moba_sample(q, k, v, pool_kt, pool_vt, pt, s_own, s_near, s_far, 4)

    state0 = (state_C[l], state_n[l][:, :, None, :],
              jnp.broadcast_to(state_m[l][:, :, None, None], (db, N_ML_HEADS, 1, LANES)))
    ys, ks, vs, cs, ns, ms = _trunk_layer(x_sample, w, attend_sample, state0, gl, 256, 128)

    st = lambda a: a[None]
    return (yp, ys, st(kp), st(vp), st(cp), st(np_), st(mp), st(ks), st(vs), st(cs), st(ns), st(ms))
```

```python
import functools
import math

import numpy as np
import jax
import jax.numpy as jnp
from jax import lax
from jax.experimental import pallas as pl
from jax.experimental.pallas import tpu as pltpu

F32 = jnp.float32
BF16 = jnp.bfloat16
HIGHEST = lax.Precision.HIGHEST

D_MODEL = 1024
ATT_HEAD_DIM = 64
N_ATT_HEADS = 8
ATT_WIDTH = 512
N_ML_HEADS = 4
ML_HEAD_DIM = 128
ML_WIDTH = 512
N_SLABS = 7
GATE_COL0 = N_SLABS * 512
MOBA_BLOCK = 256
MOBA_TOPK = 3
PAGE_SIZE = 128
NUM_BUCKETS = 32
MAX_DISTANCE = 128
D_FF = 4 * D_MODEL
RMS_EPS = 1e-6
NEG = -1e30
LANES = 128
VMEM_LIMIT = 56 * 1024 * 1024


def _nt_dot(a, b, precision=None):
    return lax.dot_general(a, b, (((1,), (1,)), ((), ())), preferred_element_type=F32, precision=precision)


def _tn_dot(a, b, precision=None):
    return lax.dot_general(a, b, (((0,), (0,)), ((), ())), preferred_element_type=F32, precision=precision)


def _bucket_upper_bounds():
    n = np.arange(0, MAX_DISTANCE + 1)
    max_exact = NUM_BUCKETS // 2
    nf = np.maximum(n, 1).astype(np.float32)
    large = max_exact + (np.log(nf / np.float32(max_exact)) / np.float32(math.log(MAX_DISTANCE / max_exact))
                         * np.float32(NUM_BUCKETS - max_exact)).astype(np.int32)
    large = np.minimum(large, NUM_BUCKETS - 1)
    bucket = np.where(n < max_exact, n, large)
    return np.array([int(n[bucket <= b].max()) for b in range(NUM_BUCKETS - 1)], dtype=np.int32)


def _bias_of_dist(dist, h, rb_ref, ub_ref):
    def body(i, acc):
        b = NUM_BUCKETS - 2 - i
        return jnp.where(dist <= ub_ref[b], rb_ref[b, h], acc)
    init = jnp.full(dist.shape, rb_ref[NUM_BUCKETS - 1, h], F32)
    return lax.fori_loop(0, NUM_BUCKETS - 1, body, init)


def _bias_tiles_kernel(rb_ref, ub_ref, t0_ref, t1_ref, s_own_ref, s_near_ref, s_far_ref, *, dec_seq):
    h = pl.program_id(0)
    kk = lax.broadcasted_iota(jnp.int32, (MOBA_BLOCK, MOBA_BLOCK), 0)
    qq = lax.broadcasted_iota(jnp.int32, (MOBA_BLOCK, MOBA_BLOCK), 1)
    d0 = qq - kk
    t0_ref[0] = jnp.where(d0 >= 0, _bias_of_dist(jnp.maximum(d0, 0), h, rb_ref, ub_ref), NEG)
    t1_ref[0] = _bias_of_dist(d0 + MOBA_BLOCK, h, rb_ref, ub_ref)
    t = lax.broadcasted_iota(jnp.int32, (dec_seq, LANES), 0)
    cc = lax.broadcasted_iota(jnp.int32, (dec_seq, LANES), 1)
    ds = t - cc
    s_own_ref[0] = jnp.where((ds >= 0) & (cc < dec_seq), _bias_of_dist(jnp.maximum(ds, 0), h, rb_ref, ub_ref), NEG)
    s_near_ref[0] = _bias_of_dist(ds + PAGE_SIZE, h, rb_ref, ub_ref)
    s_far_ref[0] = jnp.full((dec_seq, LANES), rb_ref[NUM_BUCKETS - 1, h], F32)


def _bias_tiles(rel_bias, dec_seq):
    ub = jnp.asarray(_bucket_upper_bounds())
    smem = pl.BlockSpec(memory_space=pltpu.SMEM)
    big = pl.BlockSpec((1, MOBA_BLOCK, MOBA_BLOCK), lambda h: (h, 0, 0))
    small = pl.BlockSpec((1, dec_seq, LANES), lambda h: (h, 0, 0))
    return pl.pallas_call(
        functools.partial(_bias_tiles_kernel, dec_seq=dec_seq),
        grid=(N_ATT_HEADS,),
        in_specs=[smem, smem],
        out_specs=[big, big, small, small, small],
        out_shape=[jax.ShapeDtypeStruct((N_ATT_HEADS, MOBA_BLOCK, MOBA_BLOCK), F32)] * 2
        + [jax.ShapeDtypeStruct((N_ATT_HEADS, dec_seq, LANES), F32)] * 3,
        name="bias_tiles",
    )(rel_bias, ub)


def _rms(x, g):
    return x * lax.rsqrt(jnp.mean(x * x, axis=-1, keepdims=True) + RMS_EPS) * g


def _log_sigmoid(z):
    return jnp.minimum(z, 0.0) - jnp.log1p(jnp.exp(-jnp.abs(z)))


def _proj_kernel(x_ref, g_ref, w_ref, wg_ref, bg_ref, aq_ref, ak_ref, av_ref, mq_ref, mk_ref, mv_ref, mo_ref,
                 gate_ref, *kv_t_refs):
    xn = _rms(x_ref[...], g_ref[...])
    xb = xn.astype(BF16)
    outs = (aq_ref, ak_ref, av_ref, mq_ref, mk_ref, mv_ref, mo_ref)
    scales = (ATT_HEAD_DIM ** -0.5, None, None, None, ML_HEAD_DIM ** -0.5, None, None)
    for i, (o_ref, s) in enumerate(zip(outs, scales)):
        r = jnp.dot(xb, w_ref[:, i * 512:(i + 1) * 512], preferred_element_type=F32)
        o_ref[...] = r if s is None else r * s
        if kv_t_refs and i in (1, 2):
            kv_t_refs[i - 1][0] = r.T
    z = jnp.dot(xn, wg_ref[...], preferred_element_type=F32, precision=HIGHEST) + bg_ref[...]
    lane = lax.broadcasted_iota(jnp.int32, z.shape, 1)
    gate_ref[...] = jnp.where((lane >= N_ML_HEADS) & (lane < 2 * N_ML_HEADS), _log_sigmoid(z), z)


def _project(x2, norm_g, w_main, w_gate, b_gate, tm, seq):
    n = x2.shape[0]
    const = lambda shape: pl.BlockSpec(shape, lambda i: (0, 0), pipeline_mode=pl.Buffered(1))
    slab = pl.BlockSpec((tm, 512), lambda i: (i, 0))
    out_specs = [slab] * N_SLABS + [pl.BlockSpec((tm, LANES), lambda i: (i, 0))]
    out_shape = [jax.ShapeDtypeStruct((n, 512), F32)] * N_SLABS + [jax.ShapeDtypeStruct((n, LANES), F32)]
    if seq % tm == 0:
        tiles = seq // tm
        out_specs += [pl.BlockSpec((1, 512, tm), lambda i: (i // tiles, 0, i % tiles))] * 2
        out_shape += [jax.ShapeDtypeStruct((n // seq, 512, seq), F32)] * 2
    return pl.pallas_call(
        _proj_kernel,
        grid=(n // tm,),
        in_specs=[pl.BlockSpec((tm, D_MODEL), lambda i: (i, 0)), const((1, D_MODEL)),
                  const((D_MODEL, N_SLABS * 512)), const((D_MODEL, LANES)), const((1, LANES))],
        out_specs=out_specs,
        out_shape=out_shape,
        compiler_params=pltpu.CompilerParams(dimension_semantics=("parallel",), vmem_limit_bytes=VMEM_LIMIT),
        name="rms_in_proj",
    )(x2, norm_g, w_main, w_gate, b_gate)


def _topk_penalty(scores, n_valid):
    lane = lax.broadcasted_iota(jnp.int32, scores.shape, 1)
    rank = jnp.zeros(scores.shape, jnp.int32)
    for n in range(n_valid):
        col = scores[:, n:n + 1]
        rank = rank + jnp.where((col > scores) | ((col == scores) & (lane > n)), 1, 0)
    return jnp.where((rank < MOBA_TOPK) & (lane < n_valid), 0.0, NEG)


def _moba_prompt_kernel(rb_ref, q_ref, k_ref, v_ref, t0_ref, t1_ref, o_ref, km_ref, kb_ref, vt_ref, sc_ref,
                        pen_ref, s_ref, sp_ref, *, n_blocks, group):
    hp = pl.program_id(1)
    own = pl.program_id(2)
    blk = MOBA_BLOCK
    nbp = km_ref.shape[0]

    @pl.when(own == 0)
    def _():
        km_ref[...] = jnp.zeros(km_ref.shape, F32)
        for n in range(n_blocks):
            kn = k_ref[0, n * blk:(n + 1) * blk, :]
            km_ref[n:n + 1, :] = jnp.mean(kn, axis=0, keepdims=True)
            kb_ref[n] = kn.astype(BF16)
            vt_ref[n] = v_ref[0, n * blk:(n + 1) * blk, :].T.astype(BF16)

    q2 = q_ref[0]
    lane = lax.broadcasted_iota(jnp.int32, q2.shape, 1)
    qcat = jnp.concatenate([jnp.where((lane // ATT_HEAD_DIM) == hh, q2, 0.0) for hh in range(2)], axis=0)
    qcat_b = qcat.astype(BF16)
    wide = 2 * blk
    sc_ref[...] = _nt_dot(km_ref[...], qcat, precision=HIGHEST)

    sub = lax.broadcasted_iota(jnp.int32, (nbp, wide), 0)

    def rank_body(n, rank):
        sc = sc_ref[...]
        row = sc_ref[pl.ds(n, 1), :]
        return rank + jnp.where((row > sc) | ((row == sc) & (sub > n)), 1, 0)

    rank = lax.fori_loop(0, own, rank_body, jnp.zeros((nbp, wide), jnp.int32))
    pen_ref[...] = jnp.where((rank < MOBA_TOPK) & (sub < own), 0.0, NEG)

    col_head = lax.broadcasted_iota(jnp.int32, (1, wide), 1) // blk
    far_bias = jnp.where(col_head == 0, rb_ref[NUM_BUCKETS - 1, 2 * hp], rb_ref[NUM_BUCKETS - 1, 2 * hp + 1])

    def fold8(x, op):
        return op(x.reshape(blk // 8, 8, x.shape[1]), axis=0)

    def far_row(j):
        return jnp.where(j < own - 1, pen_ref[pl.ds(j, 1), :] + far_bias, NEG)

    grp = group
    n_groups = (jnp.maximum(own - 1, 0) + grp - 1) // grp
    jn = jnp.maximum(own - 1, 0)
    near_row = jnp.where(own >= 1, pen_ref[pl.ds(jn, 1), :], NEG)

    t0c = jnp.concatenate([t0_ref[0], t0_ref[1]], axis=1)
    t1c = jnp.concatenate([t1_ref[0], t1_ref[1]], axis=1)
    no_prev = own == 0
    s_pair = _nt_dot(kb_ref[pl.ds(jn, 2)].reshape(2 * blk, LANES), qcat_b)
    s_first = s_pair[:blk] + jnp.where(no_prev, t0c, t1c + near_row)
    s_second = jnp.where(no_prev, NEG, s_pair[blk:] + t0c)
    sp_ref[0:blk, :] = s_first
    sp_ref[blk:, :] = s_second
    m8 = jnp.maximum(fold8(s_first, jnp.max), fold8(s_second, jnp.max))

    def max_body(g, m8c):
        j0 = pl.multiple_of(g * grp, grp)
        sg = _nt_dot(kb_ref[pl.ds(j0, grp)].reshape(grp * blk, LANES), qcat_b)
        s_ref[pl.ds(j0, grp)] = sg.reshape(grp, blk, wide)
        for i in range(grp):
            m8c = jnp.maximum(m8c, fold8(sg[i * blk:(i + 1) * blk], jnp.max) + far_row(j0 + i))
        return m8c

    m8 = lax.fori_loop(0, n_groups, max_body, m8)
    m = jnp.max(m8, axis=0, keepdims=True)

    def accumulate(j, p, l8, accs):
        pb = p.astype(BF16)
        new = tuple(accs[hh] + jnp.dot(vt_ref[j, hh * ATT_HEAD_DIM:(hh + 1) * ATT_HEAD_DIM, :],
                                       pb[:, hh * blk:(hh + 1) * blk], preferred_element_type=F32)
                    for hh in range(2))
        return l8 + fold8(p, jnp.sum), new

    p_pair = jnp.exp(sp_ref[...] - m)
    l8 = jnp.sum(p_pair.reshape(2 * blk // 8, 8, wide), axis=0)
    pb_pair = p_pair.astype(BF16)
    vt_pair = vt_ref[pl.ds(jn, 2)]
    accs = tuple(
        jnp.dot(jnp.concatenate([vt_pair[i, hh * ATT_HEAD_DIM:(hh + 1) * ATT_HEAD_DIM, :] for i in range(2)], axis=1),
                pb_pair[:, hh * blk:(hh + 1) * blk], preferred_element_type=F32)
        for hh in range(2))

    def exp_body(g, st):
        l8c, accc = st
        j0 = pl.multiple_of(g * grp, grp)
        for i in range(grp):
            l8c, accc = accumulate(j0 + i, jnp.exp(s_ref[j0 + i] + (far_row(j0 + i) - m)), l8c, accc)
        return l8c, accc

    l8, accs = lax.fori_loop(0, n_groups, exp_body, (l8, accs))
    l = jnp.sum(l8, axis=0, keepdims=True)
    out_t = jnp.concatenate([accs[hh] / l[:, hh * blk:(hh + 1) * blk] for hh in range(2)], axis=0)
    o_ref[0] = out_t.T


def _moba_prompt(aq, ak, av, t0, t1, rel_bias):
    b, s, _ = aq.shape
    nb = s // MOBA_BLOCK
    nbp = -(-nb // 8) * 8
    n_pairs = N_ATT_HEADS // 2
    wide = 2 * MOBA_BLOCK
    group = 4
    assert nb % group == 0
    qspec = pl.BlockSpec((1, MOBA_BLOCK, LANES), lambda bi, hp, qi: (bi, qi, hp))
    kvspec = pl.BlockSpec((1, s, LANES), lambda bi, hp, qi: (bi, 0, hp))
    tspec = pl.BlockSpec((2, MOBA_BLOCK, MOBA_BLOCK), lambda bi, hp, qi: (hp, 0, 0))
    return pl.pallas_call(
        functools.partial(_moba_prompt_kernel, n_blocks=nb, group=group),
        grid=(b, n_pairs, nb),
        in_specs=[pl.BlockSpec(memory_space=pltpu.SMEM), qspec, kvspec, kvspec, tspec, tspec],
        out_specs=qspec,
        out_shape=jax.ShapeDtypeStruct((b, s, ATT_WIDTH), F32),
        scratch_shapes=[pltpu.VMEM((nbp, LANES), F32), pltpu.VMEM((nb, MOBA_BLOCK, LANES), BF16),
                        pltpu.VMEM((nb, LANES, MOBA_BLOCK), BF16),
                        pltpu.VMEM((nbp, wide), F32), pltpu.VMEM((nbp, wide), F32),
                        pltpu.VMEM((nb, MOBA_BLOCK, wide), F32), pltpu.VMEM((2 * MOBA_BLOCK, wide), F32)],
        compiler_params=pltpu.CompilerParams(dimension_semantics=("parallel", "parallel", "arbitrary"),
                                             vmem_limit_bytes=VMEM_LIMIT),
        name="moba_prompt",
    )(rel_bias, aq, ak, av, t0, t1)


def _moba_sample_kernel(pt_ref, *refs, blocks_per_step, n_blocks, dec_seq):
    ppb = MOBA_BLOCK // PAGE_SIZE
    pps = blocks_per_step * ppb
    k_pages, v_pages = refs[:pps], refs[pps:2 * pps]
    (q_ref, kn_ref, vn_ref, s_own_ref, s_near_ref, s_far_ref, o_ref,
     qbd_ref, kmt_ref, mm_ref, ll_ref, oo_ref, own_m_ref, own_l_ref, own_o_ref) = refs[2 * pps:]
    c = pl.program_id(1)
    rows = N_ATT_HEADS * dec_seq
    row_head = lax.broadcasted_iota(jnp.int32, (rows, ATT_WIDTH), 0) // dec_seq
    lane_head = lax.broadcasted_iota(jnp.int32, (rows, ATT_WIDTH), 1) // ATT_HEAD_DIM
    lane_r = lax.broadcasted_iota(jnp.int32, (rows, LANES), 1)
    lane_k = lax.broadcasted_iota(jnp.int32, (ATT_WIDTH, LANES), 1)

    def local_softmax(s):
        m = jnp.max(s, axis=-1, keepdims=True)
        p = jnp.exp(s - m)
        return m, jnp.sum(p, axis=-1, keepdims=True), p.astype(BF16)

    @pl.when(c == 0)
    def _():
        qrep = jnp.concatenate([q_ref[0]] * N_ATT_HEADS, axis=0)
        qbd = jnp.where(row_head == lane_head, qrep, 0.0)
        qbd_ref[...] = qbd
        kmt_ref[...] = jnp.zeros(kmt_ref.shape, F32)
        mm_ref[...] = jnp.zeros(mm_ref.shape, F32)
        ll_ref[...] = jnp.zeros(ll_ref.shape, F32)
        pad = jnp.zeros((LANES - dec_seq, ATT_WIDTH), F32)
        kn = jnp.concatenate([kn_ref[0], pad], axis=0).astype(BF16)
        vn = jnp.concatenate([vn_ref[0], pad], axis=0).astype(BF16)
        m, l, p = local_softmax(_nt_dot(qbd.astype(BF16), kn) + s_own_ref[...].reshape(rows, LANES))
        own_m_ref[...] = jnp.broadcast_to(m, (rows, LANES))
        own_l_ref[...] = jnp.broadcast_to(l, (rows, LANES))
        own_o_ref[...] = jnp.dot(p, vn, preferred_element_type=F32)

    qb = qbd_ref[...].astype(BF16)
    s_near = s_near_ref[...].reshape(rows, LANES)
    s_far = s_far_ref[...].reshape(rows, LANES)
    n0 = c * blocks_per_step
    kt = jnp.concatenate([k_pages[i][0] for i in range(pps)], axis=1)
    s_all = jnp.dot(qb, kt.astype(BF16), preferred_element_type=F32)
    mm, ll, kmt = mm_ref[...], ll_ref[...], kmt_ref[...]
    for g in range(blocks_per_step):
        n = n0 + g
        bias = jnp.concatenate([s_far] * (ppb - 1) + [jnp.where(n == n_blocks - 1, s_near, s_far)], axis=1)
        m, l, p = local_softmax(s_all[:, g * MOBA_BLOCK:(g + 1) * MOBA_BLOCK] + bias)
        vt = jnp.concatenate([v_pages[ppb * g + i][0] for i in range(ppb)], axis=1).astype(BF16)
        oo_ref[n] = _nt_dot(p, vt)
        mm = jnp.where(lane_r == n, m, mm)
        ll = jnp.where(lane_r == n, l, ll)
        ksum = jnp.sum(kt[:, g * MOBA_BLOCK:(g + 1) * MOBA_BLOCK], axis=-1, keepdims=True)
        kmt = jnp.where(lane_k == n, ksum, kmt)
    mm_ref[...] = mm
    ll_ref[...] = ll
    kmt_ref[...] = kmt

    @pl.when(c == pl.num_programs(1) - 1)
    def _():
        kmean_t = kmt_ref[...] * (1.0 / MOBA_BLOCK)
        pen = _topk_penalty(jnp.dot(qbd_ref[...], kmean_t, preferred_element_type=F32, precision=HIGHEST), n_blocks)
        mp = mm_ref[...] + pen
        m_own = own_m_ref[...]
        m_tot = jnp.maximum(jnp.max(mp, axis=-1, keepdims=True), m_own)
        w = jnp.exp(mp - m_tot)
        w_own = jnp.exp(m_own - m_tot)
        den = jnp.sum(w * ll_ref[...], axis=-1, keepdims=True) + w_own * own_l_ref[...]

        num = w_own[:, 0:1] * own_o_ref[...]
        for nn in range(n_blocks):
            num = num + w[:, nn:nn + 1] * oo_ref[nn]
        full = jnp.where(row_head == lane_head, num / den[:, 0:1], 0.0)
        out = full[0:dec_seq]
        for h in range(1, N_ATT_HEADS):
            out = out + full[h * dec_seq:(h + 1) * dec_seq]
        o_ref[0] = out


def _moba_sample(aq, ak, av, pool_kt, pool_vt, page_table, s_own, s_near, s_far, blocks_per_step):
    db, dec_seq, _ = aq.shape
    n_pages = page_table.shape[1]
    ppb = MOBA_BLOCK // PAGE_SIZE
    nb = n_pages // ppb
    pps = blocks_per_step * ppb
    assert nb % blocks_per_step == 0 and nb <= LANES
    rows = N_ATT_HEADS * dec_seq

    def page_spec(i):
        return pl.BlockSpec((1, ATT_WIDTH, PAGE_SIZE), lambda b, c, pt: (pt[b, c * pps + i], 0, 0))

    per_b = pl.BlockSpec((1, dec_seq, ATT_WIDTH), lambda b, c, pt: (b, 0, 0))
    tile = pl.BlockSpec((N_ATT_HEADS, dec_seq, LANES), lambda b, c, pt: (0, 0, 0))
    grid_spec = pltpu.PrefetchScalarGridSpec(
        num_scalar_prefetch=1, grid=(db, nb // blocks_per_step),
        in_specs=[page_spec(i) for i in range(pps)] * 2 + [per_b, per_b, per_b, tile, tile, tile],
        out_specs=per_b,
        scratch_shapes=[pltpu.VMEM((rows, ATT_WIDTH), F32), pltpu.VMEM((ATT_WIDTH, LANES), F32),
                        pltpu.VMEM((rows, LANES), F32), pltpu.VMEM((rows, LANES), F32),
                        pltpu.VMEM((nb, rows, ATT_WIDTH), F32), pltpu.VMEM((rows, LANES), F32),
                        pltpu.VMEM((rows, LANES), F32), pltpu.VMEM((rows, ATT_WIDTH), F32)])
    return pl.pallas_call(
        functools.partial(_moba_sample_kernel, blocks_per_step=blocks_per_step, n_blocks=nb, dec_seq=dec_seq),
        grid_spec=grid_spec,
        out_shape=jax.ShapeDtypeStruct((db, dec_seq, ATT_WIDTH), F32),
        compiler_params=pltpu.CompilerParams(dimension_semantics=("parallel", "arbitrary"),
                                             vmem_limit_bytes=VMEM_LIMIT),
        name="moba_sample",
    )(page_table, *([pool_kt] * pps), *([pool_vt] * pps), aq, ak, av, s_own, s_near, s_far)


def _mlstm_kernel(q_ref, k_ref, v_ref, g_ref, mo_ref, nw_ref, c0_ref, n0_ref, m0_ref,
                  h_ref, c_out, n_out, m_out, c_s, n_s, m_s, *, rows, chunk):
    ci = pl.program_id(1)
    L = chunk

    @pl.when(ci == 0)
    def _():
        c_s[...] = c0_ref[0]
        n_s[...] = n0_ref[0]
        m_s[...] = m0_ref[0]

    def padded(ref, fill=None):
        x = ref[0]
        if rows == L:
            return x
        tail = jnp.zeros((L - rows, x.shape[1]), F32) if fill is None else fill
        return jnp.concatenate([x, tail], axis=0)

    lane = lax.broadcasted_iota(jnp.int32, (L, LANES), 1)
    pad_lane = lax.broadcasted_iota(jnp.int32, (max(L - rows, 1), LANES), 1)
    g = padded(g_ref, jnp.where(pad_lane < N_ML_HEADS, NEG, 0.0))
    q_all, k_all, v_all, mo_all = padded(q_ref), padded(k_ref), padded(v_ref), padded(mo_ref)

    is_lf = (lane >= N_ML_HEADS) & (lane < 2 * N_ML_HEADS)
    tri_r = lax.broadcasted_iota(jnp.int32, (L, L), 0)
    tri_c = lax.broadcasted_iota(jnp.int32, (L, L), 1)
    causal = tri_c <= tri_r
    bcum = jnp.dot(causal.astype(F32), jnp.where(is_lf, g, 0.0), preferred_element_type=F32, precision=HIGHEST)
    g_t = g.T
    b_t = bcum.T

    for h in range(N_ML_HEADS):
        sl = slice(h * ML_HEAD_DIM, (h + 1) * ML_HEAD_DIM)
        q, k, v = q_all[:, sl], k_all[:, sl], v_all[:, sl]
        ig_col = g[:, h:h + 1]
        b_col = bcum[:, N_ML_HEADS + h:N_ML_HEADS + h + 1]
        ig_row = g_t[h:h + 1, :]
        b_row = b_t[N_ML_HEADS + h:N_ML_HEADS + h + 1, :]
        m_prev = m_s[h][:, 0:1]
        C = c_s[h]
        n_row = n_s[h]

        a_col = b_col + m_prev
        log_d = jnp.where(causal, b_col - b_row + ig_row, NEG)
        m_t = jnp.maximum(a_col, jnp.max(log_d, axis=-1, keepdims=True))
        d = jnp.exp(log_d - m_t)
        w_state = jnp.exp(a_col - m_t)
        qb, kb, vb = q.astype(BF16), k.astype(BF16), v.astype(BF16)
        s = _nt_dot(qb, kb) * d
        num = w_state * jnp.dot(qb, C.astype(BF16), preferred_element_type=F32) \
            + jnp.dot(s.astype(BF16), vb, preferred_element_type=F32)
        den = w_state * jnp.sum(q * n_row, axis=-1, keepdims=True) + jnp.sum(s, axis=-1, keepdims=True)
        hid = num / jnp.maximum(jnp.abs(den), jnp.exp(-m_t))

        hn = _rms(hid, nw_ref[:, sl])
        h_ref[0, :, sl] = (hn * jax.nn.sigmoid(mo_all[:, sl]))[:rows]

        b_last = b_col[L - 1:L, :]
        m_new = m_t[L - 1:L, :]
        g_state = jnp.exp(b_last + m_prev - m_new)
        g_tok = jnp.exp(b_last - b_col + ig_col - m_new)
        kg = k * g_tok
        c_s[h] = g_state * C + _tn_dot(kg.astype(BF16), vb)
        n_s[h] = g_state * n_row + jnp.sum(kg, axis=0, keepdims=True)
        m_s[h] = jnp.broadcast_to(m_new, (1, LANES))

    @pl.when(ci == pl.num_programs(1) - 1)
    def _():
        c_out[0] = c_s[...]
        n_out[0] = n_s[...]
        m_out[0] = m_s[...]


def _mlstm(mq, mk, mv, gates, mo, ml_norm, c0, n0, m0, chunk):
    b, s, _ = mq.shape
    rows = min(s, chunk)
    n_chunks = s // rows
    tok = lambda w: pl.BlockSpec((1, rows, w), lambda bi, ci: (bi, ci, 0))
    st4 = lambda shape: pl.BlockSpec(shape, lambda bi, ci: (bi, 0, 0, 0))
    c_shape, v_shape = (1, N_ML_HEADS, ML_HEAD_DIM, ML_HEAD_DIM), (1, N_ML_HEADS, 1, LANES)
    return pl.pallas_call(
        functools.partial(_mlstm_kernel, rows=rows, chunk=chunk),
        grid=(b, n_chunks),
        in_specs=[tok(ML_WIDTH), tok(ML_WIDTH), tok(ML_WIDTH), tok(LANES), tok(ML_WIDTH),
                  pl.BlockSpec((1, ML_WIDTH), lambda bi, ci: (0, 0)),
                  st4(c_shape), st4(v_shape), st4(v_shape)],
        out_specs=[tok(ML_WIDTH), st4(c_shape), st4(v_shape), st4(v_shape)],
        out_shape=[jax.ShapeDtypeStruct((b, s, ML_WIDTH), F32),
                   jax.ShapeDtypeStruct((b,) + c_shape[1:], F32),
                   jax.ShapeDtypeStruct((b,) + v_shape[1:], F32),
                   jax.ShapeDtypeStruct((b,) + v_shape[1:], F32)],
        scratch_shapes=[pltpu.VMEM(c_shape[1:], F32), pltpu.VMEM(v_shape[1:], F32), pltpu.VMEM(v_shape[1:], F32)],
        compiler_params=pltpu.CompilerParams(dimension_semantics=("parallel", "arbitrary"),
                                             vmem_limit_bytes=VMEM_LIMIT),
        name="mlstm",
    )(mq, mk, mv, gates, mo, ml_norm, c0, n0, m0)


def _out_ffn_kernel(x_ref, att_ref, ml_ref, wo_ref, gf_ref, wu_ref, wd_ref, gl_ref, y_ref, *, ff_chunk):
    x1 = x_ref[...] \
        + jnp.dot(att_ref[...].astype(BF16), wo_ref[0:ATT_WIDTH, :], preferred_element_type=F32) \
        + jnp.dot(ml_ref[...].astype(BF16), wo_ref[ATT_WIDTH:, :], preferred_element_type=F32)
    xb = _rms(x1, gf_ref[...]).astype(BF16)
    ffn = None
    for c in range(D_FF // ff_chunk):
        sl = slice(c * ff_chunk, (c + 1) * ff_chunk)
        u = jnp.maximum(jnp.dot(xb, wu_ref[:, sl], preferred_element_type=F32), 0.0)
        d = jnp.dot((u * u).astype(BF16), wd_ref[sl, :], preferred_element_type=F32)
        ffn = d if ffn is None else ffn + d
    y_ref[...] = _rms(x1 + ffn, gl_ref[...])


def _out_ffn(x2, att, ml, w_out, norm_ffn, w_up, w_down, norm_final, tm, ff_chunk):
    n = x2.shape[0]
    const = lambda shape: pl.BlockSpec(shape, lambda i: (0, 0), pipeline_mode=pl.Buffered(1))
    row = lambda w: pl.BlockSpec((tm, w), lambda i: (i, 0))
    return pl.pallas_call(
        functools.partial(_out_ffn_kernel, ff_chunk=ff_chunk),
        grid=(n // tm,),
        in_specs=[row(D_MODEL), row(ATT_WIDTH), row(ML_WIDTH), const((D_MODEL, D_MODEL)), const((1, D_MODEL)),
                  const((D_MODEL, D_FF)), const((D_FF, D_MODEL)), const((1, D_MODEL))],
        out_specs=row(D_MODEL),
        out_shape=jax.ShapeDtypeStruct((n, D_MODEL), F32),
        compiler_params=pltpu.CompilerParams(dimension_semantics=("parallel",), vmem_limit_bytes=VMEM_LIMIT),
        name="out_proj_ffn",
    )(x2, att, ml, w_out, norm_ffn, w_up, w_down, norm_final)


def _layer_weights(l, norm_mix, w_in, b_ig, b_fg, ml_norm, w_out, norm_ffn, w_up, w_down):
    w_main = w_in[l, :, :GATE_COL0].astype(BF16)
    w_gate = jnp.pad(w_in[l, :, GATE_COL0:], ((0, 0), (0, LANES - 2 * N_ML_HEADS)))
    b_gate = jnp.pad(jnp.concatenate([b_ig[l], b_fg[l]]), (0, LANES - 2 * N_ML_HEADS))[None, :]
    return dict(norm_mix=norm_mix[l][None, :], w_main=w_main, w_gate=w_gate, b_gate=b_gate,
                ml_norm=ml_norm[l][None, :], w_out=w_out[l].astype(BF16), norm_ffn=norm_ffn[l][None, :],
                w_up=w_up[l].astype(BF16), w_down=w_down[l].astype(BF16))


def _trunk_layer(x, w, attend, state0, norm_final, tm, ml_chunk):
    b, s, _ = x.shape
    x2 = x.reshape(b * s, D_MODEL)
    aq, ak, av, mq, mk, mv, mo, gates, *kv_t = _project(x2, w["norm_mix"], w["w_main"], w["w_gate"], w["b_gate"], tm, s)
    r3 = lambda a: a.reshape(b, s, a.shape[-1])
    if kv_t:
        k_out, v_out = (a.reshape(b, N_ATT_HEADS, ATT_HEAD_DIM, s).transpose(0, 3, 1, 2) for a in kv_t)
    else:
        k_out, v_out = (a.reshape(b, s, N_ATT_HEADS, ATT_HEAD_DIM) for a in (ak, av))
    att = attend(r3(aq), r3(ak), r3(av))
    c0, n0, m0 = state0
    ml, c_new, n_new, m_new = _mlstm(r3(mq), r3(mk), r3(mv), r3(gates), r3(mo), w["ml_norm"], c0, n0, m0, ml_chunk)
    y = _out_ffn(x2, att.reshape(b * s, ATT_WIDTH), ml.reshape(b * s, ML_WIDTH), w["w_out"], w["norm_ffn"],
                 w["w_up"], w["w_down"], norm_final, tm, 1024)
    return y.reshape(b, s, D_MODEL), k_out, v_out, c_new, n_new[:, :, 0, :], m_new[:, :, 0, 0]


def kernel(x_prompt, x_sample, cache_k, cache_v, state_C, state_n, state_m, page_table, rel_bias, norm_mix, w_in,
           b_ig, b_fg, ml_norm, w_out, norm_ffn, w_up, w_down, norm_final):
    depth = w_in.shape[0]
    assert depth == 1, "the fused final RMSNorm assumes a single layer"
    bp, sp, _ = x_prompt.shape
    db, dec_seq, _ = x_sample.shape
    n_phys = cache_k.shape[1]
    t0, t1, s_own, s_near, s_far = _bias_tiles(rel_bias, dec_seq)
    gl = norm_final[None, :]
    l = 0
    w = _layer_weights(l, norm_mix, w_in, b_ig, b_fg, ml_norm, w_out, norm_ffn, w_up, w_down)

    zero_state = (jnp.zeros((bp, N_ML_HEADS, ML_HEAD_DIM, ML_HEAD_DIM), F32),
                  jnp.zeros((bp, N_ML_HEADS, 1, LANES), F32), jnp.zeros((bp, N_ML_HEADS, 1, LANES), F32))
    yp, kp, vp, cp, np_, mp = _trunk_layer(
        x_prompt, w, lambda q, k, v: _moba_prompt(q, k, v, t0, t1, rel_bias), zero_state, gl, 512, 256)

    to_pool = lambda a: jnp.transpose(a, (0, 1, 3, 4, 2)).reshape(depth * n_phys, ATT_WIDTH, PAGE_SIZE)
    pool_kt, pool_vt = to_pool(cache_k), to_pool(cache_v)
    pt = page_table + l * n_phys

    def attend_sample(q, k, v):
        return _moba_sample(q, k, v, pool_kt, pool_vt, pt, s_own, s_near, s_far, 4)

    state0 = (state_C[l], state_n[l][:, :, None, :],
              jnp.broadcast_to(state_m[l][:, :, None, None], (db, N_ML_HEADS, 1, LANES)))
    ys, ks, vs, cs, ns, ms = _trunk_layer(x_sample, w, attend_sample, state0, gl, 256, 128)

    st = lambda a: a[None]
    return (yp, ys, st(kp), st(vp), st(cp), st(np_), st(mp), st(ks), st(vs), st(cs), st(ns), st(ms))
```

```python
import functools
import math

import numpy as np
import jax
import jax.numpy as jnp
from jax import lax
from jax.experimental import pallas as pl
from jax.experimental.pallas import tpu as pltpu

F32 = jnp.float32
BF16 = jnp.bfloat16

D_MODEL = 1024
ATT_HEAD_DIM = 64
N_ATT_HEADS = 8
ATT_WIDTH = 512
N_ML_HEADS = 4
ML_HEAD_DIM = 128
ML_WIDTH = 512
N_SLABS = 7
GATE_COL0 = N_SLABS * 512
MOBA_BLOCK = 256
MOBA_TOPK = 3
PAGE_SIZE = 128
NUM_BUCKETS = 32
MAX_DISTANCE = 128
D_FF = 4 * D_MODEL
RMS_EPS = 1e-6
NEG = -1e30
LANES = 128
VMEM_LIMIT = 56 * 1024 * 1024


def _nt_dot(a, b, precision=None):
    return lax.dot_general(a, b, (((1,), (1,)), ((), ())), preferred_element_type=F32, precision=precision)


def _split_bf16(x, parts):
    out = []
    for _ in range(parts):
        p = x.astype(BF16)
        out.append(p)
        x = x - p.astype(F32)
    return out


def _dot_x3(a, b, contract):
    (ah, al), (bh, bl) = _split_bf16(a, 2), _split_bf16(b, 2)
    d = lambda x, y: lax.dot_general(x, y, (contract, ((), ())), preferred_element_type=F32)
    return d(ah, bh) + (d(ah, bl) + d(al, bh))


def _tn_dot(a, b, precision=None):
    return lax.dot_general(a, b, (((0,), (0,)), ((), ())), preferred_element_type=F32, precision=precision)


def _bucket_upper_bounds():
    n = np.arange(0, MAX_DISTANCE + 1)
    max_exact = NUM_BUCKETS // 2
    nf = np.maximum(n, 1).astype(np.float32)
    large = max_exact + (np.log(nf / np.float32(max_exact)) / np.float32(math.log(MAX_DISTANCE / max_exact))
                         * np.float32(NUM_BUCKETS - max_exact)).astype(np.int32)
    large = np.minimum(large, NUM_BUCKETS - 1)
    bucket = np.where(n < max_exact, n, large)
    return np.array([int(n[bucket <= b].max()) for b in range(NUM_BUCKETS - 1)], dtype=np.int32)


def _bias_of_dist(dist, h, rb_ref, ub_ref):
    def body(i, acc):
        b = NUM_BUCKETS - 2 - i
        return jnp.where(dist <= ub_ref[b], rb_ref[b, h], acc)
    init = jnp.full(dist.shape, rb_ref[NUM_BUCKETS - 1, h], F32)
    return lax.fori_loop(0, NUM_BUCKETS - 1, body, init)


def _bias_tiles_kernel(rb_ref, ub_ref, t0_ref, t1_ref, s_own_ref, s_near_ref, s_far_ref, *, dec_seq):
    h = pl.program_id(0)
    kk = lax.broadcasted_iota(jnp.int32, (MOBA_BLOCK, MOBA_BLOCK), 0)
    qq = lax.broadcasted_iota(jnp.int32, (MOBA_BLOCK, MOBA_BLOCK), 1)
    d0 = qq - kk
    t0_ref[0] = jnp.where(d0 >= 0, _bias_of_dist(jnp.maximum(d0, 0), h, rb_ref, ub_ref), NEG)
    t1_ref[0] = _bias_of_dist(d0 + MOBA_BLOCK, h, rb_ref, ub_ref)
    t = lax.broadcasted_iota(jnp.int32, (dec_seq, LANES), 0)
    cc = lax.broadcasted_iota(jnp.int32, (dec_seq, LANES), 1)
    ds = t - cc
    s_own_ref[0] = jnp.where((ds >= 0) & (cc < dec_seq), _bias_of_dist(jnp.maximum(ds, 0), h, rb_ref, ub_ref), NEG)
    s_near_ref[0] = _bias_of_dist(ds + PAGE_SIZE, h, rb_ref, ub_ref)
    s_far_ref[0] = jnp.full((dec_seq, LANES), rb_ref[NUM_BUCKETS - 1, h], F32)


def _bias_tiles(rel_bias, dec_seq):
    ub = jnp.asarray(_bucket_upper_bounds())
    smem = pl.BlockSpec(memory_space=pltpu.SMEM)
    big = pl.BlockSpec((1, MOBA_BLOCK, MOBA_BLOCK), lambda h: (h, 0, 0))
    small = pl.BlockSpec((1, dec_seq, LANES), lambda h: (h, 0, 0))
    return pl.pallas_call(
        functools.partial(_bias_tiles_kernel, dec_seq=dec_seq),
        grid=(N_ATT_HEADS,),
        in_specs=[smem, smem],
        out_specs=[big, big, small, small, small],
        out_shape=[jax.ShapeDtypeStruct((N_ATT_HEADS, MOBA_BLOCK, MOBA_BLOCK), F32)] * 2
        + [jax.ShapeDtypeStruct((N_ATT_HEADS, dec_seq, LANES), F32)] * 3,
        name="bias_tiles",
    )(rel_bias, ub)


def _rms(x, g):
    return x * lax.rsqrt(jnp.mean(x * x, axis=-1, keepdims=True) + RMS_EPS) * g


def _log_sigmoid(z):
    return jnp.minimum(z, 0.0) - jnp.log1p(jnp.exp(-jnp.abs(z)))


def _proj_kernel(x_ref, g_ref, w_ref, wg_ref, bg_ref, aq_ref, ak_ref, av_ref, mq_ref, mk_ref, mv_ref, mo_ref,
                 gate_ref, *kv_t_refs):
    xn = _rms(x_ref[...], g_ref[...])
    xb = xn.astype(BF16)
    outs = (aq_ref, ak_ref, av_ref, mq_ref, mk_ref, mv_ref, mo_ref)
    scales = (ATT_HEAD_DIM ** -0.5, None, None, None, ML_HEAD_DIM ** -0.5, None, None)
    for i, (o_ref, s) in enumerate(zip(outs, scales)):
        r = jnp.dot(xb, w_ref[:, i * 512:(i + 1) * 512], preferred_element_type=F32)
        o_ref[...] = r if s is None else r * s
        if kv_t_refs and i in (1, 2):
            kv_t_refs[i - 1][0] = r.T
    z = _dot_x3(xn, wg_ref[...], ((1,), (0,))) + bg_ref[...]
    lane = lax.broadcasted_iota(jnp.int32, z.shape, 1)
    gate_ref[...] = jnp.where((lane >= N_ML_HEADS) & (lane < 2 * N_ML_HEADS), _log_sigmoid(z), z)


def _project(x2, norm_g, w_main, w_gate, b_gate, tm, seq):
    n = x2.shape[0]
    const = lambda shape: pl.BlockSpec(shape, lambda i: (0, 0), pipeline_mode=pl.Buffered(1))
    slab = pl.BlockSpec((tm, 512), lambda i: (i, 0))
    out_specs = [slab] * N_SLABS + [pl.BlockSpec((tm, LANES), lambda i: (i, 0))]
    out_shape = [jax.ShapeDtypeStruct((n, 512), F32)] * N_SLABS + [jax.ShapeDtypeStruct((n, LANES), F32)]
    if seq % tm == 0:
        tiles = seq // tm
        out_specs += [pl.BlockSpec((1, 512, tm), lambda i: (i // tiles, 0, i % tiles))] * 2
        out_shape += [jax.ShapeDtypeStruct((n // seq, 512, seq), F32)] * 2
    return pl.pallas_call(
        _proj_kernel,
        grid=(n // tm,),
        in_specs=[pl.BlockSpec((tm, D_MODEL), lambda i: (i, 0)), const((1, D_MODEL)),
                  const((D_MODEL, N_SLABS * 512)), const((D_MODEL, LANES)), const((1, LANES))],
        out_specs=out_specs,
        out_shape=out_shape,
        compiler_params=pltpu.CompilerParams(dimension_semantics=("parallel",), vmem_limit_bytes=VMEM_LIMIT),
        name="rms_in_proj",
    )(x2, norm_g, w_main, w_gate, b_gate)


def _topk_penalty(scores, n_valid):
    lane = lax.broadcasted_iota(jnp.int32, scores.shape, 1)
    rank = jnp.zeros(scores.shape, jnp.int32)
    for n in range(n_valid):
        col = scores[:, n:n + 1]
        rank = rank + jnp.where((col > scores) | ((col == scores) & (lane > n)), 1, 0)
    return jnp.where((rank < MOBA_TOPK) & (lane < n_valid), 0.0, NEG)


def _moba_prompt_kernel(rb_ref, q_ref, k_ref, v_ref, t0_ref, t1_ref, o_ref, km_ref, kb_ref, vt_ref, sc_ref,
                        pen_ref, s_ref, sp_ref, *, n_blocks, group):
    hp = pl.program_id(1)
    own = pl.program_id(2)
    blk = MOBA_BLOCK
    nbp = km_ref.shape[0]

    @pl.when(own == 0)
    def _():
        km_ref[...] = jnp.zeros(km_ref.shape, F32)
        for n in range(n_blocks):
            kn = k_ref[0, n * blk:(n + 1) * blk, :]
            km_ref[n:n + 1, :] = jnp.mean(kn, axis=0, keepdims=True)
            kb_ref[n] = kn.astype(BF16)
            vt_ref[n] = v_ref[0, n * blk:(n + 1) * blk, :].T.astype(BF16)

    q2 = q_ref[0]
    lane = lax.broadcasted_iota(jnp.int32, q2.shape, 1)
    q_hi, q_lo = _split_bf16(q2, 2)
    qcat_b = jnp.concatenate([jnp.where((lane // ATT_HEAD_DIM) == hh, q_hi, jnp.zeros_like(q_hi))
                              for hh in range(2)], axis=0)
    wide = 2 * blk
    lane_m = lax.broadcasted_iota(jnp.int32, (nbp, LANES), 1)
    km2 = jnp.concatenate([jnp.where((lane_m // ATT_HEAD_DIM) == hh, km_ref[...], 0.0) for hh in range(2)], axis=0)
    km_hi, km_lo = _split_bf16(km2, 2)
    hi_q = _nt_dot(jnp.concatenate([km_hi, km_lo], axis=0), q_hi)
    sc2 = hi_q[:2 * nbp] + (hi_q[2 * nbp:] + _nt_dot(km_hi, q_lo))
    sc_ref[...] = jnp.concatenate([sc2[:nbp], sc2[nbp:]], axis=1)

    sub = lax.broadcasted_iota(jnp.int32, (nbp, wide), 0)

    def rank_body(n, rank):
        sc = sc_ref[...]
        row = sc_ref[pl.ds(n, 1), :]
        return rank + jnp.where((row > sc) | ((row == sc) & (sub > n)), 1, 0)

    rank = lax.fori_loop(0, own, rank_body, jnp.zeros((nbp, wide), jnp.int32))
    pen_ref[...] = jnp.where((rank < MOBA_TOPK) & (sub < own), 0.0, NEG)

    col_head = lax.broadcasted_iota(jnp.int32, (1, wide), 1) // blk
    far_bias = jnp.where(col_head == 0, rb_ref[NUM_BUCKETS - 1, 2 * hp], rb_ref[NUM_BUCKETS - 1, 2 * hp + 1])

    def fold8(x, op):
        return op(x.reshape(blk // 8, 8, x.shape[1]), axis=0)

    def far_row(j):
        return jnp.where(j < own - 1, pen_ref[pl.ds(j, 1), :] + far_bias, NEG)

    grp = group
    n_groups = (jnp.maximum(own - 1, 0) + grp - 1) // grp
    jn = jnp.maximum(own - 1, 0)
    near_row = jnp.where(own >= 1, pen_ref[pl.ds(jn, 1), :], NEG)

    t0c = jnp.concatenate([t0_ref[0], t0_ref[1]], axis=1)
    t1c = jnp.concatenate([t1_ref[0], t1_ref[1]], axis=1)
    no_prev = own == 0
    s_pair = _nt_dot(kb_ref[pl.ds(jn, 2)].reshape(2 * blk, LANES), qcat_b)
    s_first = s_pair[:blk] + jnp.where(no_prev, t0c, t1c + near_row)
    s_second = jnp.where(no_prev, NEG, s_pair[blk:] + t0c)
    sp_ref[0:blk, :] = s_first
    sp_ref[blk:, :] = s_second
    m8 = jnp.maximum(fold8(s_first, jnp.max), fold8(s_second, jnp.max))

    def max_body(g, m8c):
        j0 = pl.multiple_of(g * grp, grp)
        sg = _nt_dot(kb_ref[pl.ds(j0, grp)].reshape(grp * blk, LANES), qcat_b)
        s_ref[pl.ds(j0, grp)] = sg.reshape(grp, blk, wide)
        for i in range(grp):
            m8c = jnp.maximum(m8c, fold8(sg[i * blk:(i + 1) * blk], jnp.max) + far_row(j0 + i))
        return m8c

    m8 = lax.fori_loop(0, n_groups, max_body, m8)
    m = jnp.max(m8, axis=0, keepdims=True)

    def accumulate(j, p, l8, accs):
        pb = p.astype(BF16)
        new = tuple(accs[hh] + jnp.dot(vt_ref[j, hh * ATT_HEAD_DIM:(hh + 1) * ATT_HEAD_DIM, :],
                                       pb[:, hh * blk:(hh + 1) * blk], preferred_element_type=F32)
                    for hh in range(2))
        return l8 + fold8(p, jnp.sum), new

    p_pair = jnp.exp(sp_ref[...] - m)
    l8 = jnp.sum(p_pair.reshape(2 * blk // 8, 8, wide), axis=0)
    pb_pair = p_pair.astype(BF16)
    vt_pair = vt_ref[pl.ds(jn, 2)]
    accs = tuple(
        jnp.dot(jnp.concatenate([vt_pair[i, hh * ATT_HEAD_DIM:(hh + 1) * ATT_HEAD_DIM, :] for i in range(2)], axis=1),
                pb_pair[:, hh * blk:(hh + 1) * blk], preferred_element_type=F32)
        for hh in range(2))

    def exp_body(g, st):
        l8c, accc = st
        j0 = pl.multiple_of(g * grp, grp)
        for i in range(grp):
            l8c, accc = accumulate(j0 + i, jnp.exp(s_ref[j0 + i] + (far_row(j0 + i) - m)), l8c, accc)
        return l8c, accc

    l8, accs = lax.fori_loop(0, n_groups, exp_body, (l8, accs))
    l = jnp.sum(l8, axis=0, keepdims=True)
    out_t = jnp.concatenate([accs[hh] / l[:, hh * blk:(hh + 1) * blk] for hh in range(2)], axis=0)
    o_ref[0] = out_t.T


def _moba_prompt(aq, ak, av, t0, t1, rel_bias, group):
    b, s, _ = aq.shape
    nb = s // MOBA_BLOCK
    nbp = -(-nb // 8) * 8
    n_pairs = N_ATT_HEADS // 2
    wide = 2 * MOBA_BLOCK
    assert nb % group == 0
    qspec = pl.BlockSpec((1, MOBA_BLOCK, LANES), lambda bi, hp, qi: (bi, qi, hp))
    kvspec = pl.BlockSpec((1, s, LANES), lambda bi, hp, qi: (bi, 0, hp))
    tspec = pl.BlockSpec((2, MOBA_BLOCK, MOBA_BLOCK), lambda bi, hp, qi: (hp, 0, 0))
    return pl.pallas_call(
        functools.partial(_moba_prompt_kernel, n_blocks=nb, group=group),
        grid=(b, n_pairs, nb),
        in_specs=[pl.BlockSpec(memory_space=pltpu.SMEM), qspec, kvspec, kvspec, tspec, tspec],
        out_specs=qspec,
        out_shape=jax.ShapeDtypeStruct((b, s, ATT_WIDTH), F32),
        scratch_shapes=[pltpu.VMEM((nbp, LANES), F32), pltpu.VMEM((nb, MOBA_BLOCK, LANES), BF16),
                        pltpu.VMEM((nb, LANES, MOBA_BLOCK), BF16),
                        pltpu.VMEM((nbp, wide), F32), pltpu.VMEM((nbp, wide), F32),
                        pltpu.VMEM((nb, MOBA_BLOCK, wide), F32), pltpu.VMEM((2 * MOBA_BLOCK, wide), F32)],
        compiler_params=pltpu.CompilerParams(dimension_semantics=("parallel", "parallel", "arbitrary"),
                                             vmem_limit_bytes=VMEM_LIMIT),
        name="moba_prompt",
    )(rel_bias, aq, ak, av, t0, t1)


def _moba_sample_kernel(pt_ref, *refs, blocks_per_step, n_blocks, dec_seq):
    ppb = MOBA_BLOCK // PAGE_SIZE
    pps = blocks_per_step * ppb
    k_pages, v_pages = refs[:pps], refs[pps:2 * pps]
    (q_ref, kn_ref, vn_ref, s_own_ref, s_near_ref, s_far_ref, o_ref,
     qbd_ref, kmt_ref, mm_ref, ll_ref, oo_ref, own_m_ref, own_l_ref, own_o_ref) = refs[2 * pps:]
    c = pl.program_id(1)
    rows = N_ATT_HEADS * dec_seq
    row_head = lax.broadcasted_iota(jnp.int32, (rows, ATT_WIDTH), 0) // dec_seq
    lane_head = lax.broadcasted_iota(jnp.int32, (rows, ATT_WIDTH), 1) // ATT_HEAD_DIM
    lane_r = lax.broadcasted_iota(jnp.int32, (rows, LANES), 1)
    lane_k = lax.broadcasted_iota(jnp.int32, (ATT_WIDTH, LANES), 1)

    def local_softmax(s):
        m = jnp.max(s, axis=-1, keepdims=True)
        p = jnp.exp(s - m)
        return m, jnp.sum(p, axis=-1, keepdims=True), p.astype(BF16)

    @pl.when(c == 0)
    def _():
        qrep = jnp.concatenate([q_ref[0]] * N_ATT_HEADS, axis=0)
        qbd = jnp.where(row_head == lane_head, qrep, 0.0)
        qbd_ref[...] = qbd
        kmt_ref[...] = jnp.zeros(kmt_ref.shape, F32)
        mm_ref[...] = jnp.zeros(mm_ref.shape, F32)
        ll_ref[...] = jnp.zeros(ll_ref.shape, F32)
        pad = jnp.zeros((LANES - dec_seq, ATT_WIDTH), F32)
        kn = jnp.concatenate([kn_ref[0], pad], axis=0).astype(BF16)
        vn = jnp.concatenate([vn_ref[0], pad], axis=0).astype(BF16)
        m, l, p = local_softmax(_nt_dot(qbd.astype(BF16), kn) + s_own_ref[...].reshape(rows, LANES))
        own_m_ref[...] = jnp.broadcast_to(m, (rows, LANES))
        own_l_ref[...] = jnp.broadcast_to(l, (rows, LANES))
        own_o_ref[...] = jnp.dot(p, vn, preferred_element_type=F32)

    qb = qbd_ref[...].astype(BF16)
    s_near = s_near_ref[...].reshape(rows, LANES)
    s_far = s_far_ref[...].reshape(rows, LANES)
    n0 = c * blocks_per_step
    kt = jnp.concatenate([k_pages[i][0] for i in range(pps)], axis=1)
    s_all = jnp.dot(qb, kt.astype(BF16), preferred_element_type=F32)
    mm, ll, kmt = mm_ref[...], ll_ref[...], kmt_ref[...]
    for g in range(blocks_per_step):
        n = n0 + g
        bias = jnp.concatenate([s_far] * (ppb - 1) + [jnp.where(n == n_blocks - 1, s_near, s_far)], axis=1)
        m, l, p = local_softmax(s_all[:, g * MOBA_BLOCK:(g + 1) * MOBA_BLOCK] + bias)
        vt = jnp.concatenate([v_pages[ppb * g + i][0] for i in range(ppb)], axis=1).astype(BF16)
        oo_ref[n] = _nt_dot(p, vt)
        mm = jnp.where(lane_r == n, m, mm)
        ll = jnp.where(lane_r == n, l, ll)
        ksum = jnp.sum(kt[:, g * MOBA_BLOCK:(g + 1) * MOBA_BLOCK], axis=-1, keepdims=True)
        kmt = jnp.where(lane_k == n, ksum, kmt)
    mm_ref[...] = mm
    ll_ref[...] = ll
    kmt_ref[...] = kmt

    @pl.when(c == pl.num_programs(1) - 1)
    def _():
        kmean_t = kmt_ref[...] * (1.0 / MOBA_BLOCK)
        pen = _topk_penalty(_dot_x3(qbd_ref[...], kmean_t, ((1,), (0,))), n_blocks)
        mp = mm_ref[...] + pen
        m_own = own_m_ref[...]
        m_tot = jnp.maximum(jnp.max(mp, axis=-1, keepdims=True), m_own)
        w = jnp.exp(mp - m_tot)
        w_own = jnp.exp(m_own - m_tot)
        den = jnp.sum(w * ll_ref[...], axis=-1, keepdims=True) + w_own * own_l_ref[...]

        num = w_own[:, 0:1] * own_o_ref[...]
        for nn in range(n_blocks):
            num = num + w[:, nn:nn + 1] * oo_ref[nn]
        full = jnp.where(row_head == lane_head, num / den[:, 0:1], 0.0)
        out = full[0:dec_seq]
        for h in range(1, N_ATT_HEADS):
            out = out + full[h * dec_seq:(h + 1) * dec_seq]
        o_ref[0] = out


def _moba_sample(aq, ak, av, pool_kt, pool_vt, page_table, s_own, s_near, s_far, blocks_per_step):
    db, dec_seq, _ = aq.shape
    n_pages = page_table.shape[1]
    ppb = MOBA_BLOCK // PAGE_SIZE
    nb = n_pages // ppb
    pps = blocks_per_step * ppb
    assert nb % blocks_per_step == 0 and nb <= LANES
    rows = N_ATT_HEADS * dec_seq

    def page_spec(i):
        return pl.BlockSpec((1, ATT_WIDTH, PAGE_SIZE), lambda b, c, pt: (pt[b, c * pps + i], 0, 0))

    per_b = pl.BlockSpec((1, dec_seq, ATT_WIDTH), lambda b, c, pt: (b, 0, 0))
    tile = pl.BlockSpec((N_ATT_HEADS, dec_seq, LANES), lambda b, c, pt: (0, 0, 0))
    grid_spec = pltpu.PrefetchScalarGridSpec(
        num_scalar_prefetch=1, grid=(db, nb // blocks_per_step),
        in_specs=[page_spec(i) for i in range(pps)] * 2 + [per_b, per_b, per_b, tile, tile, tile],
        out_specs=per_b,
        scratch_shapes=[pltpu.VMEM((rows, ATT_WIDTH), F32), pltpu.VMEM((ATT_WIDTH, LANES), F32),
                        pltpu.VMEM((rows, LANES), F32), pltpu.VMEM((rows, LANES), F32),
                        pltpu.VMEM((nb, rows, ATT_WIDTH), F32), pltpu.VMEM((rows, LANES), F32),
                        pltpu.VMEM((rows, LANES), F32), pltpu.VMEM((rows, ATT_WIDTH), F32)])
    return pl.pallas_call(
        functools.partial(_moba_sample_kernel, blocks_per_step=blocks_per_step, n_blocks=nb, dec_seq=dec_seq),
        grid_spec=grid_spec,
        out_shape=jax.ShapeDtypeStruct((db, dec_seq, ATT_WIDTH), F32),
        compiler_params=pltpu.CompilerParams(dimension_semantics=("parallel", "arbitrary"),
                                             vmem_limit_bytes=VMEM_LIMIT),
        name="moba_sample",
    )(page_table, *([pool_kt] * pps), *([pool_vt] * pps), aq, ak, av, s_own, s_near, s_far)


def _mlstm_kernel(q_ref, k_ref, v_ref, g_ref, mo_ref, nw_ref, c0_ref, n0_ref, m0_ref,
                  h_ref, c_out, n_out, m_out, c_s, n_s, m_s, *, rows, chunk):
    ci = pl.program_id(1)
    L = chunk

    @pl.when(ci == 0)
    def _():
        c_s[...] = c0_ref[0]
        n_s[...] = n0_ref[0]
        m_s[...] = m0_ref[0]

    def padded(ref, fill=None):
        x = ref[0]
        if rows == L:
            return x
        tail = jnp.zeros((L - rows, x.shape[1]), F32) if fill is None else fill
        return jnp.concatenate([x, tail], axis=0)

    lane = lax.broadcasted_iota(jnp.int32, (L, LANES), 1)
    pad_lane = lax.broadcasted_iota(jnp.int32, (max(L - rows, 1), LANES), 1)
    g = padded(g_ref, jnp.where(pad_lane < N_ML_HEADS, NEG, 0.0))
    q_all, k_all, v_all, mo_all = padded(q_ref), padded(k_ref), padded(v_ref), padded(mo_ref)

    is_lf = (lane >= N_ML_HEADS) & (lane < 2 * N_ML_HEADS)
    tri_r = lax.broadcasted_iota(jnp.int32, (L, L), 0)
    tri_c = lax.broadcasted_iota(jnp.int32, (L, L), 1)
    causal = tri_c <= tri_r
    g3 = jnp.concatenate(_split_bf16(jnp.where(is_lf, g, 0.0), 3), axis=1)
    b3 = jnp.dot(causal.astype(BF16), g3, preferred_element_type=F32)
    bcum = b3[:, :LANES] + (b3[:, LANES:2 * LANES] + b3[:, 2 * LANES:])
    g_t = g.T
    b_t = bcum.T

    for h in range(N_ML_HEADS):
        sl = slice(h * ML_HEAD_DIM, (h + 1) * ML_HEAD_DIM)
        q, k, v = q_all[:, sl], k_all[:, sl], v_all[:, sl]
        ig_col = g[:, h:h + 1]
        b_col = bcum[:, N_ML_HEADS + h:N_ML_HEADS + h + 1]
        ig_row = g_t[h:h + 1, :]
        b_row = b_t[N_ML_HEADS + h:N_ML_HEADS + h + 1, :]
        m_prev = m_s[h][:, 0:1]
        C = c_s[h]
        n_row = n_s[h]

        a_col = b_col + m_prev
        log_d = jnp.where(causal, b_col - b_row + ig_row, NEG)
        m_t = jnp.maximum(a_col, jnp.max(log_d, axis=-1, keepdims=True))
        d = jnp.exp(log_d - m_t)
        w_state = jnp.exp(a_col - m_t)
        qb, kb, vb = q.astype(BF16), k.astype(BF16), v.astype(BF16)
        s = _nt_dot(qb, kb) * d
        num = w_state * jnp.dot(qb, C.astype(BF16), preferred_element_type=F32) \
            + jnp.dot(s.astype(BF16), vb, preferred_element_type=F32)
        den = w_state * jnp.sum(q * n_row, axis=-1, keepdims=True) + jnp.sum(s, axis=-1, keepdims=True)
        hid = num / jnp.maximum(jnp.abs(den), jnp.exp(-m_t))

        hn = _rms(hid, nw_ref[:, sl])
        h_ref[0, :, sl] = (hn * jax.nn.sigmoid(mo_all[:, sl]))[:rows]

        b_last = b_col[L - 1:L, :]
        m_new = m_t[L - 1:L, :]
        g_state = jnp.exp(b_last + m_prev - m_new)
        g_tok = jnp.exp(b_last - b_col + ig_col - m_new)
        kg = k * g_tok
        c_s[h] = g_state * C + _tn_dot(kg.astype(BF16), vb)
        n_s[h] = g_state * n_row + jnp.sum(kg, axis=0, keepdims=True)
        m_s[h] = jnp.broadcast_to(m_new, (1, LANES))

    @pl.when(ci == pl.num_programs(1) - 1)
    def _():
        c_out[0] = c_s[...]
        n_out[0] = n_s[...]
        m_out[0] = m_s[...]


def _mlstm(mq, mk, mv, gates, mo, ml_norm, c0, n0, m0, chunk):
    b, s, _ = mq.shape
    rows = min(s, chunk)
    n_chunks = s // rows
    tok = lambda w: pl.BlockSpec((1, rows, w), lambda bi, ci: (bi, ci, 0))
    st4 = lambda shape: pl.BlockSpec(shape, lambda bi, ci: (bi, 0, 0, 0))
    c_shape, v_shape = (1, N_ML_HEADS, ML_HEAD_DIM, ML_HEAD_DIM), (1, N_ML_HEADS, 1, LANES)
    return pl.pallas_call(
        functools.partial(_mlstm_kernel, rows=rows, chunk=chunk),
        grid=(b, n_chunks),
        in_specs=[tok(ML_WIDTH), tok(ML_WIDTH), tok(ML_WIDTH), tok(LANES), tok(ML_WIDTH),
                  pl.BlockSpec((1, ML_WIDTH), lambda bi, ci: (0, 0)),
                  st4(c_shape), st4(v_shape), st4(v_shape)],
        out_specs=[tok(ML_WIDTH), st4(c_shape), st4(v_shape), st4(v_shape)],
        out_shape=[jax.ShapeDtypeStruct((b, s, ML_WIDTH), F32),
                   jax.ShapeDtypeStruct((b,) + c_shape[1:], F32),
                   jax.ShapeDtypeStruct((b,) + v_shape[1:], F32),
                   jax.ShapeDtypeStruct((b,) + v_shape[1:], F32)],
        scratch_shapes=[pltpu.VMEM(c_shape[1:], F32), pltpu.VMEM(v_shape[1:], F32), pltpu.VMEM(v_shape[1:], F32)],
        compiler_params=pltpu.CompilerParams(dimension_semantics=("parallel", "arbitrary"),
                                             vmem_limit_bytes=VMEM_LIMIT),
        name="mlstm",
    )(mq, mk, mv, gates, mo, ml_norm, c0, n0, m0)


def _out_ffn_kernel(x_ref, att_ref, ml_ref, wo_ref, gf_ref, wu_ref, wd_ref, gl_ref, y_ref, *, ff_chunk):
    x1 = x_ref[...] \
        + jnp.dot(att_ref[...].astype(BF16), wo_ref[0:ATT_WIDTH, :], preferred_element_type=F32) \
        + jnp.dot(ml_ref[...].astype(BF16), wo_ref[ATT_WIDTH:, :], preferred_element_type=F32)
    xb = _rms(x1, gf_ref[...]).astype(BF16)
    ffn = None
    for c in range(D_FF // ff_chunk):
        sl = slice(c * ff_chunk, (c + 1) * ff_chunk)
        u = jnp.maximum(jnp.dot(xb, wu_ref[:, sl], preferred_element_type=F32), 0.0)
        d = jnp.dot((u * u).astype(BF16), wd_ref[sl, :], preferred_element_type=F32)
        ffn = d if ffn is None else ffn + d
    y_ref[...] = _rms(x1 + ffn, gl_ref[...])


def _out_ffn(x2, att, ml, w_out, norm_ffn, w_up, w_down, norm_final, tm, ff_chunk):
    n = x2.shape[0]
    const = lambda shape: pl.BlockSpec(shape, lambda i: (0, 0), pipeline_mode=pl.Buffered(1))
    row = lambda w: pl.BlockSpec((tm, w), lambda i: (i, 0))
    return pl.pallas_call(
        functools.partial(_out_ffn_kernel, ff_chunk=ff_chunk),
        grid=(n // tm,),
        in_specs=[row(D_MODEL), row(ATT_WIDTH), row(ML_WIDTH), const((D_MODEL, D_MODEL)), const((1, D_MODEL)),
                  const((D_MODEL, D_FF)), const((D_FF, D_MODEL)), const((1, D_MODEL))],
        out_specs=row(D_MODEL),
        out_shape=jax.ShapeDtypeStruct((n, D_MODEL), F32),
        compiler_params=pltpu.CompilerParams(dimension_semantics=("parallel",), vmem_limit_bytes=VMEM_LIMIT),
        name="out_proj_ffn",
    )(x2, att, ml, w_out, norm_ffn, w_up, w_down, norm_final)


def _layer_weights(l, norm_mix, w_in, b_ig, b_fg, ml_norm, w_out, norm_ffn, w_up, w_down):
    w_main = w_in[l, :, :GATE_COL0].astype(BF16)
    w_gate = jnp.pad(w_in[l, :, GATE_COL0:], ((0, 0), (0, LANES - 2 * N_ML_HEADS)))
    b_gate = jnp.pad(jnp.concatenate([b_ig[l], b_fg[l]]), (0, LANES - 2 * N_ML_HEADS))[None, :]
    return dict(norm_mix=norm_mix[l][None, :], w_main=w_main, w_gate=w_gate, b_gate=b_gate,
                ml_norm=ml_norm[l][None, :], w_out=w_out[l].astype(BF16), norm_ffn=norm_ffn[l][None, :],
                w_up=w_up[l].astype(BF16), w_down=w_down[l].astype(BF16))


def _trunk_layer(x, w, attend, state0, norm_final, tm, ml_chunk):
    b, s, _ = x.shape
    x2 = x.reshape(b * s, D_MODEL)
    aq, ak, av, mq, mk, mv, mo, gates, *kv_t = _project(x2, w["norm_mix"], w["w_main"], w["w_gate"], w["b_gate"], tm, s)
    r3 = lambda a: a.reshape(b, s, a.shape[-1])
    if kv_t:
        k_out, v_out = (a.reshape(b, N_ATT_HEADS, ATT_HEAD_DIM, s).transpose(0, 3, 1, 2) for a in kv_t)
    else:
        k_out, v_out = (a.reshape(b, s, N_ATT_HEADS, ATT_HEAD_DIM) for a in (ak, av))
    att = attend(r3(aq), r3(ak), r3(av))
    c0, n0, m0 = state0
    ml, c_new, n_new, m_new = _mlstm(r3(mq), r3(mk), r3(mv), r3(gates), r3(mo), w["ml_norm"], c0, n0, m0, ml_chunk)
    y = _out_ffn(x2, att.reshape(b * s, ATT_WIDTH), ml.reshape(b * s, ML_WIDTH), w["w_out"], w["norm_ffn"],
                 w["w_up"], w["w_down"], norm_final, tm, 1024)
    return y.reshape(b, s, D_MODEL), k_out, v_out, c_new, n_new[:, :, 0, :], m_new[:, :, 0, 0]


def kernel(x_prompt, x_sample, cache_k, cache_v, state_C, state_n, state_m, page_table, rel_bias, norm_mix, w_in,
           b_ig, b_fg, ml_norm, w_out, norm_ffn, w_up, w_down, norm_final):
    depth = w_in.shape[0]
    assert depth == 1, "the fused final RMSNorm assumes a single layer"
    bp, sp, _ = x_prompt.shape
    db, dec_seq, _ = x_sample.shape
    n_phys = cache_k.shape[1]
    t0, t1, s_own, s_near, s_far = _bias_tiles(rel_bias, dec_seq)
    gl = norm_final[None, :]
    l = 0
    w = _layer_weights(l, norm_mix, w_in, b_ig, b_fg, ml_norm, w_out, norm_ffn, w_up, w_down)

    zero_state = (jnp.zeros((bp, N_ML_HEADS, ML_HEAD_DIM, ML_HEAD_DIM), F32),
                  jnp.zeros((bp, N_ML_HEADS, 1, LANES), F32), jnp.zeros((bp, N_ML_HEADS, 1, LANES), F32))
    yp, kp, vp, cp, np_, mp = _trunk_layer(
        x_prompt, w, lambda q, k, v: _moba_prompt(q, k, v, t0, t1, rel_bias, 4), zero_state, gl, 512, 256)

    to_pool = lambda a: jnp.transpose(a, (0, 1, 3, 4, 2)).reshape(depth * n_phys, ATT_WIDTH, PAGE_SIZE)
    pool_kt, pool_vt = to_pool(cache_k), to_pool(cache_v)
    pt = page_table + l * n_phys

    def attend_sample(q, k, v):
        return _moba_sample(q, k, v, pool_kt, pool_vt, pt, s_own, s_near, s_far, 16)

    state0 = (state_C[l], state_n[l][:, :, None, :],
              jnp.broadcast_to(state_m[l][:, :, None, None], (db, N_ML_HEADS, 1, LANES)))
    ys, ks, vs, cs, ns, ms = _trunk_layer(x_sample, w, attend_sample, state0, gl, 256, 128)

    st = lambda a: a[None]
    return (yp, ys, st(kp), st(vp), st(cp), st(np_), st(mp), st(ks), st(vs), st(cs), st(ns), st(ms))
```

```python
import functools
import math

import numpy as np
import jax
import jax.numpy as jnp
from jax import lax
from jax.experimental import pallas as pl
from jax.experimental.pallas import tpu as pltpu

F32 = jnp.float32
BF16 = jnp.bfloat16

D_MODEL = 1024
ATT_HEAD_DIM = 64
N_ATT_HEADS = 8
ATT_WIDTH = 512
N_ML_HEADS = 4
ML_HEAD_DIM = 128
ML_WIDTH = 512
N_SLABS = 7
GATE_COL0 = N_SLABS * 512
MOBA_BLOCK = 256
MOBA_TOPK = 3
PAGE_SIZE = 128
NUM_BUCKETS = 32
MAX_DISTANCE = 128
D_FF = 4 * D_MODEL
RMS_EPS = 1e-6
NEG = -1e30
LOG2E = math.log2(math.e)
LANES = 128
SUM_ROWS = 16
VMEM_LIMIT = 56 * 1024 * 1024


def _nt_dot(a, b, precision=None):
    return lax.dot_general(a, b, (((1,), (1,)), ((), ())), preferred_element_type=F32, precision=precision)


def _split_bf16(x, parts):
    out = []
    for _ in range(parts):
        p = x.astype(BF16)
        out.append(p)
        x = x - p.astype(F32)
    return out


def _dot_x3(a, b, contract):
    (ah, al), (bh, bl) = _split_bf16(a, 2), _split_bf16(b, 2)
    d = lambda x, y: lax.dot_general(x, y, (contract, ((), ())), preferred_element_type=F32)
    return d(ah, bh) + (d(ah, bl) + d(al, bh))


def _tn_dot(a, b, precision=None):
    return lax.dot_general(a, b, (((0,), (0,)), ((), ())), preferred_element_type=F32, precision=precision)


def _bucket_upper_bounds():
    n = np.arange(0, MAX_DISTANCE + 1)
    max_exact = NUM_BUCKETS // 2
    nf = np.maximum(n, 1).astype(np.float32)
    large = max_exact + (np.log(nf / np.float32(max_exact)) / np.float32(math.log(MAX_DISTANCE / max_exact))
                         * np.float32(NUM_BUCKETS - max_exact)).astype(np.int32)
    large = np.minimum(large, NUM_BUCKETS - 1)
    bucket = np.where(n < max_exact, n, large)
    return np.array([int(n[bucket <= b].max()) for b in range(NUM_BUCKETS - 1)], dtype=np.int32)


def _bias_of_dist(dist, h, rb_ref, ub_ref):
    def body(i, acc):
        b = NUM_BUCKETS - 2 - i
        return jnp.where(dist <= ub_ref[b], rb_ref[b, h], acc)
    init = jnp.full(dist.shape, rb_ref[NUM_BUCKETS - 1, h], F32)
    return lax.fori_loop(0, NUM_BUCKETS - 1, body, init)


def _bias_tiles_kernel(rb_ref, ub_ref, t0_ref, t1_ref, s_own_ref, s_near_ref, s_far_ref, *, dec_seq):
    h = pl.program_id(0)
    kk = lax.broadcasted_iota(jnp.int32, (MOBA_BLOCK, MOBA_BLOCK), 0)
    qq = lax.broadcasted_iota(jnp.int32, (MOBA_BLOCK, MOBA_BLOCK), 1)
    d0 = qq - kk
    t0_ref[0] = jnp.where(d0 >= 0, _bias_of_dist(jnp.maximum(d0, 0), h, rb_ref, ub_ref) * LOG2E, NEG)
    t1_ref[0] = _bias_of_dist(d0 + MOBA_BLOCK, h, rb_ref, ub_ref) * LOG2E
    t = lax.broadcasted_iota(jnp.int32, (dec_seq, LANES), 0)
    cc = lax.broadcasted_iota(jnp.int32, (dec_seq, LANES), 1)
    ds = t - cc
    s_own_ref[0] = jnp.where((ds >= 0) & (cc < dec_seq),
                             _bias_of_dist(jnp.maximum(ds, 0), h, rb_ref, ub_ref) * LOG2E, NEG)
    s_near_ref[0] = _bias_of_dist(ds + PAGE_SIZE, h, rb_ref, ub_ref) * LOG2E
    s_far_ref[0] = jnp.full((dec_seq, LANES), rb_ref[NUM_BUCKETS - 1, h] * LOG2E, F32)


def _bias_tiles(rel_bias, dec_seq):
    ub = jnp.asarray(_bucket_upper_bounds())
    smem = pl.BlockSpec(memory_space=pltpu.SMEM)
    big = pl.BlockSpec((1, MOBA_BLOCK, MOBA_BLOCK), lambda h: (h, 0, 0))
    small = pl.BlockSpec((1, dec_seq, LANES), lambda h: (h, 0, 0))
    return pl.pallas_call(
        functools.partial(_bias_tiles_kernel, dec_seq=dec_seq),
        grid=(N_ATT_HEADS,),
        in_specs=[smem, smem],
        out_specs=[big, big, small, small, small],
        out_shape=[jax.ShapeDtypeStruct((N_ATT_HEADS, MOBA_BLOCK, MOBA_BLOCK), F32)] * 2
        + [jax.ShapeDtypeStruct((N_ATT_HEADS, dec_seq, LANES), F32)] * 3,
        name="bias_tiles",
    )(rel_bias, ub)


def _rms(x, g):
    return x * lax.rsqrt(jnp.mean(x * x, axis=-1, keepdims=True) + RMS_EPS) * g


def _log_sigmoid(z):
    return jnp.minimum(z, 0.0) - jnp.log1p(jnp.exp(-jnp.abs(z)))


def _proj_kernel(x_ref, g_ref, w_ref, wg_ref, bg_ref, aq_ref, ak_ref, av_ref, mq_ref, mk_ref, mv_ref, mo_ref,
                 gate_ref, *kv_t_refs):
    xn = _rms(x_ref[...], g_ref[...])
    xb = xn.astype(BF16)
    outs = (aq_ref, ak_ref, av_ref, mq_ref, mk_ref, mv_ref, mo_ref)
    scales = (ATT_HEAD_DIM ** -0.5 * LOG2E, None, None, None, ML_HEAD_DIM ** -0.5, None, None)
    for i, (o_ref, s) in enumerate(zip(outs, scales)):
        r = jnp.dot(xb, w_ref[:, i * 512:(i + 1) * 512], preferred_element_type=F32)
        o_ref[...] = r if s is None else r * s
        if kv_t_refs and i in (1, 2):
            kv_t_refs[i - 1][0] = r.T
    z = _dot_x3(xn, wg_ref[...], ((1,), (0,))) + bg_ref[...]
    lane = lax.broadcasted_iota(jnp.int32, z.shape, 1)
    gate_ref[...] = jnp.where((lane >= N_ML_HEADS) & (lane < 2 * N_ML_HEADS), _log_sigmoid(z), z)


def _project(x2, norm_g, w_main, w_gate, b_gate, tm, seq):
    n = x2.shape[0]
    const = lambda shape: pl.BlockSpec(shape, lambda i: (0, 0), pipeline_mode=pl.Buffered(1))
    slab = pl.BlockSpec((tm, 512), lambda i: (i, 0))
    out_specs = [slab] * N_SLABS + [pl.BlockSpec((tm, LANES), lambda i: (i, 0))]
    out_shape = [jax.ShapeDtypeStruct((n, 512), F32)] * N_SLABS + [jax.ShapeDtypeStruct((n, LANES), F32)]
    if seq % tm == 0:
        tiles = seq // tm
        out_specs += [pl.BlockSpec((1, 512, tm), lambda i: (i // tiles, 0, i % tiles))] * 2
        out_shape += [jax.ShapeDtypeStruct((n // seq, 512, seq), F32)] * 2
    return pl.pallas_call(
        _proj_kernel,
        grid=(n // tm,),
        in_specs=[pl.BlockSpec((tm, D_MODEL), lambda i: (i, 0)), const((1, D_MODEL)),
                  const((D_MODEL, N_SLABS * 512)), const((D_MODEL, LANES)), const((1, LANES))],
        out_specs=out_specs,
        out_shape=out_shape,
        compiler_params=pltpu.CompilerParams(dimension_semantics=("parallel",), vmem_limit_bytes=VMEM_LIMIT),
        name="rms_in_proj",
    )(x2, norm_g, w_main, w_gate, b_gate)


def _topk_penalty(scores, n_valid):
    lane = lax.broadcasted_iota(jnp.int32, scores.shape, 1)
    rank = jnp.zeros(scores.shape, jnp.int32)
    for n in range(n_valid):
        col = scores[:, n:n + 1]
        rank = rank + jnp.where((col > scores) | ((col == scores) & (lane > n)), 1, 0)
    return jnp.where((rank < MOBA_TOPK) & (lane < n_valid), 0.0, NEG)


def _moba_prompt_kernel(rb_ref, q_ref, k_ref, v_ref, t0_ref, t1_ref, o_ref, km_ref, kb_ref, vt_ref, sc_ref,
                        pen_ref, s_ref, sp_ref, *, n_blocks, group):
    hp = pl.program_id(1)
    own = pl.program_id(2)
    blk = MOBA_BLOCK
    nbp = km_ref.shape[0]

    @pl.when(own == 0)
    def _():
        km_ref[...] = jnp.zeros(km_ref.shape, F32)
        for n in range(n_blocks):
            kn = k_ref[0, n * blk:(n + 1) * blk, :]
            km_ref[n:n + 1, :] = jnp.mean(kn, axis=0, keepdims=True)
            kb_ref[n] = kn.astype(BF16)
            vtn = v_ref[0, n * blk:(n + 1) * blk, :].T.astype(BF16)
            for hh in range(2):
                vt_ref[n, hh, 0:ATT_HEAD_DIM, :] = vtn[hh * ATT_HEAD_DIM:(hh + 1) * ATT_HEAD_DIM]
                vt_ref[n, hh, ATT_HEAD_DIM:, :] = jnp.ones((SUM_ROWS, blk), BF16)

    q2 = q_ref[0]
    lane = lax.broadcasted_iota(jnp.int32, q2.shape, 1)
    q_hi, q_lo = _split_bf16(q2, 2)
    qcat_b = jnp.concatenate([jnp.where((lane // ATT_HEAD_DIM) == hh, q_hi, jnp.zeros_like(q_hi))
                              for hh in range(2)], axis=0)
    wide = 2 * blk
    lane_m = lax.broadcasted_iota(jnp.int32, (nbp, LANES), 1)
    km2 = jnp.concatenate([jnp.where((lane_m // ATT_HEAD_DIM) == hh, km_ref[...], 0.0) for hh in range(2)], axis=0)
    km_hi, km_lo = _split_bf16(km2, 2)
    hi_q = _nt_dot(jnp.concatenate([km_hi, km_lo], axis=0), q_hi)
    sc2 = hi_q[:2 * nbp] + (hi_q[2 * nbp:] + _nt_dot(km_hi, q_lo))
    sc_ref[...] = jnp.concatenate([sc2[:nbp], sc2[nbp:]], axis=1)

    sub = lax.broadcasted_iota(jnp.int32, (nbp, wide), 0)

    def rank_body(n, rank):
        sc = sc_ref[...]
        row = sc_ref[pl.ds(n, 1), :]
        return rank + jnp.where((row > sc) | ((row == sc) & (sub > n)), 1, 0)

    rank = lax.fori_loop(0, own, rank_body, jnp.zeros((nbp, wide), jnp.int32))
    pen_ref[...] = jnp.where((rank < MOBA_TOPK) & (sub < own), 0.0, NEG)

    col_head = lax.broadcasted_iota(jnp.int32, (1, wide), 1) // blk
    far_bias = jnp.where(col_head == 0, rb_ref[NUM_BUCKETS - 1, 2 * hp], rb_ref[NUM_BUCKETS - 1, 2 * hp + 1]) * LOG2E

    def fold8(x, op):
        return op(x.reshape(blk // 8, 8, x.shape[1]), axis=0)

    def far_row(j):
        return jnp.where(j < own - 1, pen_ref[pl.ds(j, 1), :] + far_bias, NEG)

    grp = group
    n_groups = (jnp.maximum(own - 1, 0) + grp - 1) // grp
    jn = jnp.maximum(own - 1, 0)
    near_row = jnp.where(own >= 1, pen_ref[pl.ds(jn, 1), :], NEG)

    t0c = jnp.concatenate([t0_ref[0], t0_ref[1]], axis=1)
    t1c = jnp.concatenate([t1_ref[0], t1_ref[1]], axis=1)
    no_prev = own == 0
    s_pair = _nt_dot(kb_ref[pl.ds(jn, 2)].reshape(2 * blk, LANES), qcat_b)
    s_first = s_pair[:blk] + jnp.where(no_prev, t0c, t1c + near_row)
    s_second = jnp.where(no_prev, NEG, s_pair[blk:] + t0c)
    sp_ref[0:blk, :] = s_first
    sp_ref[blk:, :] = s_second
    m8 = jnp.maximum(fold8(s_first, jnp.max), fold8(s_second, jnp.max))

    def max_body(g, m8c):
        j0 = pl.multiple_of(g * grp, grp)
        sg = _nt_dot(kb_ref[pl.ds(j0, grp)].reshape(grp * blk, LANES), qcat_b)
        s_ref[pl.ds(j0, grp)] = sg.reshape(grp, blk, wide)
        for i in range(grp):
            m8c = jnp.maximum(m8c, fold8(sg[i * blk:(i + 1) * blk], jnp.max) + far_row(j0 + i))
        return m8c

    m8 = lax.fori_loop(0, n_groups, max_body, m8)
    m = jnp.max(m8, axis=0, keepdims=True)

    def accumulate(j, p, accs):
        pb = p.astype(BF16)
        return tuple(accs[hh] + jnp.dot(vt_ref[j, hh], pb[:, hh * blk:(hh + 1) * blk], preferred_element_type=F32)
                     for hh in range(2))

    pb_pair = jnp.exp2(sp_ref[...] - m).astype(BF16)
    vt_pair = vt_ref[pl.ds(jn, 2)]
    accs = tuple(jnp.dot(jnp.concatenate([vt_pair[i, hh] for i in range(2)], axis=1),
                         pb_pair[:, hh * blk:(hh + 1) * blk], preferred_element_type=F32) for hh in range(2))

    def exp_body(g, accc):
        j0 = pl.multiple_of(g * grp, grp)
        for i in range(grp):
            accc = accumulate(j0 + i, jnp.exp2(s_ref[j0 + i] + (far_row(j0 + i) - m)), accc)
        return accc

    accs = lax.fori_loop(0, n_groups, exp_body, accs)
    out_t = jnp.concatenate([a[:ATT_HEAD_DIM] / a[ATT_HEAD_DIM:ATT_HEAD_DIM + 1] for a in accs], axis=0)
    o_ref[0] = out_t.T


def _moba_prompt(aq, ak, av, t0, t1, rel_bias, group):
    b, s, _ = aq.shape
    nb = s // MOBA_BLOCK
    nbp = -(-nb // 8) * 8
    n_pairs = N_ATT_HEADS // 2
    wide = 2 * MOBA_BLOCK
    assert nb % group == 0
    qspec = pl.BlockSpec((1, MOBA_BLOCK, LANES), lambda bi, hp, qi: (bi, qi, hp))
    kvspec = pl.BlockSpec((1, s, LANES), lambda bi, hp, qi: (bi, 0, hp))
    tspec = pl.BlockSpec((2, MOBA_BLOCK, MOBA_BLOCK), lambda bi, hp, qi: (hp, 0, 0))
    return pl.pallas_call(
        functools.partial(_moba_prompt_kernel, n_blocks=nb, group=group),
        grid=(b, n_pairs, nb),
        in_specs=[pl.BlockSpec(memory_space=pltpu.SMEM), qspec, kvspec, kvspec, tspec, tspec],
        out_specs=qspec,
        out_shape=jax.ShapeDtypeStruct((b, s, ATT_WIDTH), F32),
        scratch_shapes=[pltpu.VMEM((nbp, LANES), F32), pltpu.VMEM((nb, MOBA_BLOCK, LANES), BF16),
                        pltpu.VMEM((nb, 2, ATT_HEAD_DIM + SUM_ROWS, MOBA_BLOCK), BF16),
                        pltpu.VMEM((nbp, wide), F32), pltpu.VMEM((nbp, wide), F32),
                        pltpu.VMEM((nb, MOBA_BLOCK, wide), F32), pltpu.VMEM((2 * MOBA_BLOCK, wide), F32)],
        compiler_params=pltpu.CompilerParams(dimension_semantics=("parallel", "parallel", "arbitrary"),
                                             vmem_limit_bytes=VMEM_LIMIT),
        name="moba_prompt",
    )(rel_bias, aq, ak, av, t0, t1)


def _moba_sample_kernel(pt_ref, *refs, blocks_per_step, n_blocks, dec_seq):
    ppb = MOBA_BLOCK // PAGE_SIZE
    pps = blocks_per_step * ppb
    k_pages, v_pages = refs[:pps], refs[pps:2 * pps]
    (q_ref, kn_ref, vn_ref, s_own_ref, s_near_ref, s_far_ref, o_ref,
     qbd_ref, kmt_ref, mm_ref, ll_ref, oo_ref, own_m_ref, own_l_ref, own_o_ref) = refs[2 * pps:]
    c = pl.program_id(1)
    rows = N_ATT_HEADS * dec_seq
    row_head = lax.broadcasted_iota(jnp.int32, (rows, ATT_WIDTH), 0) // dec_seq
    lane_head = lax.broadcasted_iota(jnp.int32, (rows, ATT_WIDTH), 1) // ATT_HEAD_DIM
    lane_r = lax.broadcasted_iota(jnp.int32, (rows, LANES), 1)
    lane_k = lax.broadcasted_iota(jnp.int32, (ATT_WIDTH, LANES), 1)

    def local_softmax(s):
        m = jnp.max(s, axis=-1, keepdims=True)
        p = jnp.exp2(s - m)
        return m, jnp.sum(p, axis=-1, keepdims=True), p.astype(BF16)

    @pl.when(c == 0)
    def _():
        qrep = jnp.concatenate([q_ref[0]] * N_ATT_HEADS, axis=0)
        qbd = jnp.where(row_head == lane_head, qrep, 0.0)
        qbd_ref[...] = qbd
        kmt_ref[...] = jnp.zeros(kmt_ref.shape, F32)
        mm_ref[...] = jnp.zeros(mm_ref.shape, F32)
        ll_ref[...] = jnp.zeros(ll_ref.shape, F32)
        pad = jnp.zeros((LANES - dec_seq, ATT_WIDTH), F32)
        kn = jnp.concatenate([kn_ref[0], pad], axis=0).astype(BF16)
        vn = jnp.concatenate([vn_ref[0], pad], axis=0).astype(BF16)
        m, l, p = local_softmax(_nt_dot(qbd.astype(BF16), kn) + s_own_ref[...].reshape(rows, LANES))
        own_m_ref[...] = jnp.broadcast_to(m, (rows, LANES))
        own_l_ref[...] = jnp.broadcast_to(l, (rows, LANES))
        own_o_ref[...] = jnp.dot(p, vn, preferred_element_type=F32)

    qb = qbd_ref[...].astype(BF16)
    s_near = s_near_ref[...].reshape(rows, LANES)
    s_far = s_far_ref[...].reshape(rows, LANES)
    n0 = c * blocks_per_step
    kt = jnp.concatenate([k_pages[i][0] for i in range(pps)], axis=1)
    s_all = jnp.dot(qb, kt.astype(BF16), preferred_element_type=F32)
    mm, ll, kmt = mm_ref[...], ll_ref[...], kmt_ref[...]
    for g in range(blocks_per_step):
        n = n0 + g
        bias = jnp.concatenate([s_far] * (ppb - 1) + [jnp.where(n == n_blocks - 1, s_near, s_far)], axis=1)
        m, l, p = local_softmax(s_all[:, g * MOBA_BLOCK:(g + 1) * MOBA_BLOCK] + bias)
        vt = jnp.concatenate([v_pages[ppb * g + i][0] for i in range(ppb)], axis=1).astype(BF16)
        oo_ref[n] = _nt_dot(p, vt)
        mm = jnp.where(lane_r == n, m, mm)
        ll = jnp.where(lane_r == n, l, ll)
        ksum = jnp.sum(kt[:, g * MOBA_BLOCK:(g + 1) * MOBA_BLOCK], axis=-1, keepdims=True)
        kmt = jnp.where(lane_k == n, ksum, kmt)
    mm_ref[...] = mm
    ll_ref[...] = ll
    kmt_ref[...] = kmt

    @pl.when(c == pl.num_programs(1) - 1)
    def _():
        kmean_t = kmt_ref[...] * (1.0 / MOBA_BLOCK)
        pen = _topk_penalty(_dot_x3(qbd_ref[...], kmean_t, ((1,), (0,))), n_blocks)
        mp = mm_ref[...] + pen
        m_own = own_m_ref[...]
        m_tot = jnp.maximum(jnp.max(mp, axis=-1, keepdims=True), m_own)
        w = jnp.exp2(mp - m_tot)
        w_own = jnp.exp2(m_own - m_tot)
        den = jnp.sum(w * ll_ref[...], axis=-1, keepdims=True) + w_own * own_l_ref[...]

        num = w_own[:, 0:1] * own_o_ref[...]
        for nn in range(n_blocks):
            num = num + w[:, nn:nn + 1] * oo_ref[nn]
        full = jnp.where(row_head == lane_head, num / den[:, 0:1], 0.0)
        out = full[0:dec_seq]
        for h in range(1, N_ATT_HEADS):
            out = out + full[h * dec_seq:(h + 1) * dec_seq]
        o_ref[0] = out


def _moba_sample(aq, ak, av, pool_kt, pool_vt, page_table, s_own, s_near, s_far, blocks_per_step):
    db, dec_seq, _ = aq.shape
    n_pages = page_table.shape[1]
    ppb = MOBA_BLOCK // PAGE_SIZE
    nb = n_pages // ppb
    pps = blocks_per_step * ppb
    assert nb % blocks_per_step == 0 and nb <= LANES
    rows = N_ATT_HEADS * dec_seq

    def page_spec(i):
        return pl.BlockSpec((1, ATT_WIDTH, PAGE_SIZE), lambda b, c, pt: (pt[b, c * pps + i], 0, 0))

    per_b = pl.BlockSpec((1, dec_seq, ATT_WIDTH), lambda b, c, pt: (b, 0, 0))
    tile = pl.BlockSpec((N_ATT_HEADS, dec_seq, LANES), lambda b, c, pt: (0, 0, 0))
    grid_spec = pltpu.PrefetchScalarGridSpec(
        num_scalar_prefetch=1, grid=(db, nb // blocks_per_step),
        in_specs=[page_spec(i) for i in range(pps)] * 2 + [per_b, per_b, per_b, tile, tile, tile],
        out_specs=per_b,
        scratch_shapes=[pltpu.VMEM((rows, ATT_WIDTH), F32), pltpu.VMEM((ATT_WIDTH, LANES), F32),
                        pltpu.VMEM((rows, LANES), F32), pltpu.VMEM((rows, LANES), F32),
                        pltpu.VMEM((nb, rows, ATT_WIDTH), F32), pltpu.VMEM((rows, LANES), F32),
                        pltpu.VMEM((rows, LANES), F32), pltpu.VMEM((rows, ATT_WIDTH), F32)])
    return pl.pallas_call(
        functools.partial(_moba_sample_kernel, blocks_per_step=blocks_per_step, n_blocks=nb, dec_seq=dec_seq),
        grid_spec=grid_spec,
        out_shape=jax.ShapeDtypeStruct((db, dec_seq, ATT_WIDTH), F32),
        compiler_params=pltpu.CompilerParams(dimension_semantics=("parallel", "arbitrary"),
                                             vmem_limit_bytes=VMEM_LIMIT),
        name="moba_sample",
    )(page_table, *([pool_kt] * pps), *([pool_vt] * pps), aq, ak, av, s_own, s_near, s_far)


def _mlstm_kernel(q_ref, k_ref, v_ref, g_ref, mo_ref, nw_ref, c0_ref, n0_ref, m0_ref,
                  h_ref, c_out, n_out, m_out, c_s, n_s, m_s, *, rows, chunk):
    ci = pl.program_id(1)
    L = chunk

    @pl.when(ci == 0)
    def _():
        c_s[...] = c0_ref[0]
        n_s[...] = n0_ref[0]
        m_s[...] = m0_ref[0]

    def padded(ref, fill=None):
        x = ref[0]
        if rows == L:
            return x
        tail = jnp.zeros((L - rows, x.shape[1]), F32) if fill is None else fill
        return jnp.concatenate([x, tail], axis=0)

    lane = lax.broadcasted_iota(jnp.int32, (L, LANES), 1)
    pad_lane = lax.broadcasted_iota(jnp.int32, (max(L - rows, 1), LANES), 1)
    g = padded(g_ref, jnp.where(pad_lane < N_ML_HEADS, NEG, 0.0))
    q_all, k_all, v_all, mo_all = padded(q_ref), padded(k_ref), padded(v_ref), padded(mo_ref)

    is_lf = (lane >= N_ML_HEADS) & (lane < 2 * N_ML_HEADS)
    tri_r = lax.broadcasted_iota(jnp.int32, (L, L), 0)
    tri_c = lax.broadcasted_iota(jnp.int32, (L, L), 1)
    causal = tri_c <= tri_r
    g3 = jnp.concatenate(_split_bf16(jnp.where(is_lf, g, 0.0), 3), axis=1)
    b3 = jnp.dot(causal.astype(BF16), g3, preferred_element_type=F32)
    bcum = b3[:, :LANES] + (b3[:, LANES:2 * LANES] + b3[:, 2 * LANES:])
    g_t = g.T
    b_t = bcum.T

    ones_b = jnp.ones((L, LANES), BF16)
    reps = L // LANES
    wide = lambda x: x if reps == 1 else jnp.concatenate([x] * reps, axis=1)
    for h in range(N_ML_HEADS):
        sl = slice(h * ML_HEAD_DIM, (h + 1) * ML_HEAD_DIM)
        q, k, v = q_all[:, sl], k_all[:, sl], v_all[:, sl]
        ig = jnp.broadcast_to(g[:, h:h + 1], (L, LANES))
        b = jnp.broadcast_to(bcum[:, N_ML_HEADS + h:N_ML_HEADS + h + 1], (L, LANES))
        ig_row = g_t[h:h + 1, :]
        b_row = b_t[N_ML_HEADS + h:N_ML_HEADS + h + 1, :]
        m_prev = m_s[h]
        C = c_s[h]
        n_rep = n_s[h]

        a = b + m_prev
        log_d = jnp.where(causal, wide(b) - b_row + ig_row, NEG)
        m_t = jnp.maximum(a, jnp.max(log_d, axis=-1, keepdims=True))
        d = jnp.exp(log_d - wide(m_t))
        w_state = jnp.exp(a - m_t)
        qb, kb, vb = q.astype(BF16), k.astype(BF16), v.astype(BF16)
        s = _nt_dot(qb, kb) * d
        v_ones = jnp.concatenate([vb, ones_b], axis=1)
        from_state = jnp.dot(qb, jnp.concatenate([C, n_rep], axis=1).astype(BF16), preferred_element_type=F32)
        from_chunk = jnp.dot(s.astype(BF16), v_ones, preferred_element_type=F32)
        num = w_state * from_state[:, :ML_HEAD_DIM] + from_chunk[:, :ML_HEAD_DIM]
        den = w_state * from_state[:, ML_HEAD_DIM:] + from_chunk[:, ML_HEAD_DIM:]
        hid = num / jnp.maximum(jnp.abs(den), jnp.exp(-m_t))

        hn = _rms(hid, nw_ref[:, sl])
        h_ref[0, :, sl] = (hn * jax.nn.sigmoid(mo_all[:, sl]))[:rows]

        b_last = b[L - 1:L, :]
        m_new = m_t[L - 1:L, :]
        g_state = jnp.exp(b_last + m_prev - m_new)
        kg = k * jnp.exp(b_last - b + ig - m_new)
        added = _tn_dot(kg.astype(BF16), v_ones)
        c_s[h] = g_state * C + added[:, :ML_HEAD_DIM]
        n_s[h] = g_state * n_rep + added[:, ML_HEAD_DIM:]
        m_s[h] = m_new

    @pl.when(ci == pl.num_programs(1) - 1)
    def _():
        c_out[0] = c_s[...]
        n_out[0] = n_s[...]
        m_out[0] = m_s[...]


def _mlstm(mq, mk, mv, gates, mo, ml_norm, c0, n0, m0, chunk):
    b, s, _ = mq.shape
    rows = min(s, chunk)
    n_chunks = s // rows
    tok = lambda w: pl.BlockSpec((1, rows, w), lambda bi, ci: (bi, ci, 0))
    st4 = lambda shape: pl.BlockSpec(shape, lambda bi, ci: (bi, 0, 0, 0))
    c_shape, m_shape = (1, N_ML_HEADS, ML_HEAD_DIM, ML_HEAD_DIM), (1, N_ML_HEADS, 1, LANES)
    n_shape = (1, N_ML_HEADS, ML_HEAD_DIM, LANES)
    return pl.pallas_call(
        functools.partial(_mlstm_kernel, rows=rows, chunk=chunk),
        grid=(b, n_chunks),
        in_specs=[tok(ML_WIDTH), tok(ML_WIDTH), tok(ML_WIDTH), tok(LANES), tok(ML_WIDTH),
                  pl.BlockSpec((1, ML_WIDTH), lambda bi, ci: (0, 0)),
                  st4(c_shape), st4(n_shape), st4(m_shape)],
        out_specs=[tok(ML_WIDTH), st4(c_shape), st4(n_shape), st4(m_shape)],
        out_shape=[jax.ShapeDtypeStruct((b, s, ML_WIDTH), F32),
                   jax.ShapeDtypeStruct((b,) + c_shape[1:], F32),
                   jax.ShapeDtypeStruct((b,) + n_shape[1:], F32),
                   jax.ShapeDtypeStruct((b,) + m_shape[1:], F32)],
        scratch_shapes=[pltpu.VMEM(c_shape[1:], F32), pltpu.VMEM(n_shape[1:], F32), pltpu.VMEM(m_shape[1:], F32)],
        compiler_params=pltpu.CompilerParams(dimension_semantics=("parallel", "arbitrary"),
                                             vmem_limit_bytes=VMEM_LIMIT),
        name="mlstm",
    )(mq, mk, mv, gates, mo, ml_norm, c0, n0, m0)


def _out_ffn_kernel(x_ref, att_ref, ml_ref, wo_ref, gf_ref, wu_ref, wd_ref, gl_ref, y_ref, *, ff_chunk):
    x1 = x_ref[...] \
        + jnp.dot(att_ref[...].astype(BF16), wo_ref[0:ATT_WIDTH, :], preferred_element_type=F32) \
        + jnp.dot(ml_ref[...].astype(BF16), wo_ref[ATT_WIDTH:, :], preferred_element_type=F32)
    xb = _rms(x1, gf_ref[...]).astype(BF16)
    ffn = None
    for c in range(D_FF // ff_chunk):
        sl = slice(c * ff_chunk, (c + 1) * ff_chunk)
        u = jnp.maximum(jnp.dot(xb, wu_ref[:, sl], preferred_element_type=F32), 0.0)
        d = jnp.dot((u * u).astype(BF16), wd_ref[sl, :], preferred_element_type=F32)
        ffn = d if ffn is None else ffn + d
    y_ref[...] = _rms(x1 + ffn, gl_ref[...])


def _out_ffn(x2, att, ml, w_out, norm_ffn, w_up, w_down, norm_final, tm, ff_chunk):
    n = x2.shape[0]
    const = lambda shape: pl.BlockSpec(shape, lambda i: (0, 0), pipeline_mode=pl.Buffered(1))
    row = lambda w: pl.BlockSpec((tm, w), lambda i: (i, 0))
    return pl.pallas_call(
        functools.partial(_out_ffn_kernel, ff_chunk=ff_chunk),
        grid=(n // tm,),
        in_specs=[row(D_MODEL), row(ATT_WIDTH), row(ML_WIDTH), const((D_MODEL, D_MODEL)), const((1, D_MODEL)),
                  const((D_MODEL, D_FF)), const((D_FF, D_MODEL)), const((1, D_MODEL))],
        out_specs=row(D_MODEL),
        out_shape=jax.ShapeDtypeStruct((n, D_MODEL), F32),
        compiler_params=pltpu.CompilerParams(dimension_semantics=("parallel",), vmem_limit_bytes=VMEM_LIMIT),
        name="out_proj_ffn",
    )(x2, att, ml, w_out, norm_ffn, w_up, w_down, norm_final)


def _layer_weights(l, norm_mix, w_in, b_ig, b_fg, ml_norm, w_out, norm_ffn, w_up, w_down):
    w_main = w_in[l, :, :GATE_COL0].astype(BF16)
    w_gate = jnp.pad(w_in[l, :, GATE_COL0:], ((0, 0), (0, LANES - 2 * N_ML_HEADS)))
    b_gate = jnp.pad(jnp.concatenate([b_ig[l], b_fg[l]]), (0, LANES - 2 * N_ML_HEADS))[None, :]
    return dict(norm_mix=norm_mix[l][None, :], w_main=w_main, w_gate=w_gate, b_gate=b_gate,
                ml_norm=ml_norm[l][None, :], w_out=w_out[l].astype(BF16), norm_ffn=norm_ffn[l][None, :],
                w_up=w_up[l].astype(BF16), w_down=w_down[l].astype(BF16))


def _trunk_layer(x, w, attend, state0, norm_final, tm, ml_chunk):
    b, s, _ = x.shape
    x2 = x.reshape(b * s, D_MODEL)
    aq, ak, av, mq, mk, mv, mo, gates, *kv_t = _project(x2, w["norm_mix"], w["w_main"], w["w_gate"], w["b_gate"], tm, s)
    r3 = lambda a: a.reshape(b, s, a.shape[-1])
    if kv_t:
        k_out, v_out = (a.reshape(b, N_ATT_HEADS, ATT_HEAD_DIM, s).transpose(0, 3, 1, 2) for a in kv_t)
    else:
        k_out, v_out = (a.reshape(b, s, N_ATT_HEADS, ATT_HEAD_DIM) for a in (ak, av))
    att = attend(r3(aq), r3(ak), r3(av))
    c0, n0, m0 = state0
    ml, c_new, n_new, m_new = _mlstm(r3(mq), r3(mk), r3(mv), r3(gates), r3(mo), w["ml_norm"], c0, n0, m0, ml_chunk)
    y = _out_ffn(x2, att.reshape(b * s, ATT_WIDTH), ml.reshape(b * s, ML_WIDTH), w["w_out"], w["norm_ffn"],
                 w["w_up"], w["w_down"], norm_final, tm, 1024)
    return y.reshape(b, s, D_MODEL), k_out, v_out, c_new, n_new[:, :, :, 0], m_new[:, :, 0, 0]


def kernel(x_prompt, x_sample, cache_k, cache_v, state_C, state_n, state_m, page_table, rel_bias, norm_mix, w_in,
           b_ig, b_fg, ml_norm, w_out, norm_ffn, w_up, w_down, norm_final):
    depth = w_in.shape[0]
    assert depth == 1, "the fused final RMSNorm assumes a single layer"
    bp, sp, _ = x_prompt.shape
    db, dec_seq, _ = x_sample.shape
    n_phys = cache_k.shape[1]
    t0, t1, s_own, s_near, s_far = _bias_tiles(rel_bias, dec_seq)
    gl = norm_final[None, :]
    l = 0
    w = _layer_weights(l, norm_mix, w_in, b_ig, b_fg, ml_norm, w_out, norm_ffn, w_up, w_down)

    zero_state = (jnp.zeros((bp, N_ML_HEADS, ML_HEAD_DIM, ML_HEAD_DIM), F32),
                  jnp.zeros((bp, N_ML_HEADS, ML_HEAD_DIM, LANES), F32), jnp.zeros((bp, N_ML_HEADS, 1, LANES), F32))
    yp, kp, vp, cp, np_, mp = _trunk_layer(
        x_prompt, w, lambda q, k, v: _moba_prompt(q, k, v, t0, t1, rel_bias, 4), zero_state, gl, 512, 512)

    to_pool = lambda a: jnp.transpose(a, (0, 1, 3, 4, 2)).reshape(depth * n_phys, ATT_WIDTH, PAGE_SIZE)
    pool_kt, pool_vt = to_pool(cache_k), to_pool(cache_v)
    pt = page_table + l * n_phys

    def attend_sample(q, k, v):
        return _moba_sample(q, k, v, pool_kt, pool_vt, pt, s_own, s_near, s_far, 16)

    state0 = (state_C[l], jnp.broadcast_to(state_n[l][:, :, :, None], (db, N_ML_HEADS, ML_HEAD_DIM, LANES)),
              jnp.broadcast_to(state_m[l][:, :, None, None], (db, N_ML_HEADS, 1, LANES)))
    ys, ks, vs, cs, ns, ms = _trunk_layer(x_sample, w, attend_sample, state0, gl, 256, 128)

    st = lambda a: a[None]
    return (yp, ys, st(kp), st(vp), st(cp), st(np_), st(mp), st(ks), st(vs), st(cs), st(ns), st(ms))
```

```python
import functools
import math
from typing import NamedTuple

import numpy as np
import jax
import jax.numpy as jnp
from jax import lax
from jax.experimental import pallas as pl
from jax.experimental.pallas import tpu as pltpu

F32 = jnp.float32
BF16 = jnp.bfloat16

D_MODEL = 1024
ATT_HEAD_DIM = 64
N_ATT_HEADS = 8
ATT_WIDTH = 512
N_ML_HEADS = 4
ML_HEAD_DIM = 128
ML_WIDTH = 512
SLAB = 512
N_SLABS = 7
GATE_COL0 = N_SLABS * SLAB
MOBA_BLOCK = 256
MOBA_TOPK = 3
PAGE_SIZE = 128
NUM_BUCKETS = 32
MAX_DISTANCE = 128
D_FF = 4 * D_MODEL
RMS_EPS = 1e-6
NEG = -1e30
LOG2E = math.log2(math.e)
LANES = 128
SUM_ROWS = 16
VMEM_LIMIT = 56 * 1024 * 1024


class _Tiles(NamedTuple):
    tm: int
    ml_chunk: int
    ff_chunk: int = 1024


PROMPT_TILES = _Tiles(tm=512, ml_chunk=512)
SAMPLE_TILES = _Tiles(tm=256, ml_chunk=128)
PROMPT_KEY_GROUP = 4
SAMPLE_BLOCKS_PER_STEP = 16


def _nt_dot(a, b):
    return lax.dot_general(a, b, (((1,), (1,)), ((), ())), preferred_element_type=F32)


def _split_bf16(x, parts):
    out = []
    for _ in range(parts):
        p = x.astype(BF16)
        out.append(p)
        x = x - p.astype(F32)
    return out


def _dot_x3(a, b, contract):
    (ah, al), (bh, bl) = _split_bf16(a, 2), _split_bf16(b, 2)
    d = lambda x, y: lax.dot_general(x, y, (contract, ((), ())), preferred_element_type=F32)
    return d(ah, bh) + (d(ah, bl) + d(al, bh))


def _tn_dot(a, b):
    return lax.dot_general(a, b, (((0,), (0,)), ((), ())), preferred_element_type=F32)


def _bucket_upper_bounds():
    n = np.arange(0, MAX_DISTANCE + 1)
    max_exact = NUM_BUCKETS // 2
    nf = np.maximum(n, 1).astype(np.float32)
    large = max_exact + (np.log(nf / np.float32(max_exact)) / np.float32(math.log(MAX_DISTANCE / max_exact))
                         * np.float32(NUM_BUCKETS - max_exact)).astype(np.int32)
    large = np.minimum(large, NUM_BUCKETS - 1)
    bucket = np.where(n < max_exact, n, large)
    return np.array([int(n[bucket <= b].max()) for b in range(NUM_BUCKETS - 1)], dtype=np.int32)


def _bias_of_dist(dist, h, rb_ref, ub_ref):
    def body(i, acc):
        b = NUM_BUCKETS - 2 - i
        return jnp.where(dist <= ub_ref[b], rb_ref[b, h], acc)
    init = jnp.full(dist.shape, rb_ref[NUM_BUCKETS - 1, h], F32)
    return lax.fori_loop(0, NUM_BUCKETS - 1, body, init)


def _bias_tiles_kernel(rb_ref, ub_ref, t0_ref, t1_ref, s_own_ref, s_near_ref, s_far_ref, *, dec_seq):
    h = pl.program_id(0)
    kk = lax.broadcasted_iota(jnp.int32, (MOBA_BLOCK, MOBA_BLOCK), 0)
    qq = lax.broadcasted_iota(jnp.int32, (MOBA_BLOCK, MOBA_BLOCK), 1)
    d0 = qq - kk
    t0_ref[0] = jnp.where(d0 >= 0, _bias_of_dist(jnp.maximum(d0, 0), h, rb_ref, ub_ref) * LOG2E, NEG)
    t1_ref[0] = _bias_of_dist(d0 + MOBA_BLOCK, h, rb_ref, ub_ref) * LOG2E
    t = lax.broadcasted_iota(jnp.int32, (dec_seq, LANES), 0)
    cc = lax.broadcasted_iota(jnp.int32, (dec_seq, LANES), 1)
    ds = t - cc
    s_own_ref[0] = jnp.where((ds >= 0) & (cc < dec_seq),
                             _bias_of_dist(jnp.maximum(ds, 0), h, rb_ref, ub_ref) * LOG2E, NEG)
    s_near_ref[0] = _bias_of_dist(ds + PAGE_SIZE, h, rb_ref, ub_ref) * LOG2E
    s_far_ref[0] = jnp.full((dec_seq, LANES), rb_ref[NUM_BUCKETS - 1, h] * LOG2E, F32)


def _bias_tiles(rel_bias, dec_seq):
    ub = jnp.asarray(_bucket_upper_bounds())
    smem = pl.BlockSpec(memory_space=pltpu.SMEM)
    big = pl.BlockSpec((1, MOBA_BLOCK, MOBA_BLOCK), lambda h: (h, 0, 0))
    small = pl.BlockSpec((1, dec_seq, LANES), lambda h: (h, 0, 0))
    return pl.pallas_call(
        functools.partial(_bias_tiles_kernel, dec_seq=dec_seq),
        grid=(N_ATT_HEADS,),
        in_specs=[smem, smem],
        out_specs=[big, big, small, small, small],
        out_shape=[jax.ShapeDtypeStruct((N_ATT_HEADS, MOBA_BLOCK, MOBA_BLOCK), F32)] * 2
        + [jax.ShapeDtypeStruct((N_ATT_HEADS, dec_seq, LANES), F32)] * 3,
        name="bias_tiles",
    )(rel_bias, ub)


def _rms(x, g):
    return x * lax.rsqrt(jnp.mean(x * x, axis=-1, keepdims=True) + RMS_EPS) * g


def _log_sigmoid(z):
    return jnp.minimum(z, 0.0) - jnp.log1p(jnp.exp(-jnp.abs(z)))


def _proj_kernel(x_ref, g_ref, w_ref, bg_ref, aq_ref, ak_ref, av_ref, mq_ref, mk_ref, mv_ref, mo_ref,
                 gate_ref, *kv_t_refs):
    xn = _rms(x_ref[...], g_ref[...])
    xb = xn.astype(BF16)
    outs = (aq_ref, ak_ref, av_ref, mq_ref, mk_ref, mv_ref, mo_ref)
    scales = (ATT_HEAD_DIM ** -0.5 * LOG2E, None, None, None, ML_HEAD_DIM ** -0.5, None, None)
    for i, (o_ref, s) in enumerate(zip(outs, scales)):
        r = jnp.dot(xb, w_ref[:, i * SLAB:(i + 1) * SLAB], preferred_element_type=F32)
        o_ref[...] = r if s is None else r * s
        if kv_t_refs and i in (1, 2):
            kv_t_refs[i - 1][0] = r.T
    gate_hi_lo = jnp.dot(xb, w_ref[:, GATE_COL0:GATE_COL0 + 2 * LANES], preferred_element_type=F32)
    x_lo = (xn - xb.astype(F32)).astype(BF16)
    z = gate_hi_lo[:, :LANES] + (gate_hi_lo[:, LANES:]
                                 + jnp.dot(x_lo, w_ref[:, GATE_COL0:GATE_COL0 + LANES], preferred_element_type=F32))
    z = z + bg_ref[...]
    lane = lax.broadcasted_iota(jnp.int32, z.shape, 1)
    gate_ref[...] = jnp.where((lane >= N_ML_HEADS) & (lane < 2 * N_ML_HEADS), _log_sigmoid(z), z)


def _project(x2, norm_g, w_all, b_gate, tm, seq):
    n = x2.shape[0]
    const = lambda shape: pl.BlockSpec(shape, lambda i: (0, 0), pipeline_mode=pl.Buffered(1))
    slab = pl.BlockSpec((tm, SLAB), lambda i: (i, 0))
    out_specs = [slab] * N_SLABS + [pl.BlockSpec((tm, LANES), lambda i: (i, 0))]
    out_shape = [jax.ShapeDtypeStruct((n, SLAB), F32)] * N_SLABS + [jax.ShapeDtypeStruct((n, LANES), F32)]
    if seq % tm == 0:
        tiles = seq // tm
        out_specs += [pl.BlockSpec((1, SLAB, tm), lambda i: (i // tiles, 0, i % tiles))] * 2
        out_shape += [jax.ShapeDtypeStruct((n // seq, SLAB, seq), F32)] * 2
    return pl.pallas_call(
        _proj_kernel,
        grid=(n // tm,),
        in_specs=[pl.BlockSpec((tm, D_MODEL), lambda i: (i, 0)), const((1, D_MODEL)),
                  const((D_MODEL, N_SLABS * SLAB + 2 * LANES)), const((1, LANES))],
        out_specs=out_specs,
        out_shape=out_shape,
        compiler_params=pltpu.CompilerParams(dimension_semantics=("parallel",), vmem_limit_bytes=VMEM_LIMIT),
        name="rms_in_proj",
    )(x2, norm_g, w_all, b_gate)


def _topk_penalty(scores, n_valid):
    lane = lax.broadcasted_iota(jnp.int32, scores.shape, 1)
    rank = jnp.zeros(scores.shape, jnp.int32)
    for n in range(n_valid):
        col = scores[:, n:n + 1]
        rank = rank + jnp.where((col > scores) | ((col == scores) & (lane > n)), 1, 0)
    return jnp.where((rank < MOBA_TOPK) & (lane < n_valid), 0.0, NEG)


def _moba_prompt_kernel(rb_ref, q_ref, k_ref, v_ref, t0_ref, t1_ref, o_ref, km_ref, kb_ref, vt_ref, sc_ref,
                        pen_ref, s_ref, sp_ref, *, n_blocks, group):
    hp = pl.program_id(1)
    own = pl.program_id(2)
    blk = MOBA_BLOCK
    nbp = km_ref.shape[0]

    @pl.when(own == 0)
    def _():
        km_ref[...] = jnp.zeros(km_ref.shape, F32)
        for n in range(n_blocks):
            kn = k_ref[0, n * blk:(n + 1) * blk, :]
            km_ref[n:n + 1, :] = jnp.mean(kn, axis=0, keepdims=True)
            kb_ref[n] = kn.astype(BF16)
            vtn = v_ref[0, n * blk:(n + 1) * blk, :].T.astype(BF16)
            for hh in range(2):
                vt_ref[n, hh, 0:ATT_HEAD_DIM, :] = vtn[hh * ATT_HEAD_DIM:(hh + 1) * ATT_HEAD_DIM]
                vt_ref[n, hh, ATT_HEAD_DIM:, :] = jnp.ones((SUM_ROWS, blk), BF16)

    q2 = q_ref[0]
    lane = lax.broadcasted_iota(jnp.int32, q2.shape, 1)
    q_hi, q_lo = _split_bf16(q2, 2)
    qcat_b = jnp.concatenate([jnp.where((lane // ATT_HEAD_DIM) == hh, q_hi, jnp.zeros_like(q_hi))
                              for hh in range(2)], axis=0)
    wide = 2 * blk
    lane_m = lax.broadcasted_iota(jnp.int32, (nbp, LANES), 1)
    km2 = jnp.concatenate([jnp.where((lane_m // ATT_HEAD_DIM) == hh, km_ref[...], 0.0) for hh in range(2)], axis=0)
    km_hi, km_lo = _split_bf16(km2, 2)
    hi_q = _nt_dot(jnp.concatenate([km_hi, km_lo], axis=0), q_hi)
    sc2 = hi_q[:2 * nbp] + (hi_q[2 * nbp:] + _nt_dot(km_hi, q_lo))
    sc_ref[...] = jnp.concatenate([sc2[:nbp], sc2[nbp:]], axis=1)

    sub = lax.broadcasted_iota(jnp.int32, (nbp, wide), 0)

    def rank_body(n, rank):
        sc = sc_ref[...]
        row = sc_ref[pl.ds(n, 1), :]
        return rank + jnp.where((row > sc) | ((row == sc) & (sub > n)), 1, 0)

    rank = lax.fori_loop(0, own, rank_body, jnp.zeros((nbp, wide), jnp.int32))
    pen_ref[...] = jnp.where((rank < MOBA_TOPK) & (sub < own), 0.0, NEG)

    col_head = lax.broadcasted_iota(jnp.int32, (1, wide), 1) // blk
    far_bias = jnp.where(col_head == 0, rb_ref[NUM_BUCKETS - 1, 2 * hp], rb_ref[NUM_BUCKETS - 1, 2 * hp + 1]) * LOG2E

    def fold8(x, op):
        return op(x.reshape(blk // 8, 8, x.shape[1]), axis=0)

    def far_row(j):
        return jnp.where(j < own - 1, pen_ref[pl.ds(j, 1), :] + far_bias, NEG)

    grp = group
    n_groups = (jnp.maximum(own - 1, 0) + grp - 1) // grp
    jn = jnp.maximum(own - 1, 0)
    near_row = jnp.where(own >= 1, pen_ref[pl.ds(jn, 1), :], NEG)

    t0c = jnp.concatenate([t0_ref[0], t0_ref[1]], axis=1)
    t1c = jnp.concatenate([t1_ref[0], t1_ref[1]], axis=1)
    no_prev = own == 0
    s_pair = _nt_dot(kb_ref[pl.ds(jn, 2)].reshape(2 * blk, LANES), qcat_b)
    s_first = s_pair[:blk] + jnp.where(no_prev, t0c, t1c + near_row)
    s_second = jnp.where(no_prev, NEG, s_pair[blk:] + t0c)
    sp_ref[0:blk, :] = s_first
    sp_ref[blk:, :] = s_second
    m8 = jnp.maximum(fold8(s_first, jnp.max), fold8(s_second, jnp.max))

    def max_body(g, m8c):
        j0 = pl.multiple_of(g * grp, grp)
        sg = _nt_dot(kb_ref[pl.ds(j0, grp)].reshape(grp * blk, LANES), qcat_b)
        s_ref[pl.ds(j0, grp)] = sg.reshape(grp, blk, wide)
        for i in range(grp):
            m8c = jnp.maximum(m8c, fold8(sg[i * blk:(i + 1) * blk], jnp.max) + far_row(j0 + i))
        return m8c

    m8 = lax.fori_loop(0, n_groups, max_body, m8)
    m = jnp.max(m8, axis=0, keepdims=True)

    def accumulate(j, p, accs):
        pb = p.astype(BF16)
        return tuple(accs[hh] + jnp.dot(vt_ref[j, hh], pb[:, hh * blk:(hh + 1) * blk], preferred_element_type=F32)
                     for hh in range(2))

    pb_pair = jnp.exp2(sp_ref[...] - m).astype(BF16)
    vt_pair = vt_ref[pl.ds(jn, 2)]
    accs = tuple(jnp.dot(jnp.concatenate([vt_pair[i, hh] for i in range(2)], axis=1),
                         pb_pair[:, hh * blk:(hh + 1) * blk], preferred_element_type=F32) for hh in range(2))

    def exp_body(g, accc):
        j0 = pl.multiple_of(g * grp, grp)
        for i in range(grp):
            accc = accumulate(j0 + i, jnp.exp2(s_ref[j0 + i] + (far_row(j0 + i) - m)), accc)
        return accc

    accs = lax.fori_loop(0, n_groups, exp_body, accs)
    out_t = jnp.concatenate([a[:ATT_HEAD_DIM] / a[ATT_HEAD_DIM:ATT_HEAD_DIM + 1] for a in accs], axis=0)
    o_ref[0] = out_t.T


def _moba_prompt(aq, ak, av, t0, t1, rel_bias, group):
    b, s, _ = aq.shape
    nb = s // MOBA_BLOCK
    nbp = -(-nb // 8) * 8
    n_pairs = N_ATT_HEADS // 2
    wide = 2 * MOBA_BLOCK
    assert nb % group == 0
    qspec = pl.BlockSpec((1, MOBA_BLOCK, LANES), lambda bi, hp, qi: (bi, qi, hp))
    kvspec = pl.BlockSpec((1, s, LANES), lambda bi, hp, qi: (bi, 0, hp))
    tspec = pl.BlockSpec((2, MOBA_BLOCK, MOBA_BLOCK), lambda bi, hp, qi: (hp, 0, 0))
    return pl.pallas_call(
        functools.partial(_moba_prompt_kernel, n_blocks=nb, group=group),
        grid=(b, n_pairs, nb),
        in_specs=[pl.BlockSpec(memory_space=pltpu.SMEM), qspec, kvspec, kvspec, tspec, tspec],
        out_specs=qspec,
        out_shape=jax.ShapeDtypeStruct((b, s, ATT_WIDTH), F32),
        scratch_shapes=[pltpu.VMEM((nbp, LANES), F32), pltpu.VMEM((nb, MOBA_BLOCK, LANES), BF16),
                        pltpu.VMEM((nb, 2, ATT_HEAD_DIM + SUM_ROWS, MOBA_BLOCK), BF16),
                        pltpu.VMEM((nbp, wide), F32), pltpu.VMEM((nbp, wide), F32),
                        pltpu.VMEM((nb, MOBA_BLOCK, wide), F32), pltpu.VMEM((2 * MOBA_BLOCK, wide), F32)],
        compiler_params=pltpu.CompilerParams(dimension_semantics=("parallel", "parallel", "arbitrary"),
                                             vmem_limit_bytes=VMEM_LIMIT),
        name="moba_prompt",
    )(rel_bias, aq, ak, av, t0, t1)


def _moba_sample_kernel(pt_ref, *refs, blocks_per_step, n_blocks, dec_seq):
    ppb = MOBA_BLOCK // PAGE_SIZE
    pps = blocks_per_step * ppb
    k_pages, v_pages = refs[:pps], refs[pps:2 * pps]
    (q_ref, kn_ref, vn_ref, s_own_ref, s_near_ref, s_far_ref, o_ref,
     qbd_ref, kmt_ref, mm_ref, ll_ref, oo_ref, own_m_ref, own_l_ref, own_o_ref) = refs[2 * pps:]
    c = pl.program_id(1)
    rows = N_ATT_HEADS * dec_seq
    row_head = lax.broadcasted_iota(jnp.int32, (rows, ATT_WIDTH), 0) // dec_seq
    lane_head = lax.broadcasted_iota(jnp.int32, (rows, ATT_WIDTH), 1) // ATT_HEAD_DIM
    lane_r = lax.broadcasted_iota(jnp.int32, (rows, LANES), 1)
    lane_k = lax.broadcasted_iota(jnp.int32, (ATT_WIDTH, LANES), 1)

    def local_softmax(s):
        m = jnp.max(s, axis=-1, keepdims=True)
        p = jnp.exp2(s - m)
        return m, jnp.sum(p, axis=-1, keepdims=True), p.astype(BF16)

    @pl.when(c == 0)
    def _():
        qrep = jnp.concatenate([q_ref[0]] * N_ATT_HEADS, axis=0)
        qbd = jnp.where(row_head == lane_head, qrep, 0.0)
        qbd_ref[...] = qbd
        kmt_ref[...] = jnp.zeros(kmt_ref.shape, F32)
        mm_ref[...] = jnp.zeros(mm_ref.shape, F32)
        ll_ref[...] = jnp.zeros(ll_ref.shape, F32)
        pad = jnp.zeros((LANES - dec_seq, ATT_WIDTH), F32)
        kn = jnp.concatenate([kn_ref[0], pad], axis=0).astype(BF16)
        vn = jnp.concatenate([vn_ref[0], pad], axis=0).astype(BF16)
        m, l, p = local_softmax(_nt_dot(qbd.astype(BF16), kn) + s_own_ref[...].reshape(rows, LANES))
        own_m_ref[...] = jnp.broadcast_to(m, (rows, LANES))
        own_l_ref[...] = jnp.broadcast_to(l, (rows, LANES))
        own_o_ref[...] = jnp.dot(p, vn, preferred_element_type=F32)

    qb = qbd_ref[...].astype(BF16)
    s_near = s_near_ref[...].reshape(rows, LANES)
    s_far = s_far_ref[...].reshape(rows, LANES)
    n0 = c * blocks_per_step
    kt = jnp.concatenate([k_pages[i][0] for i in range(pps)], axis=1)
    s_all = jnp.dot(qb, kt.astype(BF16), preferred_element_type=F32)
    mm, ll, kmt = mm_ref[...], ll_ref[...], kmt_ref[...]
    for g in range(blocks_per_step):
        n = n0 + g
        bias = jnp.concatenate([s_far] * (ppb - 1) + [jnp.where(n == n_blocks - 1, s_near, s_far)], axis=1)
        m, l, p = local_softmax(s_all[:, g * MOBA_BLOCK:(g + 1) * MOBA_BLOCK] + bias)
        vt = jnp.concatenate([v_pages[ppb * g + i][0] for i in range(ppb)], axis=1).astype(BF16)
        oo_ref[n] = _nt_dot(p, vt)
        mm = jnp.where(lane_r == n, m, mm)
        ll = jnp.where(lane_r == n, l, ll)
        ksum = jnp.sum(kt[:, g * MOBA_BLOCK:(g + 1) * MOBA_BLOCK], axis=-1, keepdims=True)
        kmt = jnp.where(lane_k == n, ksum, kmt)
    mm_ref[...] = mm
    ll_ref[...] = ll
    kmt_ref[...] = kmt

    @pl.when(c == pl.num_programs(1) - 1)
    def _():
        kmean_t = kmt_ref[...] * (1.0 / MOBA_BLOCK)
        pen = _topk_penalty(_dot_x3(qbd_ref[...], kmean_t, ((1,), (0,))), n_blocks)
        mp = mm_ref[...] + pen
        m_own = own_m_ref[...]
        m_tot = jnp.maximum(jnp.max(mp, axis=-1, keepdims=True), m_own)
        w = jnp.exp2(mp - m_tot)
        w_own = jnp.exp2(m_own - m_tot)
        den = jnp.sum(w * ll_ref[...], axis=-1, keepdims=True) + w_own * own_l_ref[...]

        num = w_own[:, 0:1] * own_o_ref[...]
        for nn in range(n_blocks):
            num = num + w[:, nn:nn + 1] * oo_ref[nn]
        full = jnp.where(row_head == lane_head, num / den[:, 0:1], 0.0)
        out = full[0:dec_seq]
        for h in range(1, N_ATT_HEADS):
            out = out + full[h * dec_seq:(h + 1) * dec_seq]
        o_ref[0] = out


def _moba_sample(aq, ak, av, pool_kt, pool_vt, page_table, s_own, s_near, s_far, blocks_per_step):
    db, dec_seq, _ = aq.shape
    n_pages = page_table.shape[1]
    ppb = MOBA_BLOCK // PAGE_SIZE
    nb = n_pages // ppb
    pps = blocks_per_step * ppb
    assert nb % blocks_per_step == 0 and nb <= LANES
    rows = N_ATT_HEADS * dec_seq

    def page_spec(i):
        return pl.BlockSpec((1, ATT_WIDTH, PAGE_SIZE), lambda b, c, pt: (pt[b, c * pps + i], 0, 0))

    per_b = pl.BlockSpec((1, dec_seq, ATT_WIDTH), lambda b, c, pt: (b, 0, 0))
    tile = pl.BlockSpec((N_ATT_HEADS, dec_seq, LANES), lambda b, c, pt: (0, 0, 0))
    grid_spec = pltpu.PrefetchScalarGridSpec(
        num_scalar_prefetch=1, grid=(db, nb // blocks_per_step),
        in_specs=[page_spec(i) for i in range(pps)] * 2 + [per_b, per_b, per_b, tile, tile, tile],
        out_specs=per_b,
        scratch_shapes=[pltpu.VMEM((rows, ATT_WIDTH), F32), pltpu.VMEM((ATT_WIDTH, LANES), F32),
                        pltpu.VMEM((rows, LANES), F32), pltpu.VMEM((rows, LANES), F32),
                        pltpu.VMEM((nb, rows, ATT_WIDTH), F32), pltpu.VMEM((rows, LANES), F32),
                        pltpu.VMEM((rows, LANES), F32), pltpu.VMEM((rows, ATT_WIDTH), F32)])
    return pl.pallas_call(
        functools.partial(_moba_sample_kernel, blocks_per_step=blocks_per_step, n_blocks=nb, dec_seq=dec_seq),
        grid_spec=grid_spec,
        out_shape=jax.ShapeDtypeStruct((db, dec_seq, ATT_WIDTH), F32),
        compiler_params=pltpu.CompilerParams(dimension_semantics=("parallel", "arbitrary"),
                                             vmem_limit_bytes=VMEM_LIMIT),
        name="moba_sample",
    )(page_table, *([pool_kt] * pps), *([pool_vt] * pps), aq, ak, av, s_own, s_near, s_far)


def _mlstm_kernel(q_ref, k_ref, v_ref, g_ref, mo_ref, nw_ref, c0_ref, n0_ref, m0_ref,
                  h_ref, c_out, n_out, m_out, c_s, n_s, m_s, *, rows, chunk):
    ci = pl.program_id(1)
    L = chunk

    @pl.when(ci == 0)
    def _():
        c_s[...] = c0_ref[0]
        n_s[...] = n0_ref[0]
        m_s[...] = m0_ref[0]

    def padded(ref, fill=None):
        x = ref[0]
        if rows == L:
            return x
        tail = jnp.zeros((L - rows, x.shape[1]), F32) if fill is None else fill
        return jnp.concatenate([x, tail], axis=0)

    lane = lax.broadcasted_iota(jnp.int32, (L, LANES), 1)
    pad_lane = lax.broadcasted_iota(jnp.int32, (max(L - rows, 1), LANES), 1)
    g = padded(g_ref, jnp.where(pad_lane < N_ML_HEADS, NEG, 0.0))
    q_all, k_all, v_all, mo_all = padded(q_ref), padded(k_ref), padded(v_ref), padded(mo_ref)

    is_lf = (lane >= N_ML_HEADS) & (lane < 2 * N_ML_HEADS)
    tri_r = lax.broadcasted_iota(jnp.int32, (L, L), 0)
    tri_c = lax.broadcasted_iota(jnp.int32, (L, L), 1)
    causal = tri_c <= tri_r
    g3 = jnp.concatenate(_split_bf16(jnp.where(is_lf, g, 0.0), 3), axis=1)
    b3 = jnp.dot(causal.astype(BF16), g3, preferred_element_type=F32)
    bcum = b3[:, :LANES] + (b3[:, LANES:2 * LANES] + b3[:, 2 * LANES:])
    g_t = g.T
    b_t = bcum.T

    ones_b = jnp.ones((L, LANES), BF16)
    reps = L // LANES
    wide = lambda x: x if reps == 1 else jnp.concatenate([x] * reps, axis=1)
    for h in range(N_ML_HEADS):
        sl = slice(h * ML_HEAD_DIM, (h + 1) * ML_HEAD_DIM)
        q, k, v = q_all[:, sl], k_all[:, sl], v_all[:, sl]
        ig = jnp.broadcast_to(g[:, h:h + 1], (L, LANES))
        b = jnp.broadcast_to(bcum[:, N_ML_HEADS + h:N_ML_HEADS + h + 1], (L, LANES))
        ig_row = g_t[h:h + 1, :]
        b_row = b_t[N_ML_HEADS + h:N_ML_HEADS + h + 1, :]
        m_prev = m_s[h]
        C = c_s[h]
        n_rep = n_s[h]

        a = b + m_prev
        log_d = jnp.where(causal, wide(b) - b_row + ig_row, NEG)
        m_t = jnp.maximum(a, jnp.max(log_d, axis=-1, keepdims=True))
        d = jnp.exp(log_d - wide(m_t))
        w_state = jnp.exp(a - m_t)
        qb, kb, vb = q.astype(BF16), k.astype(BF16), v.astype(BF16)
        s = _nt_dot(qb, kb) * d
        v_ones = jnp.concatenate([vb, ones_b], axis=1)
        from_state = jnp.dot(qb, jnp.concatenate([C, n_rep], axis=1).astype(BF16), preferred_element_type=F32)
        from_chunk = jnp.dot(s.astype(BF16), v_ones, preferred_element_type=F32)
        num = w_state * from_state[:, :ML_HEAD_DIM] + from_chunk[:, :ML_HEAD_DIM]
        den = w_state * from_state[:, ML_HEAD_DIM:] + from_chunk[:, ML_HEAD_DIM:]
        hid = num / jnp.maximum(jnp.abs(den), jnp.exp(-m_t))

        hn = _rms(hid, nw_ref[:, sl])
        h_ref[0, :, sl] = (hn * jax.nn.sigmoid(mo_all[:, sl]))[:rows]

        b_last = b[L - 1:L, :]
        m_new = m_t[L - 1:L, :]
        g_state = jnp.exp(b_last + m_prev - m_new)
        kg = k * jnp.exp(b_last - b + ig - m_new)
        added = _tn_dot(kg.astype(BF16), v_ones)
        c_s[h] = g_state * C + added[:, :ML_HEAD_DIM]
        n_s[h] = g_state * n_rep + added[:, ML_HEAD_DIM:]
        m_s[h] = m_new

    @pl.when(ci == pl.num_programs(1) - 1)
    def _():
        c_out[0] = c_s[...]
        n_out[0] = n_s[...]
        m_out[0] = m_s[...]


def _mlstm(mq, mk, mv, gates, mo, ml_norm, c0, n0, m0, chunk):
    b, s, _ = mq.shape
    rows = min(s, chunk)
    n_chunks = s // rows
    tok = lambda w: pl.BlockSpec((1, rows, w), lambda bi, ci: (bi, ci, 0))
    st4 = lambda shape: pl.BlockSpec(shape, lambda bi, ci: (bi, 0, 0, 0))
    c_shape, m_shape = (1, N_ML_HEADS, ML_HEAD_DIM, ML_HEAD_DIM), (1, N_ML_HEADS, 1, LANES)
    n_shape = (1, N_ML_HEADS, ML_HEAD_DIM, LANES)
    return pl.pallas_call(
        functools.partial(_mlstm_kernel, rows=rows, chunk=chunk),
        grid=(b, n_chunks),
        in_specs=[tok(ML_WIDTH), tok(ML_WIDTH), tok(ML_WIDTH), tok(LANES), tok(ML_WIDTH),
                  pl.BlockSpec((1, ML_WIDTH), lambda bi, ci: (0, 0)),
                  st4(c_shape), st4(n_shape), st4(m_shape)],
        out_specs=[tok(ML_WIDTH), st4(c_shape), st4(n_shape), st4(m_shape)],
        out_shape=[jax.ShapeDtypeStruct((b, s, ML_WIDTH), F32),
                   jax.ShapeDtypeStruct((b,) + c_shape[1:], F32),
                   jax.ShapeDtypeStruct((b,) + n_shape[1:], F32),
                   jax.ShapeDtypeStruct((b,) + m_shape[1:], F32)],
        scratch_shapes=[pltpu.VMEM(c_shape[1:], F32), pltpu.VMEM(n_shape[1:], F32), pltpu.VMEM(m_shape[1:], F32)],
        compiler_params=pltpu.CompilerParams(dimension_semantics=("parallel", "arbitrary"),
                                             vmem_limit_bytes=VMEM_LIMIT),
        name="mlstm",
    )(mq, mk, mv, gates, mo, ml_norm, c0, n0, m0)


def _out_ffn_kernel(x_ref, att_ref, ml_ref, wo_ref, gf_ref, wu_ref, wd_ref, gl_ref, y_ref, *, ff_chunk):
    x1 = x_ref[...] \
        + jnp.dot(att_ref[...].astype(BF16), wo_ref[0:ATT_WIDTH, :], preferred_element_type=F32) \
        + jnp.dot(ml_ref[...].astype(BF16), wo_ref[ATT_WIDTH:, :], preferred_element_type=F32)
    xb = _rms(x1, gf_ref[...]).astype(BF16)
    ffn = None
    for c in range(D_FF // ff_chunk):
        sl = slice(c * ff_chunk, (c + 1) * ff_chunk)
        u = jnp.maximum(jnp.dot(xb, wu_ref[:, sl], preferred_element_type=F32), 0.0)
        d = jnp.dot((u * u).astype(BF16), wd_ref[sl, :], preferred_element_type=F32)
        ffn = d if ffn is None else ffn + d
    y_ref[...] = _rms(x1 + ffn, gl_ref[...])


def _out_ffn(x2, att, ml, w_out, norm_ffn, w_up, w_down, norm_final, tm, ff_chunk):
    n = x2.shape[0]
    const = lambda shape: pl.BlockSpec(shape, lambda i: (0, 0), pipeline_mode=pl.Buffered(1))
    row = lambda w: pl.BlockSpec((tm, w), lambda i: (i, 0))
    return pl.pallas_call(
        functools.partial(_out_ffn_kernel, ff_chunk=ff_chunk),
        grid=(n // tm,),
        in_specs=[row(D_MODEL), row(ATT_WIDTH), row(ML_WIDTH), const((D_MODEL, D_MODEL)), const((1, D_MODEL)),
                  const((D_MODEL, D_FF)), const((D_FF, D_MODEL)), const((1, D_MODEL))],
        out_specs=row(D_MODEL),
        out_shape=jax.ShapeDtypeStruct((n, D_MODEL), F32),
        compiler_params=pltpu.CompilerParams(dimension_semantics=("parallel",), vmem_limit_bytes=VMEM_LIMIT),
        name="out_proj_ffn",
    )(x2, att, ml, w_out, norm_ffn, w_up, w_down, norm_final)


def _layer_weights(l, norm_mix, w_in, b_ig, b_fg, ml_norm, w_out, norm_ffn, w_up, w_down):
    w_gate = jnp.pad(w_in[l, :, GATE_COL0:], ((0, 0), (0, LANES - 2 * N_ML_HEADS)))
    w_all = jnp.concatenate([w_in[l, :, :GATE_COL0].astype(BF16)] + _split_bf16(w_gate, 2), axis=1)
    b_gate = jnp.pad(jnp.concatenate([b_ig[l], b_fg[l]]), (0, LANES - 2 * N_ML_HEADS))[None, :]
    return dict(norm_mix=norm_mix[l][None, :], w_all=w_all, b_gate=b_gate,
                ml_norm=ml_norm[l][None, :], w_out=w_out[l].astype(BF16), norm_ffn=norm_ffn[l][None, :],
                w_up=w_up[l].astype(BF16), w_down=w_down[l].astype(BF16))


def _trunk_layer(x, w, attend, state0, norm_final, tiles):
    b, s, _ = x.shape
    x2 = x.reshape(b * s, D_MODEL)
    aq, ak, av, mq, mk, mv, mo, gates, *kv_t = _project(x2, w["norm_mix"], w["w_all"], w["b_gate"], tiles.tm, s)
    r3 = lambda a: a.reshape(b, s, a.shape[-1])
    if kv_t:
        k_out, v_out = (a.reshape(b, N_ATT_HEADS, ATT_HEAD_DIM, s).transpose(0, 3, 1, 2) for a in kv_t)
    else:
        k_out, v_out = (a.reshape(b, s, N_ATT_HEADS, ATT_HEAD_DIM) for a in (ak, av))
    att = attend(r3(aq), r3(ak), r3(av))
    c0, n0, m0 = state0
    ml, c_new, n_new, m_new = _mlstm(r3(mq), r3(mk), r3(mv), r3(gates), r3(mo), w["ml_norm"], c0, n0, m0, tiles.ml_chunk)
    y = _out_ffn(x2, att.reshape(b * s, ATT_WIDTH), ml.reshape(b * s, ML_WIDTH), w["w_out"], w["norm_ffn"],
                 w["w_up"], w["w_down"], norm_final, tiles.tm, tiles.ff_chunk)
    return y.reshape(b, s, D_MODEL), k_out, v_out, c_new, n_new[:, :, :, 0], m_new[:, :, 0, 0]


def kernel(x_prompt, x_sample, cache_k, cache_v, state_C, state_n, state_m, page_table, rel_bias, norm_mix, w_in,
           b_ig, b_fg, ml_norm, w_out, norm_ffn, w_up, w_down, norm_final):
    depth = w_in.shape[0]
    assert depth == 1, "the fused final RMSNorm assumes a single layer"
    bp, sp, _ = x_prompt.shape
    db, dec_seq, _ = x_sample.shape
    n_phys = cache_k.shape[1]
    t0, t1, s_own, s_near, s_far = _bias_tiles(rel_bias, dec_seq)
    gl = norm_final[None, :]
    l = 0
    w = _layer_weights(l, norm_mix, w_in, b_ig, b_fg, ml_norm, w_out, norm_ffn, w_up, w_down)

    zero_state = (jnp.zeros((bp, N_ML_HEADS, ML_HEAD_DIM, ML_HEAD_DIM), F32),
                  jnp.zeros((bp, N_ML_HEADS, ML_HEAD_DIM, LANES), F32), jnp.zeros((bp, N_ML_HEADS, 1, LANES), F32))
    yp, kp, vp, cp, np_, mp = _trunk_layer(
        x_prompt, w, lambda q, k, v: _moba_prompt(q, k, v, t0, t1, rel_bias, PROMPT_KEY_GROUP), zero_state, gl,
        PROMPT_TILES)

    to_pool = lambda a: jnp.transpose(a, (0, 1, 3, 4, 2)).reshape(depth * n_phys, ATT_WIDTH, PAGE_SIZE)
    pool_kt, pool_vt = to_pool(cache_k), to_pool(cache_v)
    pt = page_table + l * n_phys

    def attend_sample(q, k, v):
        return _moba_sample(q, k, v, pool_kt, pool_vt, pt, s_own, s_near, s_far, SAMPLE_BLOCKS_PER_STEP)

    state0 = (state_C[l], jnp.broadcast_to(state_n[l][:, :, :, None], (db, N_ML_HEADS, ML_HEAD_DIM, LANES)),
              jnp.broadcast_to(state_m[l][:, :, None, None], (db, N_ML_HEADS, 1, LANES)))
    ys, ks, vs, cs, ns, ms = _trunk_layer(x_sample, w, attend_sample, state0, gl, SAMPLE_TILES)

    st = lambda a: a[None]
    return (yp, ys, st(kp), st(vp), st(cp), st(np_), st(mp), st(ks), st(vs), st(cs), st(ns), st(ms))
```

```python
import functools
import math
from typing import NamedTuple

import numpy as np
import jax
import jax.numpy as jnp
from jax import lax
from jax.experimental import pallas as pl
from jax.experimental.pallas import tpu as pltpu

F32 = jnp.float32
BF16 = jnp.bfloat16

D_MODEL = 1024
ATT_HEAD_DIM = 64
N_ATT_HEADS = 8
ATT_WIDTH = 512
N_ML_HEADS = 4
ML_HEAD_DIM = 128
ML_WIDTH = 512
SLAB = 512
N_SLABS = 7
GATE_COL0 = N_SLABS * SLAB
MOBA_BLOCK = 256
MOBA_TOPK = 3
PAGE_SIZE = 128
NUM_BUCKETS = 32
MAX_DISTANCE = 128
D_FF = 4 * D_MODEL
RMS_EPS = 1e-6
NEG = -1e30
LOG2E = math.log2(math.e)
LANES = 128
SUM_ROWS = 16
MAX_SLACK = 64.0
BOUND_INFLATE = 1.05
BF16_LOGIT_ERR = 2.0 ** -6
VMEM_LIMIT = 56 * 1024 * 1024


class _Tiles(NamedTuple):
    tm: int
    ml_chunk: int
    ff_chunk: int = 1024


PROMPT_TILES = _Tiles(tm=512, ml_chunk=512)
SAMPLE_TILES = _Tiles(tm=256, ml_chunk=128)
PROMPT_KEY_GROUP = 4
SAMPLE_BLOCKS_PER_STEP = 16


def _nt_dot(a, b):
    return lax.dot_general(a, b, (((1,), (1,)), ((), ())), preferred_element_type=F32)


def _split_bf16(x, parts):
    out = []
    for _ in range(parts):
        p = x.astype(BF16)
        out.append(p)
        x = x - p.astype(F32)
    return out


def _dot_x3(a, b, contract):
    (ah, al), (bh, bl) = _split_bf16(a, 2), _split_bf16(b, 2)
    d = lambda x, y: lax.dot_general(x, y, (contract, ((), ())), preferred_element_type=F32)
    return d(ah, bh) + (d(ah, bl) + d(al, bh))


def _tn_dot(a, b):
    return lax.dot_general(a, b, (((0,), (0,)), ((), ())), preferred_element_type=F32)


def _bucket_upper_bounds():
    n = np.arange(0, MAX_DISTANCE + 1)
    max_exact = NUM_BUCKETS // 2
    nf = np.maximum(n, 1).astype(np.float32)
    large = max_exact + (np.log(nf / np.float32(max_exact)) / np.float32(math.log(MAX_DISTANCE / max_exact))
                         * np.float32(NUM_BUCKETS - max_exact)).astype(np.int32)
    large = np.minimum(large, NUM_BUCKETS - 1)
    bucket = np.where(n < max_exact, n, large)
    return np.array([int(n[bucket <= b].max()) for b in range(NUM_BUCKETS - 1)], dtype=np.int32)


def _bias_of_dist(dist, h, rb_ref, ub_ref):
    def body(i, acc):
        b = NUM_BUCKETS - 2 - i
        return jnp.where(dist <= ub_ref[b], rb_ref[b, h], acc)
    init = jnp.full(dist.shape, rb_ref[NUM_BUCKETS - 1, h], F32)
    return lax.fori_loop(0, NUM_BUCKETS - 1, body, init)


def _bias_tiles_kernel(rb_ref, ub_ref, t0_ref, t1_ref, s_own_ref, s_near_ref, s_far_ref, *, dec_seq):
    h = pl.program_id(0)
    kk = lax.broadcasted_iota(jnp.int32, (MOBA_BLOCK, MOBA_BLOCK), 0)
    qq = lax.broadcasted_iota(jnp.int32, (MOBA_BLOCK, MOBA_BLOCK), 1)
    d0 = qq - kk
    t0_ref[0] = jnp.where(d0 >= 0, _bias_of_dist(jnp.maximum(d0, 0), h, rb_ref, ub_ref) * LOG2E, NEG)
    t1_ref[0] = _bias_of_dist(d0 + MOBA_BLOCK, h, rb_ref, ub_ref) * LOG2E
    t = lax.broadcasted_iota(jnp.int32, (dec_seq, LANES), 0)
    cc = lax.broadcasted_iota(jnp.int32, (dec_seq, LANES), 1)
    ds = t - cc
    s_own_ref[0] = jnp.where((ds >= 0) & (cc < dec_seq),
                             _bias_of_dist(jnp.maximum(ds, 0), h, rb_ref, ub_ref) * LOG2E, NEG)
    s_near_ref[0] = _bias_of_dist(ds + PAGE_SIZE, h, rb_ref, ub_ref) * LOG2E
    s_far_ref[0] = jnp.full((dec_seq, LANES), rb_ref[NUM_BUCKETS - 1, h] * LOG2E, F32)


def _bias_tiles(rel_bias, dec_seq):
    ub = jnp.asarray(_bucket_upper_bounds())
    smem = pl.BlockSpec(memory_space=pltpu.SMEM)
    big = pl.BlockSpec((1, MOBA_BLOCK, MOBA_BLOCK), lambda h: (h, 0, 0))
    small = pl.BlockSpec((1, dec_seq, LANES), lambda h: (h, 0, 0))
    return pl.pallas_call(
        functools.partial(_bias_tiles_kernel, dec_seq=dec_seq),
        grid=(N_ATT_HEADS,),
        in_specs=[smem, smem],
        out_specs=[big, big, small, small, small],
        out_shape=[jax.ShapeDtypeStruct((N_ATT_HEADS, MOBA_BLOCK, MOBA_BLOCK), F32)] * 2
        + [jax.ShapeDtypeStruct((N_ATT_HEADS, dec_seq, LANES), F32)] * 3,
        name="bias_tiles",
    )(rel_bias, ub)


def _rms(x, g):
    return x * lax.rsqrt(jnp.mean(x * x, axis=-1, keepdims=True) + RMS_EPS) * g


def _log_sigmoid(z):
    return jnp.minimum(z, 0.0) - jnp.log1p(jnp.exp(-jnp.abs(z)))


def _proj_kernel(x_ref, g_ref, w_ref, bg_ref, aq_ref, ak_ref, av_ref, mq_ref, mk_ref, mv_ref, mo_ref,
                 gate_ref, *kv_t_refs):
    xn = _rms(x_ref[...], g_ref[...])
    xb = xn.astype(BF16)
    outs = (aq_ref, ak_ref, av_ref, mq_ref, mk_ref, mv_ref, mo_ref)
    scales = (ATT_HEAD_DIM ** -0.5 * LOG2E, None, None, None, ML_HEAD_DIM ** -0.5, None, None)
    for i, (o_ref, s) in enumerate(zip(outs, scales)):
        r = jnp.dot(xb, w_ref[:, i * SLAB:(i + 1) * SLAB], preferred_element_type=F32)
        o_ref[...] = r if s is None else r * s
        if kv_t_refs and i in (1, 2):
            kv_t_refs[i - 1][0] = r.T
    gate_hi_lo = jnp.dot(xb, w_ref[:, GATE_COL0:GATE_COL0 + 2 * LANES], preferred_element_type=F32)
    x_lo = (xn - xb.astype(F32)).astype(BF16)
    z = gate_hi_lo[:, :LANES] + (gate_hi_lo[:, LANES:]
                                 + jnp.dot(x_lo, w_ref[:, GATE_COL0:GATE_COL0 + LANES], preferred_element_type=F32))
    z = z + bg_ref[...]
    lane = lax.broadcasted_iota(jnp.int32, z.shape, 1)
    gate_ref[...] = jnp.where((lane >= N_ML_HEADS) & (lane < 2 * N_ML_HEADS), _log_sigmoid(z), z)


def _project(x2, norm_g, w_all, b_gate, tm, seq):
    n = x2.shape[0]
    const = lambda shape: pl.BlockSpec(shape, lambda i: (0, 0), pipeline_mode=pl.Buffered(1))
    slab = pl.BlockSpec((tm, SLAB), lambda i: (i, 0))
    out_specs = [slab] * N_SLABS + [pl.BlockSpec((tm, LANES), lambda i: (i, 0))]
    out_shape = [jax.ShapeDtypeStruct((n, SLAB), F32)] * N_SLABS + [jax.ShapeDtypeStruct((n, LANES), F32)]
    if seq % tm == 0:
        tiles = seq // tm
        out_specs += [pl.BlockSpec((1, SLAB, tm), lambda i: (i // tiles, 0, i % tiles))] * 2
        out_shape += [jax.ShapeDtypeStruct((n // seq, SLAB, seq), F32)] * 2
    return pl.pallas_call(
        _proj_kernel,
        grid=(n // tm,),
        in_specs=[pl.BlockSpec((tm, D_MODEL), lambda i: (i, 0)), const((1, D_MODEL)),
                  const((D_MODEL, N_SLABS * SLAB + 2 * LANES)), const((1, LANES))],
        out_specs=out_specs,
        out_shape=out_shape,
        compiler_params=pltpu.CompilerParams(dimension_semantics=("parallel",), vmem_limit_bytes=VMEM_LIMIT),
        name="rms_in_proj",
    )(x2, norm_g, w_all, b_gate)


def _topk_penalty(scores, n_valid):
    lane = lax.broadcasted_iota(jnp.int32, scores.shape, 1)
    rank = jnp.zeros(scores.shape, jnp.int32)
    for n in range(n_valid):
        col = scores[:, n:n + 1]
        rank = rank + jnp.where((col > scores) | ((col == scores) & (lane > n)), 1, 0)
    return jnp.where((rank < MOBA_TOPK) & (lane < n_valid), 0.0, NEG)


def _moba_prompt_kernel(rb_ref, q_ref, k_ref, v_ref, t0_ref, t1_ref, o_ref, km_ref, kb_ref, vt_ref, sc_ref,
                        pen_ref, sp_ref, rad_ref, cen_ref, m_ref, *, n_blocks, group):
    hp = pl.program_id(1)
    own = pl.program_id(2)
    blk = MOBA_BLOCK
    nbp = km_ref.shape[0]

    sel_l = lax.broadcasted_iota(jnp.int32, (LANES, LANES), 0)
    sel_c = lax.broadcasted_iota(jnp.int32, (LANES, LANES), 1)
    head_sel = jnp.where(sel_l // ATT_HEAD_DIM == sel_c, 1.0, 0.0).astype(BF16)

    @pl.when(own == 0)
    def _():
        km_ref[...] = jnp.zeros(km_ref.shape, F32)
        rad_ref[...] = jnp.zeros(rad_ref.shape, F32)
        cen_ref[...] = jnp.zeros(cen_ref.shape, F32)
        for n in range(n_blocks):
            kn = k_ref[0, n * blk:(n + 1) * blk, :]
            cn = jnp.mean(kn, axis=0, keepdims=True)
            km_ref[n:n + 1, :] = cn
            kb_ref[n] = kn.astype(BF16)
            dk = kn - cn
            rad_ref[n:n + 1, :] = jnp.max(jnp.dot((dk * dk).astype(BF16), head_sel, preferred_element_type=F32),
                                          axis=0, keepdims=True)
            cen_ref[n:n + 1, :] = jnp.dot(jnp.broadcast_to(cn * cn, (8, LANES)).astype(BF16), head_sel,
                                          preferred_element_type=F32)[0:1]
            vtn = v_ref[0, n * blk:(n + 1) * blk, :].T.astype(BF16)
            for hh in range(2):
                vt_ref[n, hh, 0:ATT_HEAD_DIM, :] = vtn[hh * ATT_HEAD_DIM:(hh + 1) * ATT_HEAD_DIM]
                vt_ref[n, hh, ATT_HEAD_DIM:, :] = jnp.ones((SUM_ROWS, blk), BF16)

    q2 = q_ref[0]
    lane = lax.broadcasted_iota(jnp.int32, q2.shape, 1)
    q_hi, q_lo = _split_bf16(q2, 2)
    qcat_b = jnp.concatenate([jnp.where((lane // ATT_HEAD_DIM) == hh, q_hi, jnp.zeros_like(q_hi))
                              for hh in range(2)], axis=0)
    wide = 2 * blk
    lane_m = lax.broadcasted_iota(jnp.int32, (nbp, LANES), 1)
    km2 = jnp.concatenate([jnp.where((lane_m // ATT_HEAD_DIM) == hh, km_ref[...], 0.0) for hh in range(2)], axis=0)
    km_hi, km_lo = _split_bf16(km2, 2)
    hi_q = _nt_dot(jnp.concatenate([km_hi, km_lo], axis=0), q_hi)
    sc2 = hi_q[:2 * nbp] + (hi_q[2 * nbp:] + _nt_dot(km_hi, q_lo))
    sc_ref[...] = jnp.concatenate([sc2[:nbp], sc2[nbp:]], axis=1)

    sub = lax.broadcasted_iota(jnp.int32, (nbp, wide), 0)

    def rank_body(n, rank):
        sc = sc_ref[...]
        row = sc_ref[pl.ds(n, 1), :]
        return rank + jnp.where((row > sc) | ((row == sc) & (sub > n)), 1, 0)

    rank = lax.fori_loop(0, own, rank_body, jnp.zeros((nbp, wide), jnp.int32))
    pen_ref[...] = jnp.where((rank < MOBA_TOPK) & (sub < own), 0.0, NEG)

    col_head = lax.broadcasted_iota(jnp.int32, (1, wide), 1) // blk
    far_bias = jnp.where(col_head == 0, rb_ref[NUM_BUCKETS - 1, 2 * hp], rb_ref[NUM_BUCKETS - 1, 2 * hp + 1]) * LOG2E

    def fold8(x, op):
        return op(x.reshape(blk // 8, 8, x.shape[1]), axis=0)

    def far_row(j):
        return jnp.where(j < own - 1, pen_ref[pl.ds(j, 1), :] + far_bias, NEG)

    grp = group
    n_groups = (jnp.maximum(own - 1, 0) + grp - 1) // grp
    jn = jnp.maximum(own - 1, 0)
    near_row = jnp.where(own >= 1, pen_ref[pl.ds(jn, 1), :], NEG)

    t0c = jnp.concatenate([t0_ref[0], t0_ref[1]], axis=1)
    t1c = jnp.concatenate([t1_ref[0], t1_ref[1]], axis=1)
    no_prev = own == 0
    s_pair = _nt_dot(kb_ref[pl.ds(jn, 2)].reshape(2 * blk, LANES), qcat_b)
    s_first = s_pair[:blk] + jnp.where(no_prev, t0c, t1c + near_row)
    s_second = jnp.where(no_prev, NEG, s_pair[blk:] + t0c)
    sp_ref[0:blk, :] = s_first
    sp_ref[blk:, :] = s_second
    m8 = jnp.maximum(fold8(s_first, jnp.max), fold8(s_second, jnp.max))
    pair_max = jnp.max(m8, axis=0, keepdims=True)

    row8 = lax.broadcasted_iota(jnp.int32, (8, LANES), 0)
    lane8 = lax.broadcasted_iota(jnp.int32, (8, LANES), 1)
    q_sq = _nt_dot(jnp.where(lane8 // ATT_HEAD_DIM == row8, 1.0, 0.0).astype(BF16),
                   (q2 * q2).astype(BF16))
    q_norm = jnp.sqrt(jnp.concatenate([q_sq[0:1], q_sq[1:2]], axis=1))
    per_head = lambda x: jnp.concatenate([jnp.broadcast_to(x[:, hh:hh + 1], (nbp, blk)) for hh in range(2)], axis=1)
    radius, centre = per_head(jnp.sqrt(rad_ref[...])), per_head(jnp.sqrt(cen_ref[...]))
    far_pen = jnp.where(sub < own - 1, pen_ref[...], NEG)
    mean_logit = sc_ref[...] + far_bias + far_pen
    spread = q_norm * (BOUND_INFLATE * radius + BF16_LOGIT_ERR * (centre + radius))
    far_low = jnp.max(mean_logit, axis=0, keepdims=True)
    far_high = jnp.max(mean_logit + spread, axis=0, keepdims=True)
    m_ref[...] = jnp.maximum(pair_max, far_high)
    slack = jnp.maximum(pair_max, far_high) - jnp.maximum(pair_max, far_low)

    @pl.when(jnp.max(slack) > MAX_SLACK)
    def _():
        def max_body(g, m8c):
            j0 = pl.multiple_of(g * grp, grp)
            sg = _nt_dot(kb_ref[pl.ds(j0, grp)].reshape(grp * blk, LANES), qcat_b)
            for i in range(grp):
                m8c = jnp.maximum(m8c, fold8(sg[i * blk:(i + 1) * blk], jnp.max) + far_row(j0 + i))
            return m8c

        m_ref[...] = jnp.max(lax.fori_loop(0, n_groups, max_body, m8), axis=0, keepdims=True)

    m = m_ref[...]

    def accumulate(j, p, accs):
        pb = p.astype(BF16)
        return tuple(accs[hh] + jnp.dot(vt_ref[j, hh], pb[:, hh * blk:(hh + 1) * blk], preferred_element_type=F32)
                     for hh in range(2))

    pb_pair = jnp.exp2(sp_ref[...] - m).astype(BF16)
    vt_pair = vt_ref[pl.ds(jn, 2)]
    accs = tuple(jnp.dot(jnp.concatenate([vt_pair[i, hh] for i in range(2)], axis=1),
                         pb_pair[:, hh * blk:(hh + 1) * blk], preferred_element_type=F32) for hh in range(2))

    def exp_body(g, accc):
        j0 = pl.multiple_of(g * grp, grp)
        sg = _nt_dot(kb_ref[pl.ds(j0, grp)].reshape(grp * blk, LANES), qcat_b)
        for i in range(grp):
            accc = accumulate(j0 + i, jnp.exp2(sg[i * blk:(i + 1) * blk] + (far_row(j0 + i) - m)), accc)
        return accc

    accs = lax.fori_loop(0, n_groups, exp_body, accs)
    out_t = jnp.concatenate([a[:ATT_HEAD_DIM] / a[ATT_HEAD_DIM:ATT_HEAD_DIM + 1] for a in accs], axis=0)
    o_ref[0] = out_t.T


def _moba_prompt(aq, ak, av, t0, t1, rel_bias, group):
    b, s, _ = aq.shape
    nb = s // MOBA_BLOCK
    nbp = -(-nb // 8) * 8
    n_pairs = N_ATT_HEADS // 2
    wide = 2 * MOBA_BLOCK
    assert nb % group == 0
    qspec = pl.BlockSpec((1, MOBA_BLOCK, LANES), lambda bi, hp, qi: (bi, qi, hp))
    kvspec = pl.BlockSpec((1, s, LANES), lambda bi, hp, qi: (bi, 0, hp))
    tspec = pl.BlockSpec((2, MOBA_BLOCK, MOBA_BLOCK), lambda bi, hp, qi: (hp, 0, 0))
    return pl.pallas_call(
        functools.partial(_moba_prompt_kernel, n_blocks=nb, group=group),
        grid=(b, n_pairs, nb),
        in_specs=[pl.BlockSpec(memory_space=pltpu.SMEM), qspec, kvspec, kvspec, tspec, tspec],
        out_specs=qspec,
        out_shape=jax.ShapeDtypeStruct((b, s, ATT_WIDTH), F32),
        scratch_shapes=[pltpu.VMEM((nbp, LANES), F32), pltpu.VMEM((nb, MOBA_BLOCK, LANES), BF16),
                        pltpu.VMEM((nb, 2, ATT_HEAD_DIM + SUM_ROWS, MOBA_BLOCK), BF16),
                        pltpu.VMEM((nbp, wide), F32), pltpu.VMEM((nbp, wide), F32),
                        pltpu.VMEM((2 * MOBA_BLOCK, wide), F32), pltpu.VMEM((nbp, LANES), F32),
                        pltpu.VMEM((nbp, LANES), F32), pltpu.VMEM((1, wide), F32)],
        compiler_params=pltpu.CompilerParams(dimension_semantics=("parallel", "parallel", "arbitrary"),
                                             vmem_limit_bytes=VMEM_LIMIT),
        name="moba_prompt",
    )(rel_bias, aq, ak, av, t0, t1)


def _moba_sample_kernel(pt_ref, *refs, blocks_per_step, n_blocks, dec_seq):
    ppb = MOBA_BLOCK // PAGE_SIZE
    pps = blocks_per_step * ppb
    k_pages, v_pages = refs[:pps], refs[pps:2 * pps]
    (q_ref, kn_ref, vn_ref, s_own_ref, s_near_ref, s_far_ref, o_ref,
     qbd_ref, kmt_ref, mm_ref, ll_ref, oo_ref, own_m_ref, own_l_ref, own_o_ref) = refs[2 * pps:]
    c = pl.program_id(1)
    rows = N_ATT_HEADS * dec_seq
    row_head = lax.broadcasted_iota(jnp.int32, (rows, ATT_WIDTH), 0) // dec_seq
    lane_head = lax.broadcasted_iota(jnp.int32, (rows, ATT_WIDTH), 1) // ATT_HEAD_DIM
    lane_r = lax.broadcasted_iota(jnp.int32, (rows, LANES), 1)
    lane_k = lax.broadcasted_iota(jnp.int32, (ATT_WIDTH, LANES), 1)

    def local_softmax(s):
        m = jnp.max(s, axis=-1, keepdims=True)
        p = jnp.exp2(s - m)
        return m, jnp.sum(p, axis=-1, keepdims=True), p.astype(BF16)

    @pl.when(c == 0)
    def _():
        qrep = jnp.concatenate([q_ref[0]] * N_ATT_HEADS, axis=0)
        qbd = jnp.where(row_head == lane_head, qrep, 0.0)
        qbd_ref[...] = qbd
        kmt_ref[...] = jnp.zeros(kmt_ref.shape, F32)
        mm_ref[...] = jnp.zeros(mm_ref.shape, F32)
        ll_ref[...] = jnp.zeros(ll_ref.shape, F32)
        pad = jnp.zeros((LANES - dec_seq, ATT_WIDTH), F32)
        kn = jnp.concatenate([kn_ref[0], pad], axis=0).astype(BF16)
        vn = jnp.concatenate([vn_ref[0], pad], axis=0).astype(BF16)
        m, l, p = local_softmax(_nt_dot(qbd.astype(BF16), kn) + s_own_ref[...].reshape(rows, LANES))
        own_m_ref[...] = jnp.broadcast_to(m, (rows, LANES))
        own_l_ref[...] = jnp.broadcast_to(l, (rows, LANES))
        own_o_ref[...] = jnp.dot(p, vn, preferred_element_type=F32)

    qb = qbd_ref[...].astype(BF16)
    s_near = s_near_ref[...].reshape(rows, LANES)
    s_far = s_far_ref[...].reshape(rows, LANES)
    n0 = c * blocks_per_step
    kt = jnp.concatenate([k_pages[i][0] for i in range(pps)], axis=1)
    s_all = jnp.dot(qb, kt.astype(BF16), preferred_element_type=F32)
    mm, ll, kmt = mm_ref[...], ll_ref[...], kmt_ref[...]
    for g in range(blocks_per_step):
        n = n0 + g
        bias = jnp.concatenate([s_far] * (ppb - 1) + [jnp.where(n == n_blocks - 1, s_near, s_far)], axis=1)
        m, l, p = local_softmax(s_all[:, g * MOBA_BLOCK:(g + 1) * MOBA_BLOCK] + bias)
        vt = jnp.concatenate([v_pages[ppb * g + i][0] for i in range(ppb)], axis=1).astype(BF16)
        oo_ref[n] = _nt_dot(p, vt)
        mm = jnp.where(lane_r == n, m, mm)
        ll = jnp.where(lane_r == n, l, ll)
        ksum = jnp.sum(kt[:, g * MOBA_BLOCK:(g + 1) * MOBA_BLOCK], axis=-1, keepdims=True)
        kmt = jnp.where(lane_k == n, ksum, kmt)
    mm_ref[...] = mm
    ll_ref[...] = ll
    kmt_ref[...] = kmt

    @pl.when(c == pl.num_programs(1) - 1)
    def _():
        kmean_t = kmt_ref[...] * (1.0 / MOBA_BLOCK)
        pen = _topk_penalty(_dot_x3(qbd_ref[...], kmean_t, ((1,), (0,))), n_blocks)
        mp = mm_ref[...] + pen
        m_own = own_m_ref[...]
        m_tot = jnp.maximum(jnp.max(mp, axis=-1, keepdims=True), m_own)
        w = jnp.exp2(mp - m_tot)
        w_own = jnp.exp2(m_own - m_tot)
        den = jnp.sum(w * ll_ref[...], axis=-1, keepdims=True) + w_own * own_l_ref[...]

        num = w_own[:, 0:1] * own_o_ref[...]
        for nn in range(n_blocks):
            num = num + w[:, nn:nn + 1] * oo_ref[nn]
        full = jnp.where(row_head == lane_head, num / den[:, 0:1], 0.0)
        out = full[0:dec_seq]
        for h in range(1, N_ATT_HEADS):
            out = out + full[h * dec_seq:(h + 1) * dec_seq]
        o_ref[0] = out


def _moba_sample(aq, ak, av, pool_kt, pool_vt, page_table, s_own, s_near, s_far, blocks_per_step):
    db, dec_seq, _ = aq.shape
    n_pages = page_table.shape[1]
    ppb = MOBA_BLOCK // PAGE_SIZE
    nb = n_pages // ppb
    pps = blocks_per_step * ppb
    assert nb % blocks_per_step == 0 and nb <= LANES
    rows = N_ATT_HEADS * dec_seq

    def page_spec(i):
        return pl.BlockSpec((1, ATT_WIDTH, PAGE_SIZE), lambda b, c, pt: (pt[b, c * pps + i], 0, 0))

    per_b = pl.BlockSpec((1, dec_seq, ATT_WIDTH), lambda b, c, pt: (b, 0, 0))
    tile = pl.BlockSpec((N_ATT_HEADS, dec_seq, LANES), lambda b, c, pt: (0, 0, 0))
    grid_spec = pltpu.PrefetchScalarGridSpec(
        num_scalar_prefetch=1, grid=(db, nb // blocks_per_step),
        in_specs=[page_spec(i) for i in range(pps)] * 2 + [per_b, per_b, per_b, tile, tile, tile],
        out_specs=per_b,
        scratch_shapes=[pltpu.VMEM((rows, ATT_WIDTH), F32), pltpu.VMEM((ATT_WIDTH, LANES), F32),
                        pltpu.VMEM((rows, LANES), F32), pltpu.VMEM((rows, LANES), F32),
                        pltpu.VMEM((nb, rows, ATT_WIDTH), F32), pltpu.VMEM((rows, LANES), F32),
                        pltpu.VMEM((rows, LANES), F32), pltpu.VMEM((rows, ATT_WIDTH), F32)])
    return pl.pallas_call(
        functools.partial(_moba_sample_kernel, blocks_per_step=blocks_per_step, n_blocks=nb, dec_seq=dec_seq),
        grid_spec=grid_spec,
        out_shape=jax.ShapeDtypeStruct((db, dec_seq, ATT_WIDTH), F32),
        compiler_params=pltpu.CompilerParams(dimension_semantics=("parallel", "arbitrary"),
                                             vmem_limit_bytes=VMEM_LIMIT),
        name="moba_sample",
    )(page_table, *([pool_kt] * pps), *([pool_vt] * pps), aq, ak, av, s_own, s_near, s_far)


def _mlstm_kernel(q_ref, k_ref, v_ref, g_ref, mo_ref, nw_ref, c0_ref, n0_ref, m0_ref,
                  h_ref, c_out, n_out, m_out, c_s, n_s, m_s, *, rows, chunk):
    ci = pl.program_id(1)
    L = chunk

    @pl.when(ci == 0)
    def _():
        c_s[...] = c0_ref[0]
        n_s[...] = n0_ref[0]
        m_s[...] = m0_ref[0]

    def padded(ref, fill=None):
        x = ref[0]
        if rows == L:
            return x
        tail = jnp.zeros((L - rows, x.shape[1]), F32) if fill is None else fill
        return jnp.concatenate([x, tail], axis=0)

    lane = lax.broadcasted_iota(jnp.int32, (L, LANES), 1)
    pad_lane = lax.broadcasted_iota(jnp.int32, (max(L - rows, 1), LANES), 1)
    g = padded(g_ref, jnp.where(pad_lane < N_ML_HEADS, NEG, 0.0))
    q_all, k_all, v_all, mo_all = padded(q_ref), padded(k_ref), padded(v_ref), padded(mo_ref)

    is_lf = (lane >= N_ML_HEADS) & (lane < 2 * N_ML_HEADS)
    tri_r = lax.broadcasted_iota(jnp.int32, (L, L), 0)
    tri_c = lax.broadcasted_iota(jnp.int32, (L, L), 1)
    causal = tri_c <= tri_r
    g3 = jnp.concatenate(_split_bf16(jnp.where(is_lf, g, 0.0), 3), axis=1)
    b3 = jnp.dot(causal.astype(BF16), g3, preferred_element_type=F32)
    bcum = b3[:, :LANES] + (b3[:, LANES:2 * LANES] + b3[:, 2 * LANES:])
    g_t = g.T
    b_t = bcum.T

    ones_b = jnp.ones((L, LANES), BF16)
    reps = L // LANES
    wide = lambda x: x if reps == 1 else jnp.concatenate([x] * reps, axis=1)
    for h in range(N_ML_HEADS):
        sl = slice(h * ML_HEAD_DIM, (h + 1) * ML_HEAD_DIM)
        q, k, v = q_all[:, sl], k_all[:, sl], v_all[:, sl]
        ig = jnp.broadcast_to(g[:, h:h + 1], (L, LANES))
        b = jnp.broadcast_to(bcum[:, N_ML_HEADS + h:N_ML_HEADS + h + 1], (L, LANES))
        ig_row = g_t[h:h + 1, :]
        b_row = b_t[N_ML_HEADS + h:N_ML_HEADS + h + 1, :]
        m_prev = m_s[h]
        C = c_s[h]
        n_rep = n_s[h]

        a = b + m_prev
        log_d = jnp.where(causal, wide(b) - b_row + ig_row, NEG)
        m_t = jnp.maximum(a, jnp.max(log_d, axis=-1, keepdims=True))
        d = jnp.exp(log_d - wide(m_t))
        w_state = jnp.exp(a - m_t)
        qb, kb, vb = q.astype(BF16), k.astype(BF16), v.astype(BF16)
        s = _nt_dot(qb, kb) * d
        v_ones = jnp.concatenate([vb, ones_b], axis=1)
        from_state = jnp.dot(qb, jnp.concatenate([C, n_rep], axis=1).astype(BF16), preferred_element_type=F32)
        from_chunk = jnp.dot(s.astype(BF16), v_ones, preferred_element_type=F32)
        num = w_state * from_state[:, :ML_HEAD_DIM] + from_chunk[:, :ML_HEAD_DIM]
        den = w_state * from_state[:, ML_HEAD_DIM:] + from_chunk[:, ML_HEAD_DIM:]
        hid = num / jnp.maximum(jnp.abs(den), jnp.exp(-m_t))

        hn = _rms(hid, nw_ref[:, sl])
        h_ref[0, :, sl] = (hn * jax.nn.sigmoid(mo_all[:, sl]))[:rows]

        b_last = b[L - 1:L, :]
        m_new = m_t[L - 1:L, :]
        g_state = jnp.exp(b_last + m_prev - m_new)
        kg = k * jnp.exp(b_last - b + ig - m_new)
        added = _tn_dot(kg.astype(BF16), v_ones)
        c_s[h] = g_state * C + added[:, :ML_HEAD_DIM]
        n_s[h] = g_state * n_rep + added[:, ML_HEAD_DIM:]
        m_s[h] = m_new

    @pl.when(ci == pl.num_programs(1) - 1)
    def _():
        c_out[0] = c_s[...]
        n_out[0] = n_s[...]
        m_out[0] = m_s[...]


def _mlstm(mq, mk, mv, gates, mo, ml_norm, c0, n0, m0, chunk):
    b, s, _ = mq.shape
    rows = min(s, chunk)
    n_chunks = s // rows
    tok = lambda w: pl.BlockSpec((1, rows, w), lambda bi, ci: (bi, ci, 0))
    st4 = lambda shape: pl.BlockSpec(shape, lambda bi, ci: (bi, 0, 0, 0))
    c_shape, m_shape = (1, N_ML_HEADS, ML_HEAD_DIM, ML_HEAD_DIM), (1, N_ML_HEADS, 1, LANES)
    n_shape = (1, N_ML_HEADS, ML_HEAD_DIM, LANES)
    return pl.pallas_call(
        functools.partial(_mlstm_kernel, rows=rows, chunk=chunk),
        grid=(b, n_chunks),
        in_specs=[tok(ML_WIDTH), tok(ML_WIDTH), tok(ML_WIDTH), tok(LANES), tok(ML_WIDTH),
                  pl.BlockSpec((1, ML_WIDTH), lambda bi, ci: (0, 0)),
                  st4(c_shape), st4(n_shape), st4(m_shape)],
        out_specs=[tok(ML_WIDTH), st4(c_shape), st4(n_shape), st4(m_shape)],
        out_shape=[jax.ShapeDtypeStruct((b, s, ML_WIDTH), F32),
                   jax.ShapeDtypeStruct((b,) + c_shape[1:], F32),
                   jax.ShapeDtypeStruct((b,) + n_shape[1:], F32),
                   jax.ShapeDtypeStruct((b,) + m_shape[1:], F32)],
        scratch_shapes=[pltpu.VMEM(c_shape[1:], F32), pltpu.VMEM(n_shape[1:], F32), pltpu.VMEM(m_shape[1:], F32)],
        compiler_params=pltpu.CompilerParams(dimension_semantics=("parallel", "arbitrary"),
                                             vmem_limit_bytes=VMEM_LIMIT),
        name="mlstm",
    )(mq, mk, mv, gates, mo, ml_norm, c0, n0, m0)


def _out_ffn_kernel(x_ref, att_ref, ml_ref, wo_ref, gf_ref, wu_ref, wd_ref, gl_ref, y_ref, *, ff_chunk):
    x1 = x_ref[...] \
        + jnp.dot(att_ref[...].astype(BF16), wo_ref[0:ATT_WIDTH, :], preferred_element_type=F32) \
        + jnp.dot(ml_ref[...].astype(BF16), wo_ref[ATT_WIDTH:, :], preferred_element_type=F32)
    xb = _rms(x1, gf_ref[...]).astype(BF16)
    ffn = None
    for c in range(D_FF // ff_chunk):
        sl = slice(c * ff_chunk, (c + 1) * ff_chunk)
        u = jnp.maximum(jnp.dot(xb, wu_ref[:, sl], preferred_element_type=F32), 0.0)
        d = jnp.dot((u * u).astype(BF16), wd_ref[sl, :], preferred_element_type=F32)
        ffn = d if ffn is None else ffn + d
    y_ref[...] = _rms(x1 + ffn, gl_ref[...])


def _out_ffn(x2, att, ml, w_out, norm_ffn, w_up, w_down, norm_final, tm, ff_chunk):
    n = x2.shape[0]
    const = lambda shape: pl.BlockSpec(shape, lambda i: (0, 0), pipeline_mode=pl.Buffered(1))
    row = lambda w: pl.BlockSpec((tm, w), lambda i: (i, 0))
    return pl.pallas_call(
        functools.partial(_out_ffn_kernel, ff_chunk=ff_chunk),
        grid=(n // tm,),
        in_specs=[row(D_MODEL), row(ATT_WIDTH), row(ML_WIDTH), const((D_MODEL, D_MODEL)), const((1, D_MODEL)),
                  const((D_MODEL, D_FF)), const((D_FF, D_MODEL)), const((1, D_MODEL))],
        out_specs=row(D_MODEL),
        out_shape=jax.ShapeDtypeStruct((n, D_MODEL), F32),
        compiler_params=pltpu.CompilerParams(dimension_semantics=("parallel",), vmem_limit_bytes=VMEM_LIMIT),
        name="out_proj_ffn",
    )(x2, att, ml, w_out, norm_ffn, w_up, w_down, norm_final)


def _layer_weights(l, norm_mix, w_in, b_ig, b_fg, ml_norm, w_out, norm_ffn, w_up, w_down):
    w_gate = jnp.pad(w_in[l, :, GATE_COL0:], ((0, 0), (0, LANES - 2 * N_ML_HEADS)))
    w_all = jnp.concatenate([w_in[l, :, :GATE_COL0].astype(BF16)] + _split_bf16(w_gate, 2), axis=1)
    b_gate = jnp.pad(jnp.concatenate([b_ig[l], b_fg[l]]), (0, LANES - 2 * N_ML_HEADS))[None, :]
    return dict(norm_mix=norm_mix[l][None, :], w_all=w_all, b_gate=b_gate,
                ml_norm=ml_norm[l][None, :], w_out=w_out[l].astype(BF16), norm_ffn=norm_ffn[l][None, :],
                w_up=w_up[l].astype(BF16), w_down=w_down[l].astype(BF16))


def _trunk_layer(x, w, attend, state0, norm_final, tiles):
    b, s, _ = x.shape
    x2 = x.reshape(b * s, D_MODEL)
    aq, ak, av, mq, mk, mv, mo, gates, *kv_t = _project(x2, w["norm_mix"], w["w_all"], w["b_gate"], tiles.tm, s)
    r3 = lambda a: a.reshape(b, s, a.shape[-1])
    if kv_t:
        k_out, v_out = (a.reshape(b, N_ATT_HEADS, ATT_HEAD_DIM, s).transpose(0, 3, 1, 2) for a in kv_t)
    else:
        k_out, v_out = (a.reshape(b, s, N_ATT_HEADS, ATT_HEAD_DIM) for a in (ak, av))
    att = attend(r3(aq), r3(ak), r3(av))
    c0, n0, m0 = state0
    ml, c_new, n_new, m_new = _mlstm(r3(mq), r3(mk), r3(mv), r3(gates), r3(mo), w["ml_norm"], c0, n0, m0, tiles.ml_chunk)
    y = _out_ffn(x2, att.reshape(b * s, ATT_WIDTH), ml.reshape(b * s, ML_WIDTH), w["w_out"], w["norm_ffn"],
                 w["w_up"], w["w_down"], norm_final, tiles.tm, tiles.ff_chunk)
    return y.reshape(b, s, D_MODEL), k_out, v_out, c_new, n_new[:, :, :, 0], m_new[:, :, 0, 0]


def kernel(x_prompt, x_sample, cache_k, cache_v, state_C, state_n, state_m, page_table, rel_bias, norm_mix, w_in,
           b_ig, b_fg, ml_norm, w_out, norm_ffn, w_up, w_down, norm_final):
    depth = w_in.shape[0]
    assert depth == 1, "the fused final RMSNorm assumes a single layer"
    bp, sp, _ = x_prompt.shape
    db, dec_seq, _ = x_sample.shape
    n_phys = cache_k.shape[1]
    t0, t1, s_own, s_near, s_far = _bias_tiles(rel_bias, dec_seq)
    gl = norm_final[None, :]
    l = 0
    w = _layer_weights(l, norm_mix, w_in, b_ig, b_fg, ml_norm, w_out, norm_ffn, w_up, w_down)

    zero_state = (jnp.zeros((bp, N_ML_HEADS, ML_HEAD_DIM, ML_HEAD_DIM), F32),
                  jnp.zeros((bp, N_ML_HEADS, ML_HEAD_DIM, LANES), F32), jnp.zeros((bp, N_ML_HEADS, 1, LANES), F32))
    yp, kp, vp, cp, np_, mp = _trunk_layer(
        x_prompt, w, lambda q, k, v: _moba_prompt(q, k, v, t0, t1, rel_bias, PROMPT_KEY_GROUP), zero_state, gl,
        PROMPT_TILES)

    to_pool = lambda a: jnp.transpose(a, (0, 1, 3, 4, 2)).reshape(depth * n_phys, ATT_WIDTH, PAGE_SIZE)
    pool_kt, pool_vt = to_pool(cache_k), to_pool(cache_v)
    pt = page_table + l * n_phys

    def attend_sample(q, k, v):
        return _moba_sample(q, k, v, pool_kt, pool_vt, pt, s_own, s_near, s_far, SAMPLE_BLOCKS_PER_STEP)

    state0 = (state_C[l], jnp.broadcast_to(state_n[l][:, :, :, None], (db, N_ML_HEADS, ML_HEAD_DIM, LANES)),
              jnp.broadcast_to(state_m[l][:, :, None, None], (db, N_ML_HEADS, 1, LANES)))
    ys, ks, vs, cs, ns, ms = _trunk_layer(x_sample, w, attend_sample, state0, gl, SAMPLE_TILES)

    st = lambda a: a[None]
    return (yp, ys, st(kp), st(vp), st(cp), st(np_), st(mp), st(ks), st(vs), st(cs), st(ns), st(ms))
```

```python
import functools
import math
from typing import NamedTuple

import numpy as np
import jax
import jax.numpy as jnp
from jax import lax
from jax.experimental import pallas as pl
from jax.experimental.pallas import tpu as pltpu

F32 = jnp.float32
BF16 = jnp.bfloat16

D_MODEL = 1024
ATT_HEAD_DIM = 64
N_ATT_HEADS = 8
ATT_WIDTH = 512
N_ML_HEADS = 4
ML_HEAD_DIM = 128
ML_WIDTH = 512
SLAB = 512
N_SLABS = 7
GATE_COL0 = N_SLABS * SLAB
MOBA_BLOCK = 256
MOBA_TOPK = 3
PAGE_SIZE = 128
NUM_BUCKETS = 32
MAX_DISTANCE = 128
D_FF = 4 * D_MODEL
RMS_EPS = 1e-6
NEG = -1e30
LOG2E = math.log2(math.e)
LANES = 128
SUM_ROWS = 16
MAX_SLACK = 64.0
BOUND_INFLATE = 1.05
BF16_LOGIT_ERR = 2.0 ** -6
VMEM_LIMIT = 56 * 1024 * 1024


class _Tiles(NamedTuple):
    tm: int
    ml_chunk: int
    ff_chunk: int = 1024


PROMPT_TILES = _Tiles(tm=512, ml_chunk=512)
SAMPLE_TILES = _Tiles(tm=256, ml_chunk=128)
PROMPT_KEY_GROUP = 4
PROMPT_HEADS_PER_STEP = 4
SAMPLE_BLOCKS_PER_STEP = 16


def _nt_dot(a, b):
    return lax.dot_general(a, b, (((1,), (1,)), ((), ())), preferred_element_type=F32)


def _split_bf16(x, parts):
    out = []
    for _ in range(parts):
        p = x.astype(BF16)
        out.append(p)
        x = x - p.astype(F32)
    return out


def _dot_x3(a, b, contract):
    (ah, al), (bh, bl) = _split_bf16(a, 2), _split_bf16(b, 2)
    d = lambda x, y: lax.dot_general(x, y, (contract, ((), ())), preferred_element_type=F32)
    return d(ah, bh) + (d(ah, bl) + d(al, bh))


def _tn_dot(a, b):
    return lax.dot_general(a, b, (((0,), (0,)), ((), ())), preferred_element_type=F32)


def _bucket_upper_bounds():
    n = np.arange(0, MAX_DISTANCE + 1)
    max_exact = NUM_BUCKETS // 2
    nf = np.maximum(n, 1).astype(np.float32)
    large = max_exact + (np.log(nf / np.float32(max_exact)) / np.float32(math.log(MAX_DISTANCE / max_exact))
                         * np.float32(NUM_BUCKETS - max_exact)).astype(np.int32)
    large = np.minimum(large, NUM_BUCKETS - 1)
    bucket = np.where(n < max_exact, n, large)
    return np.array([int(n[bucket <= b].max()) for b in range(NUM_BUCKETS - 1)], dtype=np.int32)


def _bias_of_dist(dist, h, rb_ref, ub_ref):
    def body(i, acc):
        b = NUM_BUCKETS - 2 - i
        return jnp.where(dist <= ub_ref[b], rb_ref[b, h], acc)
    init = jnp.full(dist.shape, rb_ref[NUM_BUCKETS - 1, h], F32)
    return lax.fori_loop(0, NUM_BUCKETS - 1, body, init)


def _bias_tiles_kernel(rb_ref, ub_ref, t0_ref, t1_ref, s_own_ref, s_near_ref, s_far_ref, *, dec_seq):
    h = pl.program_id(0)
    kk = lax.broadcasted_iota(jnp.int32, (MOBA_BLOCK, MOBA_BLOCK), 0)
    qq = lax.broadcasted_iota(jnp.int32, (MOBA_BLOCK, MOBA_BLOCK), 1)
    d0 = qq - kk
    t0_ref[0] = jnp.where(d0 >= 0, _bias_of_dist(jnp.maximum(d0, 0), h, rb_ref, ub_ref) * LOG2E, NEG)
    t1_ref[0] = _bias_of_dist(d0 + MOBA_BLOCK, h, rb_ref, ub_ref) * LOG2E
    t = lax.broadcasted_iota(jnp.int32, (dec_seq, LANES), 0)
    cc = lax.broadcasted_iota(jnp.int32, (dec_seq, LANES), 1)
    ds = t - cc
    s_own_ref[0] = jnp.where((ds >= 0) & (cc < dec_seq),
                             _bias_of_dist(jnp.maximum(ds, 0), h, rb_ref, ub_ref) * LOG2E, NEG)
    s_near_ref[0] = _bias_of_dist(ds + PAGE_SIZE, h, rb_ref, ub_ref) * LOG2E
    s_far_ref[0] = jnp.full((dec_seq, LANES), rb_ref[NUM_BUCKETS - 1, h] * LOG2E, F32)


def _bias_tiles(rel_bias, dec_seq):
    ub = jnp.asarray(_bucket_upper_bounds())
    smem = pl.BlockSpec(memory_space=pltpu.SMEM)
    big = pl.BlockSpec((1, MOBA_BLOCK, MOBA_BLOCK), lambda h: (h, 0, 0))
    small = pl.BlockSpec((1, dec_seq, LANES), lambda h: (h, 0, 0))
    return pl.pallas_call(
        functools.partial(_bias_tiles_kernel, dec_seq=dec_seq),
        grid=(N_ATT_HEADS,),
        in_specs=[smem, smem],
        out_specs=[big, big, small, small, small],
        out_shape=[jax.ShapeDtypeStruct((N_ATT_HEADS, MOBA_BLOCK, MOBA_BLOCK), F32)] * 2
        + [jax.ShapeDtypeStruct((N_ATT_HEADS, dec_seq, LANES), F32)] * 3,
        name="bias_tiles",
    )(rel_bias, ub)


def _rms(x, g):
    return x * lax.rsqrt(jnp.mean(x * x, axis=-1, keepdims=True) + RMS_EPS) * g


def _log_sigmoid(z):
    return jnp.minimum(z, 0.0) - jnp.log1p(jnp.exp(-jnp.abs(z)))


def _proj_kernel(x_ref, g_ref, w_ref, bg_ref, aq_ref, ak_ref, av_ref, mq_ref, mk_ref, mv_ref, mo_ref,
                 gate_ref, *kv_t_refs):
    xn = _rms(x_ref[...], g_ref[...])
    xb = xn.astype(BF16)
    outs = (aq_ref, ak_ref, av_ref, mq_ref, mk_ref, mv_ref, mo_ref)
    scales = (ATT_HEAD_DIM ** -0.5 * LOG2E, None, None, None, ML_HEAD_DIM ** -0.5, None, None)
    for i, (o_ref, s) in enumerate(zip(outs, scales)):
        r = jnp.dot(xb, w_ref[:, i * SLAB:(i + 1) * SLAB], preferred_element_type=F32)
        o_ref[...] = r if s is None else r * s
        if kv_t_refs and i in (1, 2):
            kv_t_refs[i - 1][0] = r.T
    gate_hi_lo = jnp.dot(xb, w_ref[:, GATE_COL0:GATE_COL0 + 2 * LANES], preferred_element_type=F32)
    x_lo = (xn - xb.astype(F32)).astype(BF16)
    z = gate_hi_lo[:, :LANES] + (gate_hi_lo[:, LANES:]
                                 + jnp.dot(x_lo, w_ref[:, GATE_COL0:GATE_COL0 + LANES], preferred_element_type=F32))
    z = z + bg_ref[...]
    lane = lax.broadcasted_iota(jnp.int32, z.shape, 1)
    gate_ref[...] = jnp.where((lane >= N_ML_HEADS) & (lane < 2 * N_ML_HEADS), _log_sigmoid(z), z)


def _project(x2, norm_g, w_all, b_gate, tm, seq):
    n = x2.shape[0]
    const = lambda shape: pl.BlockSpec(shape, lambda i: (0, 0), pipeline_mode=pl.Buffered(1))
    slab = pl.BlockSpec((tm, SLAB), lambda i: (i, 0))
    out_specs = [slab] * N_SLABS + [pl.BlockSpec((tm, LANES), lambda i: (i, 0))]
    out_shape = [jax.ShapeDtypeStruct((n, SLAB), F32)] * N_SLABS + [jax.ShapeDtypeStruct((n, LANES), F32)]
    if seq % tm == 0:
        tiles = seq // tm
        out_specs += [pl.BlockSpec((1, SLAB, tm), lambda i: (i // tiles, 0, i % tiles))] * 2
        out_shape += [jax.ShapeDtypeStruct((n // seq, SLAB, seq), F32)] * 2
    return pl.pallas_call(
        _proj_kernel,
        grid=(n // tm,),
        in_specs=[pl.BlockSpec((tm, D_MODEL), lambda i: (i, 0)), const((1, D_MODEL)),
                  const((D_MODEL, N_SLABS * SLAB + 2 * LANES)), const((1, LANES))],
        out_specs=out_specs,
        out_shape=out_shape,
        compiler_params=pltpu.CompilerParams(dimension_semantics=("parallel",), vmem_limit_bytes=VMEM_LIMIT),
        name="rms_in_proj",
    )(x2, norm_g, w_all, b_gate)


def _topk_penalty(scores, n_valid):
    lane = lax.broadcasted_iota(jnp.int32, scores.shape, 1)
    rank = jnp.zeros(scores.shape, jnp.int32)
    for n in range(n_valid):
        col = scores[:, n:n + 1]
        rank = rank + jnp.where((col > scores) | ((col == scores) & (lane > n)), 1, 0)
    return jnp.where((rank < MOBA_TOPK) & (lane < n_valid), 0.0, NEG)


def _moba_prompt_kernel(rb_ref, q_ref, k_ref, v_ref, t0_ref, t1_ref, o_ref, km_ref, kb_ref, vt_ref, sc_ref,
                        pen_ref, sp_ref, rad_ref, cen_ref, m_ref, *, n_blocks, group, heads):
    hp = pl.program_id(1)
    own = pl.program_id(2)
    blk = MOBA_BLOCK
    nbp = km_ref.shape[0]
    gl = heads * ATT_HEAD_DIM

    sel_l = lax.broadcasted_iota(jnp.int32, (gl, LANES), 0)
    sel_c = lax.broadcasted_iota(jnp.int32, (gl, LANES), 1)
    head_sel = jnp.where(sel_l // ATT_HEAD_DIM == sel_c, 1.0, 0.0).astype(BF16)

    @pl.when(own == 0)
    def _():
        km_ref[...] = jnp.zeros(km_ref.shape, F32)
        rad_ref[...] = jnp.zeros(rad_ref.shape, F32)
        cen_ref[...] = jnp.zeros(cen_ref.shape, F32)
        for n in range(n_blocks):
            kn = k_ref[0, n * blk:(n + 1) * blk, :]
            cn = jnp.mean(kn, axis=0, keepdims=True)
            km_ref[n:n + 1, :] = cn
            kb_ref[n] = kn.astype(BF16)
            dk = kn - cn
            rad_ref[n:n + 1, :] = jnp.max(jnp.dot((dk * dk).astype(BF16), head_sel, preferred_element_type=F32),
                                          axis=0, keepdims=True)
            cen_ref[n:n + 1, :] = jnp.dot(jnp.broadcast_to(cn * cn, (8, gl)).astype(BF16), head_sel,
                                          preferred_element_type=F32)[0:1]
            vtn = v_ref[0, n * blk:(n + 1) * blk, :].T.astype(BF16)
            for hh in range(heads):
                vt_ref[n, hh, 0:ATT_HEAD_DIM, :] = vtn[hh * ATT_HEAD_DIM:(hh + 1) * ATT_HEAD_DIM]
                vt_ref[n, hh, ATT_HEAD_DIM:, :] = jnp.ones((SUM_ROWS, blk), BF16)

    q2 = q_ref[0]
    lane = lax.broadcasted_iota(jnp.int32, q2.shape, 1)
    q_hi, q_lo = _split_bf16(q2, 2)
    qcat_b = jnp.concatenate([jnp.where((lane // ATT_HEAD_DIM) == hh, q_hi, jnp.zeros_like(q_hi))
                              for hh in range(heads)], axis=0)
    wide = heads * blk
    lane_m = lax.broadcasted_iota(jnp.int32, (nbp, gl), 1)
    km2 = jnp.concatenate([jnp.where((lane_m // ATT_HEAD_DIM) == hh, km_ref[...], 0.0) for hh in range(heads)], axis=0)
    km_hi, km_lo = _split_bf16(km2, 2)
    hi_q = _nt_dot(jnp.concatenate([km_hi, km_lo], axis=0), q_hi)
    sc2 = hi_q[:heads * nbp] + (hi_q[heads * nbp:] + _nt_dot(km_hi, q_lo))
    sc_ref[...] = jnp.concatenate([sc2[hh * nbp:(hh + 1) * nbp] for hh in range(heads)], axis=1)

    sub = lax.broadcasted_iota(jnp.int32, (nbp, wide), 0)

    def rank_body(n, rank):
        sc = sc_ref[...]
        row = sc_ref[pl.ds(n, 1), :]
        return rank + jnp.where((row > sc) | ((row == sc) & (sub > n)), 1, 0)

    rank = lax.fori_loop(0, own, rank_body, jnp.zeros((nbp, wide), jnp.int32))
    pen_ref[...] = jnp.where((rank < MOBA_TOPK) & (sub < own), 0.0, NEG)

    far_bias = jnp.concatenate([jnp.full((1, blk), rb_ref[NUM_BUCKETS - 1, heads * hp + hh] * LOG2E, F32)
                                for hh in range(heads)], axis=1)

    def fold8(x, op):
        return op(x.reshape(blk // 8, 8, x.shape[1]), axis=0)

    def far_row(j):
        return jnp.where(j < own - 1, pen_ref[pl.ds(j, 1), :] + far_bias, NEG)

    grp = group
    n_groups = (jnp.maximum(own - 1, 0) + grp - 1) // grp
    jn = jnp.maximum(own - 1, 0)
    near_row = jnp.where(own >= 1, pen_ref[pl.ds(jn, 1), :], NEG)

    t0c = jnp.concatenate([t0_ref[hh] for hh in range(heads)], axis=1)
    t1c = jnp.concatenate([t1_ref[hh] for hh in range(heads)], axis=1)
    no_prev = own == 0
    s_pair = _nt_dot(kb_ref[pl.ds(jn, 2)].reshape(2 * blk, gl), qcat_b)
    s_first = s_pair[:blk] + jnp.where(no_prev, t0c, t1c + near_row)
    s_second = jnp.where(no_prev, NEG, s_pair[blk:] + t0c)
    sp_ref[0:blk, :] = s_first
    sp_ref[blk:, :] = s_second
    m8 = jnp.maximum(fold8(s_first, jnp.max), fold8(s_second, jnp.max))
    pair_max = jnp.max(m8, axis=0, keepdims=True)

    row8 = lax.broadcasted_iota(jnp.int32, (8, gl), 0)
    lane8 = lax.broadcasted_iota(jnp.int32, (8, gl), 1)
    q_sq = _nt_dot(jnp.where(lane8 // ATT_HEAD_DIM == row8, 1.0, 0.0).astype(BF16),
                   (q2 * q2).astype(BF16))
    q_norm = jnp.sqrt(jnp.concatenate([q_sq[hh:hh + 1] for hh in range(heads)], axis=1))
    per_head = lambda x: jnp.concatenate([jnp.broadcast_to(x[:, hh:hh + 1], (nbp, blk)) for hh in range(heads)], axis=1)
    radius, centre = per_head(jnp.sqrt(rad_ref[...])), per_head(jnp.sqrt(cen_ref[...]))
    far_pen = jnp.where(sub < own - 1, pen_ref[...], NEG)
    mean_logit = sc_ref[...] + far_bias + far_pen
    spread = q_norm * (BOUND_INFLATE * radius + BF16_LOGIT_ERR * (centre + radius))
    far_low = jnp.max(mean_logit, axis=0, keepdims=True)
    far_high = jnp.max(mean_logit + spread, axis=0, keepdims=True)
    m_ref[...] = jnp.maximum(pair_max, far_high)
    slack = jnp.maximum(pair_max, far_high) - jnp.maximum(pair_max, far_low)

    @pl.when(jnp.max(slack) > MAX_SLACK)
    def _():
        def max_body(g, m8c):
            j0 = pl.multiple_of(g * grp, grp)
            sg = _nt_dot(kb_ref[pl.ds(j0, grp)].reshape(grp * blk, gl), qcat_b)
            for i in range(grp):
                m8c = jnp.maximum(m8c, fold8(sg[i * blk:(i + 1) * blk], jnp.max) + far_row(j0 + i))
            return m8c

        m_ref[...] = jnp.max(lax.fori_loop(0, n_groups, max_body, m8), axis=0, keepdims=True)

    m = m_ref[...]

    def accumulate(j, p, accs):
        pb = p.astype(BF16)
        return tuple(accs[hh] + jnp.dot(vt_ref[j, hh], pb[:, hh * blk:(hh + 1) * blk], preferred_element_type=F32)
                     for hh in range(heads))

    pb_pair = jnp.exp2(sp_ref[...] - m).astype(BF16)
    vt_pair = vt_ref[pl.ds(jn, 2)]
    accs = tuple(jnp.dot(jnp.concatenate([vt_pair[i, hh] for i in range(2)], axis=1),
                         pb_pair[:, hh * blk:(hh + 1) * blk], preferred_element_type=F32) for hh in range(heads))

    def exp_body(g, accc):
        j0 = pl.multiple_of(g * grp, grp)
        sg = _nt_dot(kb_ref[pl.ds(j0, grp)].reshape(grp * blk, gl), qcat_b)
        for i in range(grp):
            accc = accumulate(j0 + i, jnp.exp2(sg[i * blk:(i + 1) * blk] + (far_row(j0 + i) - m)), accc)
        return accc

    accs = lax.fori_loop(0, n_groups, exp_body, accs)
    out_t = jnp.concatenate([a[:ATT_HEAD_DIM] / a[ATT_HEAD_DIM:ATT_HEAD_DIM + 1] for a in accs], axis=0)
    o_ref[0] = out_t.T


def _moba_prompt(aq, ak, av, t0, t1, rel_bias, group, heads):
    b, s, _ = aq.shape
    nb = s // MOBA_BLOCK
    nbp = -(-nb // 8) * 8
    n_groups = N_ATT_HEADS // heads
    gl = heads * ATT_HEAD_DIM
    wide = heads * MOBA_BLOCK
    assert nb % group == 0
    qspec = pl.BlockSpec((1, MOBA_BLOCK, gl), lambda bi, hp, qi: (bi, qi, hp))
    kvspec = pl.BlockSpec((1, s, gl), lambda bi, hp, qi: (bi, 0, hp))
    tspec = pl.BlockSpec((heads, MOBA_BLOCK, MOBA_BLOCK), lambda bi, hp, qi: (hp, 0, 0))
    return pl.pallas_call(
        functools.partial(_moba_prompt_kernel, n_blocks=nb, group=group, heads=heads),
        grid=(b, n_groups, nb),
        in_specs=[pl.BlockSpec(memory_space=pltpu.SMEM), qspec, kvspec, kvspec, tspec, tspec],
        out_specs=qspec,
        out_shape=jax.ShapeDtypeStruct((b, s, ATT_WIDTH), F32),
        scratch_shapes=[pltpu.VMEM((nbp, gl), F32), pltpu.VMEM((nb, MOBA_BLOCK, gl), BF16),
                        pltpu.VMEM((nb, heads, ATT_HEAD_DIM + SUM_ROWS, MOBA_BLOCK), BF16),
                        pltpu.VMEM((nbp, wide), F32), pltpu.VMEM((nbp, wide), F32),
                        pltpu.VMEM((2 * MOBA_BLOCK, wide), F32), pltpu.VMEM((nbp, LANES), F32),
                        pltpu.VMEM((nbp, LANES), F32), pltpu.VMEM((1, wide), F32)],
        compiler_params=pltpu.CompilerParams(dimension_semantics=("parallel", "parallel", "arbitrary"),
                                             vmem_limit_bytes=VMEM_LIMIT),
        name="moba_prompt",
    )(rel_bias, aq, ak, av, t0, t1)


def _moba_sample_kernel(pt_ref, *refs, blocks_per_step, n_blocks, dec_seq):
    ppb = MOBA_BLOCK // PAGE_SIZE
    pps = blocks_per_step * ppb
    k_pages, v_pages = refs[:pps], refs[pps:2 * pps]
    (q_ref, kn_ref, vn_ref, s_own_ref, s_near_ref, s_far_ref, o_ref,
     qbd_ref, kmt_ref, mm_ref, ll_ref, oo_ref, own_m_ref, own_l_ref, own_o_ref) = refs[2 * pps:]
    c = pl.program_id(1)
    rows = N_ATT_HEADS * dec_seq
    row_head = lax.broadcasted_iota(jnp.int32, (rows, ATT_WIDTH), 0) // dec_seq
    lane_head = lax.broadcasted_iota(jnp.int32, (rows, ATT_WIDTH), 1) // ATT_HEAD_DIM
    lane_r = lax.broadcasted_iota(jnp.int32, (rows, LANES), 1)
    lane_k = lax.broadcasted_iota(jnp.int32, (ATT_WIDTH, LANES), 1)

    def local_softmax(s):
        m = jnp.max(s, axis=-1, keepdims=True)
        p = jnp.exp2(s - m)
        return m, jnp.sum(p, axis=-1, keepdims=True), p.astype(BF16)

    @pl.when(c == 0)
    def _():
        qrep = jnp.concatenate([q_ref[0]] * N_ATT_HEADS, axis=0)
        qbd = jnp.where(row_head == lane_head, qrep, 0.0)
        qbd_ref[...] = qbd
        kmt_ref[...] = jnp.zeros(kmt_ref.shape, F32)
        mm_ref[...] = jnp.zeros(mm_ref.shape, F32)
        ll_ref[...] = jnp.zeros(ll_ref.shape, F32)
        pad = jnp.zeros((LANES - dec_seq, ATT_WIDTH), F32)
        kn = jnp.concatenate([kn_ref[0], pad], axis=0).astype(BF16)
        vn = jnp.concatenate([vn_ref[0], pad], axis=0).astype(BF16)
        m, l, p = local_softmax(_nt_dot(qbd.astype(BF16), kn) + s_own_ref[...].reshape(rows, LANES))
        own_m_ref[...] = jnp.broadcast_to(m, (rows, LANES))
        own_l_ref[...] = jnp.broadcast_to(l, (rows, LANES))
        own_o_ref[...] = jnp.dot(p, vn, preferred_element_type=F32)

    qb = qbd_ref[...].astype(BF16)
    s_near = s_near_ref[...].reshape(rows, LANES)
    s_far = s_far_ref[...].reshape(rows, LANES)
    n0 = c * blocks_per_step
    kt = jnp.concatenate([k_pages[i][0] for i in range(pps)], axis=1)
    s_all = jnp.dot(qb, kt.astype(BF16), preferred_element_type=F32)
    mm, ll, kmt = mm_ref[...], ll_ref[...], kmt_ref[...]
    for g in range(blocks_per_step):
        n = n0 + g
        bias = jnp.concatenate([s_far] * (ppb - 1) + [jnp.where(n == n_blocks - 1, s_near, s_far)], axis=1)
        m, l, p = local_softmax(s_all[:, g * MOBA_BLOCK:(g + 1) * MOBA_BLOCK] + bias)
        vt = jnp.concatenate([v_pages[ppb * g + i][0] for i in range(ppb)], axis=1).astype(BF16)
        oo_ref[n] = _nt_dot(p, vt)
        mm = jnp.where(lane_r == n, m, mm)
        ll = jnp.where(lane_r == n, l, ll)
        ksum = jnp.sum(kt[:, g * MOBA_BLOCK:(g + 1) * MOBA_BLOCK], axis=-1, keepdims=True)
        kmt = jnp.where(lane_k == n, ksum, kmt)
    mm_ref[...] = mm
    ll_ref[...] = ll
    kmt_ref[...] = kmt

    @pl.when(c == pl.num_programs(1) - 1)
    def _():
        kmean_t = kmt_ref[...] * (1.0 / MOBA_BLOCK)
        pen = _topk_penalty(_dot_x3(qbd_ref[...], kmean_t, ((1,), (0,))), n_blocks)
        mp = mm_ref[...] + pen
        m_own = own_m_ref[...]
        m_tot = jnp.maximum(jnp.max(mp, axis=-1, keepdims=True), m_own)
        w = jnp.exp2(mp - m_tot)
        w_own = jnp.exp2(m_own - m_tot)
        den = jnp.sum(w * ll_ref[...], axis=-1, keepdims=True) + w_own * own_l_ref[...]

        num = w_own[:, 0:1] * own_o_ref[...]
        for nn in range(n_blocks):
            num = num + w[:, nn:nn + 1] * oo_ref[nn]
        full = jnp.where(row_head == lane_head, num / den[:, 0:1], 0.0)
        out = full[0:dec_seq]
        for h in range(1, N_ATT_HEADS):
            out = out + full[h * dec_seq:(h + 1) * dec_seq]
        o_ref[0] = out


def _moba_sample(aq, ak, av, pool_kt, pool_vt, page_table, s_own, s_near, s_far, blocks_per_step):
    db, dec_seq, _ = aq.shape
    n_pages = page_table.shape[1]
    ppb = MOBA_BLOCK // PAGE_SIZE
    nb = n_pages // ppb
    pps = blocks_per_step * ppb
    assert nb % blocks_per_step == 0 and nb <= LANES
    rows = N_ATT_HEADS * dec_seq

    def page_spec(i):
        return pl.BlockSpec((1, ATT_WIDTH, PAGE_SIZE), lambda b, c, pt: (pt[b, c * pps + i], 0, 0))

    per_b = pl.BlockSpec((1, dec_seq, ATT_WIDTH), lambda b, c, pt: (b, 0, 0))
    tile = pl.BlockSpec((N_ATT_HEADS, dec_seq, LANES), lambda b, c, pt: (0, 0, 0))
    grid_spec = pltpu.PrefetchScalarGridSpec(
        num_scalar_prefetch=1, grid=(db, nb // blocks_per_step),
        in_specs=[page_spec(i) for i in range(pps)] * 2 + [per_b, per_b, per_b, tile, tile, tile],
        out_specs=per_b,
        scratch_shapes=[pltpu.VMEM((rows, ATT_WIDTH), F32), pltpu.VMEM((ATT_WIDTH, LANES), F32),
                        pltpu.VMEM((rows, LANES), F32), pltpu.VMEM((rows, LANES), F32),
                        pltpu.VMEM((nb, rows, ATT_WIDTH), F32), pltpu.VMEM((rows, LANES), F32),
                        pltpu.VMEM((rows, LANES), F32), pltpu.VMEM((rows, ATT_WIDTH), F32)])
    return pl.pallas_call(
        functools.partial(_moba_sample_kernel, blocks_per_step=blocks_per_step, n_blocks=nb, dec_seq=dec_seq),
        grid_spec=grid_spec,
        out_shape=jax.ShapeDtypeStruct((db, dec_seq, ATT_WIDTH), F32),
        compiler_params=pltpu.CompilerParams(dimension_semantics=("parallel", "arbitrary"),
                                             vmem_limit_bytes=VMEM_LIMIT),
        name="moba_sample",
    )(page_table, *([pool_kt] * pps), *([pool_vt] * pps), aq, ak, av, s_own, s_near, s_far)


def _mlstm_kernel(q_ref, k_ref, v_ref, g_ref, mo_ref, nw_ref, c0_ref, n0_ref, m0_ref,
                  h_ref, c_out, n_out, m_out, c_s, n_s, m_s, *, rows, chunk):
    ci = pl.program_id(1)
    L = chunk

    @pl.when(ci == 0)
    def _():
        c_s[...] = c0_ref[0]
        n_s[...] = n0_ref[0]
        m_s[...] = m0_ref[0]

    def padded(ref, fill=None):
        x = ref[0]
        if rows == L:
            return x
        tail = jnp.zeros((L - rows, x.shape[1]), F32) if fill is None else fill
        return jnp.concatenate([x, tail], axis=0)

    lane = lax.broadcasted_iota(jnp.int32, (L, LANES), 1)
    pad_lane = lax.broadcasted_iota(jnp.int32, (max(L - rows, 1), LANES), 1)
    g = padded(g_ref, jnp.where(pad_lane < N_ML_HEADS, NEG, 0.0))
    q_all, k_all, v_all, mo_all = padded(q_ref), padded(k_ref), padded(v_ref), padded(mo_ref)

    is_lf = (lane >= N_ML_HEADS) & (lane < 2 * N_ML_HEADS)
    tri_r = lax.broadcasted_iota(jnp.int32, (L, L), 0)
    tri_c = lax.broadcasted_iota(jnp.int32, (L, L), 1)
    causal = tri_c <= tri_r
    g3 = jnp.concatenate(_split_bf16(jnp.where(is_lf, g, 0.0), 3), axis=1)
    b3 = jnp.dot(causal.astype(BF16), g3, preferred_element_type=F32)
    bcum = b3[:, :LANES] + (b3[:, LANES:2 * LANES] + b3[:, 2 * LANES:])
    g_t = g.T
    b_t = bcum.T

    ones_b = jnp.ones((L, LANES), BF16)
    reps = L // LANES
    wide = lambda x: x if reps == 1 else jnp.concatenate([x] * reps, axis=1)
    for h in range(N_ML_HEADS):
        sl = slice(h * ML_HEAD_DIM, (h + 1) * ML_HEAD_DIM)
        q, k, v = q_all[:, sl], k_all[:, sl], v_all[:, sl]
        ig = jnp.broadcast_to(g[:, h:h + 1], (L, LANES))
        b = jnp.broadcast_to(bcum[:, N_ML_HEADS + h:N_ML_HEADS + h + 1], (L, LANES))
        ig_row = g_t[h:h + 1, :]
        b_row = b_t[N_ML_HEADS + h:N_ML_HEADS + h + 1, :]
        m_prev = m_s[h]
        C = c_s[h]
        n_rep = n_s[h]

        a = b + m_prev
        log_d = jnp.where(causal, wide(b) - b_row + ig_row, NEG)
        m_t = jnp.maximum(a, jnp.max(log_d, axis=-1, keepdims=True))
        d = jnp.exp(log_d - wide(m_t))
        w_state = jnp.exp(a - m_t)
        qb, kb, vb = q.astype(BF16), k.astype(BF16), v.astype(BF16)
        s = _nt_dot(qb, kb) * d
        v_ones = jnp.concatenate([vb, ones_b], axis=1)
        from_state = jnp.dot(qb, jnp.concatenate([C, n_rep], axis=1).astype(BF16), preferred_element_type=F32)
        from_chunk = jnp.dot(s.astype(BF16), v_ones, preferred_element_type=F32)
        num = w_state * from_state[:, :ML_HEAD_DIM] + from_chunk[:, :ML_HEAD_DIM]
        den = w_state * from_state[:, ML_HEAD_DIM:] + from_chunk[:, ML_HEAD_DIM:]
        hid = num / jnp.maximum(jnp.abs(den), jnp.exp(-m_t))

        hn = _rms(hid, nw_ref[:, sl])
        h_ref[0, :, sl] = (hn * jax.nn.sigmoid(mo_all[:, sl]))[:rows]

        b_last = b[L - 1:L, :]
        m_new = m_t[L - 1:L, :]
        g_state = jnp.exp(b_last + m_prev - m_new)
        kg = k * jnp.exp(b_last - b + ig - m_new)
        added = _tn_dot(kg.astype(BF16), v_ones)
        c_s[h] = g_state * C + added[:, :ML_HEAD_DIM]
        n_s[h] = g_state * n_rep + added[:, ML_HEAD_DIM:]
        m_s[h] = m_new

    @pl.when(ci == pl.num_programs(1) - 1)
    def _():
        c_out[0] = c_s[...]
        n_out[0] = n_s[...]
        m_out[0] = m_s[...]


def _mlstm(mq, mk, mv, gates, mo, ml_norm, c0, n0, m0, chunk):
    b, s, _ = mq.shape
    rows = min(s, chunk)
    n_chunks = s // rows
    tok = lambda w: pl.BlockSpec((1, rows, w), lambda bi, ci: (bi, ci, 0))
    st4 = lambda shape: pl.BlockSpec(shape, lambda bi, ci: (bi, 0, 0, 0))
    c_shape, m_shape = (1, N_ML_HEADS, ML_HEAD_DIM, ML_HEAD_DIM), (1, N_ML_HEADS, 1, LANES)
    n_shape = (1, N_ML_HEADS, ML_HEAD_DIM, LANES)
    return pl.pallas_call(
        functools.partial(_mlstm_kernel, rows=rows, chunk=chunk),
        grid=(b, n_chunks),
        in_specs=[tok(ML_WIDTH), tok(ML_WIDTH), tok(ML_WIDTH), tok(LANES), tok(ML_WIDTH),
                  pl.BlockSpec((1, ML_WIDTH), lambda bi, ci: (0, 0)),
                  st4(c_shape), st4(n_shape), st4(m_shape)],
        out_specs=[tok(ML_WIDTH), st4(c_shape), st4(n_shape), st4(m_shape)],
        out_shape=[jax.ShapeDtypeStruct((b, s, ML_WIDTH), F32),
                   jax.ShapeDtypeStruct((b,) + c_shape[1:], F32),
                   jax.ShapeDtypeStruct((b,) + n_shape[1:], F32),
                   jax.ShapeDtypeStruct((b,) + m_shape[1:], F32)],
        scratch_shapes=[pltpu.VMEM(c_shape[1:], F32), pltpu.VMEM(n_shape[1:], F32), pltpu.VMEM(m_shape[1:], F32)],
        compiler_params=pltpu.CompilerParams(dimension_semantics=("parallel", "arbitrary"),
                                             vmem_limit_bytes=VMEM_LIMIT),
        name="mlstm",
    )(mq, mk, mv, gates, mo, ml_norm, c0, n0, m0)


def _out_ffn_kernel(x_ref, att_ref, ml_ref, wo_ref, gf_ref, wu_ref, wd_ref, gl_ref, y_ref, *, ff_chunk):
    x1 = x_ref[...] \
        + jnp.dot(att_ref[...].astype(BF16), wo_ref[0:ATT_WIDTH, :], preferred_element_type=F32) \
        + jnp.dot(ml_ref[...].astype(BF16), wo_ref[ATT_WIDTH:, :], preferred_element_type=F32)
    xb = _rms(x1, gf_ref[...]).astype(BF16)
    ffn = None
    for c in range(D_FF // ff_chunk):
        sl = slice(c * ff_chunk, (c + 1) * ff_chunk)
        u = jnp.maximum(jnp.dot(xb, wu_ref[:, sl], preferred_element_type=F32), 0.0)
        d = jnp.dot((u * u).astype(BF16), wd_ref[sl, :], preferred_element_type=F32)
        ffn = d if ffn is None else ffn + d
    y_ref[...] = _rms(x1 + ffn, gl_ref[...])


def _out_ffn(x2, att, ml, w_out, norm_ffn, w_up, w_down, norm_final, tm, ff_chunk):
    n = x2.shape[0]
    const = lambda shape: pl.BlockSpec(shape, lambda i: (0, 0), pipeline_mode=pl.Buffered(1))
    row = lambda w: pl.BlockSpec((tm, w), lambda i: (i, 0))
    return pl.pallas_call(
        functools.partial(_out_ffn_kernel, ff_chunk=ff_chunk),
        grid=(n // tm,),
        in_specs=[row(D_MODEL), row(ATT_WIDTH), row(ML_WIDTH), const((D_MODEL, D_MODEL)), const((1, D_MODEL)),
                  const((D_MODEL, D_FF)), const((D_FF, D_MODEL)), const((1, D_MODEL))],
        out_specs=row(D_MODEL),
        out_shape=jax.ShapeDtypeStruct((n, D_MODEL), F32),
        compiler_params=pltpu.CompilerParams(dimension_semantics=("parallel",), vmem_limit_bytes=VMEM_LIMIT),
        name="out_proj_ffn",
    )(x2, att, ml, w_out, norm_ffn, w_up, w_down, norm_final)


def _layer_weights(l, norm_mix, w_in, b_ig, b_fg, ml_norm, w_out, norm_ffn, w_up, w_down):
    w_gate = jnp.pad(w_in[l, :, GATE_COL0:], ((0, 0), (0, LANES - 2 * N_ML_HEADS)))
    w_all = jnp.concatenate([w_in[l, :, :GATE_COL0].astype(BF16)] + _split_bf16(w_gate, 2), axis=1)
    b_gate = jnp.pad(jnp.concatenate([b_ig[l], b_fg[l]]), (0, LANES - 2 * N_ML_HEADS))[None, :]
    return dict(norm_mix=norm_mix[l][None, :], w_all=w_all, b_gate=b_gate,
                ml_norm=ml_norm[l][None, :], w_out=w_out[l].astype(BF16), norm_ffn=norm_ffn[l][None, :],
                w_up=w_up[l].astype(BF16), w_down=w_down[l].astype(BF16))


def _trunk_layer(x, w, attend, state0, norm_final, tiles):
    b, s, _ = x.shape
    x2 = x.reshape(b * s, D_MODEL)
    aq, ak, av, mq, mk, mv, mo, gates, *kv_t = _project(x2, w["norm_mix"], w["w_all"], w["b_gate"], tiles.tm, s)
    r3 = lambda a: a.reshape(b, s, a.shape[-1])
    if kv_t:
        k_out, v_out = (a.reshape(b, N_ATT_HEADS, ATT_HEAD_DIM, s).transpose(0, 3, 1, 2) for a in kv_t)
    else:
        k_out, v_out = (a.reshape(b, s, N_ATT_HEADS, ATT_HEAD_DIM) for a in (ak, av))
    att = attend(r3(aq), r3(ak), r3(av))
    c0, n0, m0 = state0
    ml, c_new, n_new, m_new = _mlstm(r3(mq), r3(mk), r3(mv), r3(gates), r3(mo), w["ml_norm"], c0, n0, m0, tiles.ml_chunk)
    y = _out_ffn(x2, att.reshape(b * s, ATT_WIDTH), ml.reshape(b * s, ML_WIDTH), w["w_out"], w["norm_ffn"],
                 w["w_up"], w["w_down"], norm_final, tiles.tm, tiles.ff_chunk)
    return y.reshape(b, s, D_MODEL), k_out, v_out, c_new, n_new[:, :, :, 0], m_new[:, :, 0, 0]


def kernel(x_prompt, x_sample, cache_k, cache_v, state_C, state_n, state_m, page_table, rel_bias, norm_mix, w_in,
           b_ig, b_fg, ml_norm, w_out, norm_ffn, w_up, w_down, norm_final):
    depth = w_in.shape[0]
    assert depth == 1, "the fused final RMSNorm assumes a single layer"
    bp, sp, _ = x_prompt.shape
    db, dec_seq, _ = x_sample.shape
    n_phys = cache_k.shape[1]
    t0, t1, s_own, s_near, s_far = _bias_tiles(rel_bias, dec_seq)
    gl = norm_final[None, :]
    l = 0
    w = _layer_weights(l, norm_mix, w_in, b_ig, b_fg, ml_norm, w_out, norm_ffn, w_up, w_down)

    zero_state = (jnp.zeros((bp, N_ML_HEADS, ML_HEAD_DIM, ML_HEAD_DIM), F32),
                  jnp.zeros((bp, N_ML_HEADS, ML_HEAD_DIM, LANES), F32), jnp.zeros((bp, N_ML_HEADS, 1, LANES), F32))
    yp, kp, vp, cp, np_, mp = _trunk_layer(
        x_prompt, w, lambda q, k, v: _moba_prompt(q, k, v, t0, t1, rel_bias, PROMPT_KEY_GROUP, PROMPT_HEADS_PER_STEP),
        zero_state, gl,
        PROMPT_TILES)

    to_pool = lambda a: jnp.transpose(a, (0, 1, 3, 4, 2)).reshape(depth * n_phys, ATT_WIDTH, PAGE_SIZE)
    pool_kt, pool_vt = to_pool(cache_k), to_pool(cache_v)
    pt = page_table + l * n_phys

    def attend_sample(q, k, v):
        return _moba_sample(q, k, v, pool_kt, pool_vt, pt, s_own, s_near, s_far, SAMPLE_BLOCKS_PER_STEP)

    state0 = (state_C[l], jnp.broadcast_to(state_n[l][:, :, :, None], (db, N_ML_HEADS, ML_HEAD_DIM, LANES)),
              jnp.broadcast_to(state_m[l][:, :, None, None], (db, N_ML_HEADS, 1, LANES)))
    ys, ks, vs, cs, ns, ms = _trunk_layer(x_sample, w, attend_sample, state0, gl, SAMPLE_TILES)

    st = lambda a: a[None]
    return (yp, ys, st(kp), st(vp), st(cp), st(np_), st(mp), st(ks), st(vs), st(cs), st(ns), st(ms))
```

```python
import functools
import math
from typing import NamedTuple

import numpy as np
import jax
import jax.numpy as jnp
from jax import lax
from jax.experimental import pallas as pl
from jax.experimental.pallas import tpu as pltpu

F32 = jnp.float32
BF16 = jnp.bfloat16

D_MODEL = 1024
ATT_HEAD_DIM = 64
N_ATT_HEADS = 8
ATT_WIDTH = 512
N_ML_HEADS = 4
ML_HEAD_DIM = 128
ML_WIDTH = 512
SLAB = 512
N_SLABS = 7
GATE_COL0 = N_SLABS * SLAB
MOBA_BLOCK = 256
MOBA_TOPK = 3
PAGE_SIZE = 128
NUM_BUCKETS = 32
MAX_DISTANCE = 128
D_FF = 4 * D_MODEL
RMS_EPS = 1e-6
NEG = -1e30
LOG2E = math.log2(math.e)
LANES = 128
SUM_ROWS = 16
MAX_SLACK = 64.0
BOUND_INFLATE = 1.05
BF16_LOGIT_ERR = 2.0 ** -6
VMEM_LIMIT = 56 * 1024 * 1024


class _Tiles(NamedTuple):
    tm: int
    ml_chunk: int
    ml_batch_rows: int
    ff_chunk: int = 1024


PROMPT_TILES = _Tiles(tm=512, ml_chunk=512, ml_batch_rows=4)
SAMPLE_TILES = _Tiles(tm=256, ml_chunk=128, ml_batch_rows=4)
PROMPT_KEY_GROUP = 4
PROMPT_HEADS_PER_STEP = 4
SAMPLE_BLOCKS_PER_STEP = 16


def _nt_dot(a, b):
    return lax.dot_general(a, b, (((1,), (1,)), ((), ())), preferred_element_type=F32)


def _split_bf16(x, parts):
    out = []
    for _ in range(parts):
        p = x.astype(BF16)
        out.append(p)
        x = x - p.astype(F32)
    return out


def _dot_x3(a, b, contract):
    (ah, al), (bh, bl) = _split_bf16(a, 2), _split_bf16(b, 2)
    d = lambda x, y: lax.dot_general(x, y, (contract, ((), ())), preferred_element_type=F32)
    return d(ah, bh) + (d(ah, bl) + d(al, bh))


def _tn_dot(a, b):
    return lax.dot_general(a, b, (((0,), (0,)), ((), ())), preferred_element_type=F32)


def _bucket_upper_bounds():
    n = np.arange(0, MAX_DISTANCE + 1)
    max_exact = NUM_BUCKETS // 2
    nf = np.maximum(n, 1).astype(np.float32)
    large = max_exact + (np.log(nf / np.float32(max_exact)) / np.float32(math.log(MAX_DISTANCE / max_exact))
                         * np.float32(NUM_BUCKETS - max_exact)).astype(np.int32)
    large = np.minimum(large, NUM_BUCKETS - 1)
    bucket = np.where(n < max_exact, n, large)
    return np.array([int(n[bucket <= b].max()) for b in range(NUM_BUCKETS - 1)], dtype=np.int32)


def _bias_of_dist(dist, h, rb_ref, ub_ref):
    def body(i, acc):
        b = NUM_BUCKETS - 2 - i
        return jnp.where(dist <= ub_ref[b], rb_ref[b, h], acc)
    init = jnp.full(dist.shape, rb_ref[NUM_BUCKETS - 1, h], F32)
    return lax.fori_loop(0, NUM_BUCKETS - 1, body, init)


def _bias_tiles_kernel(rb_ref, ub_ref, t0_ref, t1_ref, s_own_ref, s_near_ref, s_far_ref, *, dec_seq):
    h = pl.program_id(0)
    kk = lax.broadcasted_iota(jnp.int32, (MOBA_BLOCK, MOBA_BLOCK), 0)
    qq = lax.broadcasted_iota(jnp.int32, (MOBA_BLOCK, MOBA_BLOCK), 1)
    d0 = qq - kk
    t0_ref[0] = jnp.where(d0 >= 0, _bias_of_dist(jnp.maximum(d0, 0), h, rb_ref, ub_ref) * LOG2E, NEG)
    t1_ref[0] = _bias_of_dist(d0 + MOBA_BLOCK, h, rb_ref, ub_ref) * LOG2E
    t = lax.broadcasted_iota(jnp.int32, (dec_seq, LANES), 0)
    cc = lax.broadcasted_iota(jnp.int32, (dec_seq, LANES), 1)
    ds = t - cc
    s_own_ref[0] = jnp.where((ds >= 0) & (cc < dec_seq),
                             _bias_of_dist(jnp.maximum(ds, 0), h, rb_ref, ub_ref) * LOG2E, NEG)
    s_near_ref[0] = _bias_of_dist(ds + PAGE_SIZE, h, rb_ref, ub_ref) * LOG2E
    s_far_ref[0] = jnp.full((dec_seq, LANES), rb_ref[NUM_BUCKETS - 1, h] * LOG2E, F32)


def _bias_tiles(rel_bias, dec_seq):
    ub = jnp.asarray(_bucket_upper_bounds())
    smem = pl.BlockSpec(memory_space=pltpu.SMEM)
    big = pl.BlockSpec((1, MOBA_BLOCK, MOBA_BLOCK), lambda h: (h, 0, 0))
    small = pl.BlockSpec((1, dec_seq, LANES), lambda h: (h, 0, 0))
    return pl.pallas_call(
        functools.partial(_bias_tiles_kernel, dec_seq=dec_seq),
        grid=(N_ATT_HEADS,),
        in_specs=[smem, smem],
        out_specs=[big, big, small, small, small],
        out_shape=[jax.ShapeDtypeStruct((N_ATT_HEADS, MOBA_BLOCK, MOBA_BLOCK), F32)] * 2
        + [jax.ShapeDtypeStruct((N_ATT_HEADS, dec_seq, LANES), F32)] * 3,
        name="bias_tiles",
    )(rel_bias, ub)


def _rms(x, g):
    return x * lax.rsqrt(jnp.mean(x * x, axis=-1, keepdims=True) + RMS_EPS) * g


def _log_sigmoid(z):
    return jnp.minimum(z, 0.0) - jnp.log1p(jnp.exp(-jnp.abs(z)))


def _proj_kernel(x_ref, g_ref, w_ref, bg_ref, aq_ref, ak_ref, av_ref, mq_ref, mk_ref, mv_ref, mo_ref,
                 gate_ref, *kv_t_refs):
    xn = _rms(x_ref[...], g_ref[...])
    xb = xn.astype(BF16)
    outs = (aq_ref, ak_ref, av_ref, mq_ref, mk_ref, mv_ref, mo_ref)
    scales = (ATT_HEAD_DIM ** -0.5 * LOG2E, None, None, None, ML_HEAD_DIM ** -0.5, None, None)
    for i, (o_ref, s) in enumerate(zip(outs, scales)):
        r = jnp.dot(xb, w_ref[:, i * SLAB:(i + 1) * SLAB], preferred_element_type=F32)
        o_ref[...] = r if s is None else r * s
        if kv_t_refs and i in (1, 2):
            kv_t_refs[i - 1][0] = r.T
    gate_hi_lo = jnp.dot(xb, w_ref[:, GATE_COL0:GATE_COL0 + 2 * LANES], preferred_element_type=F32)
    x_lo = (xn - xb.astype(F32)).astype(BF16)
    z = gate_hi_lo[:, :LANES] + (gate_hi_lo[:, LANES:]
                                 + jnp.dot(x_lo, w_ref[:, GATE_COL0:GATE_COL0 + LANES], preferred_element_type=F32))
    z = z + bg_ref[...]
    lane = lax.broadcasted_iota(jnp.int32, z.shape, 1)
    gate_ref[...] = jnp.where((lane >= N_ML_HEADS) & (lane < 2 * N_ML_HEADS), _log_sigmoid(z), z)


def _project(x2, norm_g, w_all, b_gate, tm, seq):
    n = x2.shape[0]
    const = lambda shape: pl.BlockSpec(shape, lambda i: (0, 0), pipeline_mode=pl.Buffered(1))
    slab = pl.BlockSpec((tm, SLAB), lambda i: (i, 0))
    out_specs = [slab] * N_SLABS + [pl.BlockSpec((tm, LANES), lambda i: (i, 0))]
    out_shape = [jax.ShapeDtypeStruct((n, SLAB), F32)] * N_SLABS + [jax.ShapeDtypeStruct((n, LANES), F32)]
    if seq % tm == 0:
        tiles = seq // tm
        out_specs += [pl.BlockSpec((1, SLAB, tm), lambda i: (i // tiles, 0, i % tiles))] * 2
        out_shape += [jax.ShapeDtypeStruct((n // seq, SLAB, seq), F32)] * 2
    return pl.pallas_call(
        _proj_kernel,
        grid=(n // tm,),
        in_specs=[pl.BlockSpec((tm, D_MODEL), lambda i: (i, 0)), const((1, D_MODEL)),
                  const((D_MODEL, N_SLABS * SLAB + 2 * LANES)), const((1, LANES))],
        out_specs=out_specs,
        out_shape=out_shape,
        compiler_params=pltpu.CompilerParams(dimension_semantics=("parallel",), vmem_limit_bytes=VMEM_LIMIT),
        name="rms_in_proj",
    )(x2, norm_g, w_all, b_gate)


def _topk_penalty(scores, n_valid):
    lane = lax.broadcasted_iota(jnp.int32, scores.shape, 1)
    rank = jnp.zeros(scores.shape, jnp.int32)
    for n in range(n_valid):
        col = scores[:, n:n + 1]
        rank = rank + jnp.where((col > scores) | ((col == scores) & (lane > n)), 1, 0)
    return jnp.where((rank < MOBA_TOPK) & (lane < n_valid), 0.0, NEG)


def _moba_prompt_kernel(rb_ref, q_ref, k_ref, v_ref, t0_ref, t1_ref, o_ref, km_ref, kb_ref, vt_ref, sc_ref,
                        pen_ref, sp_ref, rad_ref, cen_ref, m_ref, *, n_blocks, group, heads):
    hp = pl.program_id(1)
    own = pl.program_id(2)
    blk = MOBA_BLOCK
    nbp = km_ref.shape[0]
    gl = heads * ATT_HEAD_DIM

    sel_l = lax.broadcasted_iota(jnp.int32, (gl, LANES), 0)
    sel_c = lax.broadcasted_iota(jnp.int32, (gl, LANES), 1)
    head_sel = jnp.where(sel_l // ATT_HEAD_DIM == sel_c, 1.0, 0.0).astype(BF16)

    @pl.when(own == 0)
    def _():
        km_ref[...] = jnp.zeros(km_ref.shape, F32)
        rad_ref[...] = jnp.zeros(rad_ref.shape, F32)
        cen_ref[...] = jnp.zeros(cen_ref.shape, F32)
        for n in range(n_blocks):
            kn = k_ref[0, n * blk:(n + 1) * blk, :]
            cn = jnp.mean(kn, axis=0, keepdims=True)
            km_ref[n:n + 1, :] = cn
            kb_ref[n] = kn.astype(BF16)
            dk = kn - cn
            rad_ref[n:n + 1, :] = jnp.max(jnp.dot((dk * dk).astype(BF16), head_sel, preferred_element_type=F32),
                                          axis=0, keepdims=True)
            cen_ref[n:n + 1, :] = jnp.dot(jnp.broadcast_to(cn * cn, (8, gl)).astype(BF16), head_sel,
                                          preferred_element_type=F32)[0:1]
            vtn = v_ref[0, n * blk:(n + 1) * blk, :].T.astype(BF16)
            for hh in range(heads):
                vt_ref[n, hh, 0:ATT_HEAD_DIM, :] = vtn[hh * ATT_HEAD_DIM:(hh + 1) * ATT_HEAD_DIM]
                vt_ref[n, hh, ATT_HEAD_DIM:, :] = jnp.ones((SUM_ROWS, blk), BF16)

    q2 = q_ref[0]
    lane = lax.broadcasted_iota(jnp.int32, q2.shape, 1)
    q_hi, q_lo = _split_bf16(q2, 2)
    qcat_b = jnp.concatenate([jnp.where((lane // ATT_HEAD_DIM) == hh, q_hi, jnp.zeros_like(q_hi))
                              for hh in range(heads)], axis=0)
    wide = heads * blk
    lane_m = lax.broadcasted_iota(jnp.int32, (nbp, gl), 1)
    km2 = jnp.concatenate([jnp.where((lane_m // ATT_HEAD_DIM) == hh, km_ref[...], 0.0) for hh in range(heads)], axis=0)
    km_hi, km_lo = _split_bf16(km2, 2)
    hi_q = _nt_dot(jnp.concatenate([km_hi, km_lo], axis=0), q_hi)
    sc2 = hi_q[:heads * nbp] + (hi_q[heads * nbp:] + _nt_dot(km_hi, q_lo))
    sc_ref[...] = jnp.concatenate([sc2[hh * nbp:(hh + 1) * nbp] for hh in range(heads)], axis=1)

    sub = lax.broadcasted_iota(jnp.int32, (nbp, wide), 0)

    def rank_body(n, rank):
        sc = sc_ref[...]
        row = sc_ref[pl.ds(n, 1), :]
        return rank + jnp.where((row > sc) | ((row == sc) & (sub > n)), 1, 0)

    rank = lax.fori_loop(0, own, rank_body, jnp.zeros((nbp, wide), jnp.int32))
    pen_ref[...] = jnp.where((rank < MOBA_TOPK) & (sub < own), 0.0, NEG)

    far_bias = jnp.concatenate([jnp.full((1, blk), rb_ref[NUM_BUCKETS - 1, heads * hp + hh] * LOG2E, F32)
                                for hh in range(heads)], axis=1)

    def fold8(x, op):
        return op(x.reshape(blk // 8, 8, x.shape[1]), axis=0)

    def far_row(j):
        return jnp.where(j < own - 1, pen_ref[pl.ds(j, 1), :] + far_bias, NEG)

    grp = group
    n_groups = (jnp.maximum(own - 1, 0) + grp - 1) // grp
    jn = jnp.maximum(own - 1, 0)
    near_row = jnp.where(own >= 1, pen_ref[pl.ds(jn, 1), :], NEG)

    t0c = jnp.concatenate([t0_ref[hh] for hh in range(heads)], axis=1)
    t1c = jnp.concatenate([t1_ref[hh] for hh in range(heads)], axis=1)
    no_prev = own == 0
    s_pair = _nt_dot(kb_ref[pl.ds(jn, 2)].reshape(2 * blk, gl), qcat_b)
    s_first = s_pair[:blk] + jnp.where(no_prev, t0c, t1c + near_row)
    s_second = jnp.where(no_prev, NEG, s_pair[blk:] + t0c)
    sp_ref[0:blk, :] = s_first
    sp_ref[blk:, :] = s_second
    m8 = jnp.maximum(fold8(s_first, jnp.max), fold8(s_second, jnp.max))
    pair_max = jnp.max(m8, axis=0, keepdims=True)

    row8 = lax.broadcasted_iota(jnp.int32, (8, gl), 0)
    lane8 = lax.broadcasted_iota(jnp.int32, (8, gl), 1)
    q_sq = _nt_dot(jnp.where(lane8 // ATT_HEAD_DIM == row8, 1.0, 0.0).astype(BF16),
                   (q2 * q2).astype(BF16))
    q_norm = jnp.sqrt(jnp.concatenate([q_sq[hh:hh + 1] for hh in range(heads)], axis=1))
    per_head = lambda x: jnp.concatenate([jnp.broadcast_to(x[:, hh:hh + 1], (nbp, blk)) for hh in range(heads)], axis=1)
    radius, centre = per_head(jnp.sqrt(rad_ref[...])), per_head(jnp.sqrt(cen_ref[...]))
    far_pen = jnp.where(sub < own - 1, pen_ref[...], NEG)
    mean_logit = sc_ref[...] + far_bias + far_pen
    spread = q_norm * (BOUND_INFLATE * radius + BF16_LOGIT_ERR * (centre + radius))
    far_low = jnp.max(mean_logit, axis=0, keepdims=True)
    far_high = jnp.max(mean_logit + spread, axis=0, keepdims=True)
    m_ref[...] = jnp.maximum(pair_max, far_high)
    slack = jnp.maximum(pair_max, far_high) - jnp.maximum(pair_max, far_low)

    @pl.when(jnp.max(slack) > MAX_SLACK)
    def _():
        def max_body(g, m8c):
            j0 = pl.multiple_of(g * grp, grp)
            sg = _nt_dot(kb_ref[pl.ds(j0, grp)].reshape(grp * blk, gl), qcat_b)
            for i in range(grp):
                m8c = jnp.maximum(m8c, fold8(sg[i * blk:(i + 1) * blk], jnp.max) + far_row(j0 + i))
            return m8c

        m_ref[...] = jnp.max(lax.fori_loop(0, n_groups, max_body, m8), axis=0, keepdims=True)

    m = m_ref[...]

    def accumulate(j, p, accs):
        pb = p.astype(BF16)
        return tuple(accs[hh] + jnp.dot(vt_ref[j, hh], pb[:, hh * blk:(hh + 1) * blk], preferred_element_type=F32)
                     for hh in range(heads))

    pb_pair = jnp.exp2(sp_ref[...] - m).astype(BF16)
    vt_pair = vt_ref[pl.ds(jn, 2)]
    accs = tuple(jnp.dot(jnp.concatenate([vt_pair[i, hh] for i in range(2)], axis=1),
                         pb_pair[:, hh * blk:(hh + 1) * blk], preferred_element_type=F32) for hh in range(heads))

    def exp_body(g, accc):
        j0 = pl.multiple_of(g * grp, grp)
        sg = _nt_dot(kb_ref[pl.ds(j0, grp)].reshape(grp * blk, gl), qcat_b)
        for i in range(grp):
            accc = accumulate(j0 + i, jnp.exp2(sg[i * blk:(i + 1) * blk] + (far_row(j0 + i) - m)), accc)
        return accc

    accs = lax.fori_loop(0, n_groups, exp_body, accs)
    out_t = jnp.concatenate([a[:ATT_HEAD_DIM] / a[ATT_HEAD_DIM:ATT_HEAD_DIM + 1] for a in accs], axis=0)
    o_ref[0] = out_t.T


def _moba_prompt(aq, ak, av, t0, t1, rel_bias, group, heads):
    b, s, _ = aq.shape
    nb = s // MOBA_BLOCK
    nbp = -(-nb // 8) * 8
    n_groups = N_ATT_HEADS // heads
    gl = heads * ATT_HEAD_DIM
    wide = heads * MOBA_BLOCK
    assert nb % group == 0
    qspec = pl.BlockSpec((1, MOBA_BLOCK, gl), lambda bi, hp, qi: (bi, qi, hp))
    kvspec = pl.BlockSpec((1, s, gl), lambda bi, hp, qi: (bi, 0, hp))
    tspec = pl.BlockSpec((heads, MOBA_BLOCK, MOBA_BLOCK), lambda bi, hp, qi: (hp, 0, 0))
    return pl.pallas_call(
        functools.partial(_moba_prompt_kernel, n_blocks=nb, group=group, heads=heads),
        grid=(b, n_groups, nb),
        in_specs=[pl.BlockSpec(memory_space=pltpu.SMEM), qspec, kvspec, kvspec, tspec, tspec],
        out_specs=qspec,
        out_shape=jax.ShapeDtypeStruct((b, s, ATT_WIDTH), F32),
        scratch_shapes=[pltpu.VMEM((nbp, gl), F32), pltpu.VMEM((nb, MOBA_BLOCK, gl), BF16),
                        pltpu.VMEM((nb, heads, ATT_HEAD_DIM + SUM_ROWS, MOBA_BLOCK), BF16),
                        pltpu.VMEM((nbp, wide), F32), pltpu.VMEM((nbp, wide), F32),
                        pltpu.VMEM((2 * MOBA_BLOCK, wide), F32), pltpu.VMEM((nbp, LANES), F32),
                        pltpu.VMEM((nbp, LANES), F32), pltpu.VMEM((1, wide), F32)],
        compiler_params=pltpu.CompilerParams(dimension_semantics=("parallel", "parallel", "arbitrary"),
                                             vmem_limit_bytes=VMEM_LIMIT),
        name="moba_prompt",
    )(rel_bias, aq, ak, av, t0, t1)


def _moba_sample_kernel(pt_ref, *refs, blocks_per_step, n_blocks, dec_seq):
    ppb = MOBA_BLOCK // PAGE_SIZE
    pps = blocks_per_step * ppb
    k_pages, v_pages = refs[:pps], refs[pps:2 * pps]
    (q_ref, kn_ref, vn_ref, s_own_ref, s_near_ref, s_far_ref, o_ref,
     qbd_ref, kmt_ref, mm_ref, ll_ref, oo_ref, own_m_ref, own_l_ref, own_o_ref) = refs[2 * pps:]
    c = pl.program_id(1)
    rows = N_ATT_HEADS * dec_seq
    row_head = lax.broadcasted_iota(jnp.int32, (rows, ATT_WIDTH), 0) // dec_seq
    lane_head = lax.broadcasted_iota(jnp.int32, (rows, ATT_WIDTH), 1) // ATT_HEAD_DIM
    lane_r = lax.broadcasted_iota(jnp.int32, (rows, LANES), 1)
    lane_k = lax.broadcasted_iota(jnp.int32, (ATT_WIDTH, LANES), 1)

    def local_softmax(s):
        m = jnp.max(s, axis=-1, keepdims=True)
        p = jnp.exp2(s - m)
        return m, jnp.sum(p, axis=-1, keepdims=True), p.astype(BF16)

    @pl.when(c == 0)
    def _():
        qrep = jnp.concatenate([q_ref[0]] * N_ATT_HEADS, axis=0)
        qbd = jnp.where(row_head == lane_head, qrep, 0.0)
        qbd_ref[...] = qbd
        kmt_ref[...] = jnp.zeros(kmt_ref.shape, F32)
        mm_ref[...] = jnp.zeros(mm_ref.shape, F32)
        ll_ref[...] = jnp.zeros(ll_ref.shape, F32)
        pad = jnp.zeros((LANES - dec_seq, ATT_WIDTH), F32)
        kn = jnp.concatenate([kn_ref[0], pad], axis=0).astype(BF16)
        vn = jnp.concatenate([vn_ref[0], pad], axis=0).astype(BF16)
        m, l, p = local_softmax(_nt_dot(qbd.astype(BF16), kn) + s_own_ref[...].reshape(rows, LANES))
        own_m_ref[...] = jnp.broadcast_to(m, (rows, LANES))
        own_l_ref[...] = jnp.broadcast_to(l, (rows, LANES))
        own_o_ref[...] = jnp.dot(p, vn, preferred_element_type=F32)

    qb = qbd_ref[...].astype(BF16)
    s_near = s_near_ref[...].reshape(rows, LANES)
    s_far = s_far_ref[...].reshape(rows, LANES)
    n0 = c * blocks_per_step
    kt = jnp.concatenate([k_pages[i][0] for i in range(pps)], axis=1)
    s_all = jnp.dot(qb, kt.astype(BF16), preferred_element_type=F32)
    mm, ll, kmt = mm_ref[...], ll_ref[...], kmt_ref[...]
    for g in range(blocks_per_step):
        n = n0 + g
        bias = jnp.concatenate([s_far] * (ppb - 1) + [jnp.where(n == n_blocks - 1, s_near, s_far)], axis=1)
        m, l, p = local_softmax(s_all[:, g * MOBA_BLOCK:(g + 1) * MOBA_BLOCK] + bias)
        vt = jnp.concatenate([v_pages[ppb * g + i][0] for i in range(ppb)], axis=1).astype(BF16)
        oo_ref[n] = _nt_dot(p, vt)
        mm = jnp.where(lane_r == n, m, mm)
        ll = jnp.where(lane_r == n, l, ll)
        ksum = jnp.sum(kt[:, g * MOBA_BLOCK:(g + 1) * MOBA_BLOCK], axis=-1, keepdims=True)
        kmt = jnp.where(lane_k == n, ksum, kmt)
    mm_ref[...] = mm
    ll_ref[...] = ll
    kmt_ref[...] = kmt

    @pl.when(c == pl.num_programs(1) - 1)
    def _():
        kmean_t = kmt_ref[...] * (1.0 / MOBA_BLOCK)
        pen = _topk_penalty(_dot_x3(qbd_ref[...], kmean_t, ((1,), (0,))), n_blocks)
        mp = mm_ref[...] + pen
        m_own = own_m_ref[...]
        m_tot = jnp.maximum(jnp.max(mp, axis=-1, keepdims=True), m_own)
        w = jnp.exp2(mp - m_tot)
        w_own = jnp.exp2(m_own - m_tot)
        den = jnp.sum(w * ll_ref[...], axis=-1, keepdims=True) + w_own * own_l_ref[...]

        num = w_own[:, 0:1] * own_o_ref[...]
        for nn in range(n_blocks):
            num = num + w[:, nn:nn + 1] * oo_ref[nn]
        full = jnp.where(row_head == lane_head, num / den[:, 0:1], 0.0)
        out = full[0:dec_seq]
        for h in range(1, N_ATT_HEADS):
            out = out + full[h * dec_seq:(h + 1) * dec_seq]
        o_ref[0] = out


def _moba_sample(aq, ak, av, pool_kt, pool_vt, page_table, s_own, s_near, s_far, blocks_per_step):
    db, dec_seq, _ = aq.shape
    n_pages = page_table.shape[1]
    ppb = MOBA_BLOCK // PAGE_SIZE
    nb = n_pages // ppb
    pps = blocks_per_step * ppb
    assert nb % blocks_per_step == 0 and nb <= LANES
    rows = N_ATT_HEADS * dec_seq

    def page_spec(i):
        return pl.BlockSpec((1, ATT_WIDTH, PAGE_SIZE), lambda b, c, pt: (pt[b, c * pps + i], 0, 0))

    per_b = pl.BlockSpec((1, dec_seq, ATT_WIDTH), lambda b, c, pt: (b, 0, 0))
    tile = pl.BlockSpec((N_ATT_HEADS, dec_seq, LANES), lambda b, c, pt: (0, 0, 0))
    grid_spec = pltpu.PrefetchScalarGridSpec(
        num_scalar_prefetch=1, grid=(db, nb // blocks_per_step),
        in_specs=[page_spec(i) for i in range(pps)] * 2 + [per_b, per_b, per_b, tile, tile, tile],
        out_specs=per_b,
        scratch_shapes=[pltpu.VMEM((rows, ATT_WIDTH), F32), pltpu.VMEM((ATT_WIDTH, LANES), F32),
                        pltpu.VMEM((rows, LANES), F32), pltpu.VMEM((rows, LANES), F32),
                        pltpu.VMEM((nb, rows, ATT_WIDTH), F32), pltpu.VMEM((rows, LANES), F32),
                        pltpu.VMEM((rows, LANES), F32), pltpu.VMEM((rows, ATT_WIDTH), F32)])
    return pl.pallas_call(
        functools.partial(_moba_sample_kernel, blocks_per_step=blocks_per_step, n_blocks=nb, dec_seq=dec_seq),
        grid_spec=grid_spec,
        out_shape=jax.ShapeDtypeStruct((db, dec_seq, ATT_WIDTH), F32),
        compiler_params=pltpu.CompilerParams(dimension_semantics=("parallel", "arbitrary"),
                                             vmem_limit_bytes=VMEM_LIMIT),
        name="moba_sample",
    )(page_table, *([pool_kt] * pps), *([pool_vt] * pps), aq, ak, av, s_own, s_near, s_far)


def _mlstm_kernel(q_ref, k_ref, v_ref, g_ref, mo_ref, nw_ref, c0_ref, n0_ref, m0_ref,
                  h_ref, c_out, n_out, m_out, c_s, n_s, m_s, *, rows, chunk, batch_rows):
    ci = pl.program_id(1)
    L = chunk

    @pl.when(ci == 0)
    def _():
        c_s[...] = c0_ref[...]
        n_s[...] = n0_ref[...]
        m_s[...] = m0_ref[...]

    def padded(ref, bi, fill=None):
        x = ref[bi]
        if rows == L:
            return x
        tail = jnp.zeros((L - rows, x.shape[1]), F32) if fill is None else fill
        return jnp.concatenate([x, tail], axis=0)

    lane = lax.broadcasted_iota(jnp.int32, (L, LANES), 1)
    pad_lane = lax.broadcasted_iota(jnp.int32, (max(L - rows, 1), LANES), 1)

    is_lf = (lane >= N_ML_HEADS) & (lane < 2 * N_ML_HEADS)
    tri_r = lax.broadcasted_iota(jnp.int32, (L, L), 0)
    tri_c = lax.broadcasted_iota(jnp.int32, (L, L), 1)
    causal = tri_c <= tri_r
    ones_b = jnp.ones((L, LANES), BF16)
    reps = L // LANES
    wide = lambda x: x if reps == 1 else jnp.concatenate([x] * reps, axis=1)
    for bi in range(batch_rows):
        g = padded(g_ref, bi, jnp.where(pad_lane < N_ML_HEADS, NEG, 0.0))
        q_all, k_all, v_all, mo_all = padded(q_ref, bi), padded(k_ref, bi), padded(v_ref, bi), padded(mo_ref, bi)
        g3 = jnp.concatenate(_split_bf16(jnp.where(is_lf, g, 0.0), 3), axis=1)
        b3 = jnp.dot(causal.astype(BF16), g3, preferred_element_type=F32)
        bcum = b3[:, :LANES] + (b3[:, LANES:2 * LANES] + b3[:, 2 * LANES:])
        g_t = g.T
        b_t = bcum.T

        for h in range(N_ML_HEADS):
            sl = slice(h * ML_HEAD_DIM, (h + 1) * ML_HEAD_DIM)
            q, k, v = q_all[:, sl], k_all[:, sl], v_all[:, sl]
            ig = jnp.broadcast_to(g[:, h:h + 1], (L, LANES))
            b = jnp.broadcast_to(bcum[:, N_ML_HEADS + h:N_ML_HEADS + h + 1], (L, LANES))
            ig_row = g_t[h:h + 1, :]
            b_row = b_t[N_ML_HEADS + h:N_ML_HEADS + h + 1, :]
            m_prev = m_s[bi, h]
            C = c_s[bi, h]
            n_rep = n_s[bi, h]

            a = b + m_prev
            log_d = jnp.where(causal, wide(b) - b_row + ig_row, NEG)
            m_t = jnp.maximum(a, jnp.max(log_d, axis=-1, keepdims=True))
            d = jnp.exp(log_d - wide(m_t))
            w_state = jnp.exp(a - m_t)
            qb, kb, vb = q.astype(BF16), k.astype(BF16), v.astype(BF16)
            s = _nt_dot(qb, kb) * d
            v_ones = jnp.concatenate([vb, ones_b], axis=1)
            from_state = jnp.dot(qb, jnp.concatenate([C, n_rep], axis=1).astype(BF16), preferred_element_type=F32)
            from_chunk = jnp.dot(s.astype(BF16), v_ones, preferred_element_type=F32)
            num = w_state * from_state[:, :ML_HEAD_DIM] + from_chunk[:, :ML_HEAD_DIM]
            den = w_state * from_state[:, ML_HEAD_DIM:] + from_chunk[:, ML_HEAD_DIM:]
            hid = num / jnp.maximum(jnp.abs(den), jnp.exp(-m_t))

            hn = _rms(hid, nw_ref[:, sl])
            h_ref[bi, :, sl] = (hn * jax.nn.sigmoid(mo_all[:, sl]))[:rows]

            b_last = b[L - 1:L, :]
            m_new = m_t[L - 1:L, :]
            g_state = jnp.exp(b_last + m_prev - m_new)
            kg = k * jnp.exp(b_last - b + ig - m_new)
            added = _tn_dot(kg.astype(BF16), v_ones)
            c_s[bi, h] = g_state * C + added[:, :ML_HEAD_DIM]
            n_s[bi, h] = g_state * n_rep + added[:, ML_HEAD_DIM:]
            m_s[bi, h] = m_new

    @pl.when(ci == pl.num_programs(1) - 1)
    def _():
        c_out[...] = c_s[...]
        n_out[...] = n_s[...]
        m_out[...] = m_s[...]


def _mlstm(mq, mk, mv, gates, mo, ml_norm, c0, n0, m0, chunk, batch_rows):
    b, s, _ = mq.shape
    rows = min(s, chunk)
    n_chunks = s // rows
    assert b % batch_rows == 0
    bb = batch_rows
    tok = lambda w: pl.BlockSpec((bb, rows, w), lambda bi, ci: (bi, ci, 0))
    st4 = lambda shape: pl.BlockSpec(shape, lambda bi, ci: (bi, 0, 0, 0))
    c_shape, m_shape = (bb, N_ML_HEADS, ML_HEAD_DIM, ML_HEAD_DIM), (bb, N_ML_HEADS, 1, LANES)
    n_shape = (bb, N_ML_HEADS, ML_HEAD_DIM, LANES)
    return pl.pallas_call(
        functools.partial(_mlstm_kernel, rows=rows, chunk=chunk, batch_rows=bb),
        grid=(b // bb, n_chunks),
        in_specs=[tok(ML_WIDTH), tok(ML_WIDTH), tok(ML_WIDTH), tok(LANES), tok(ML_WIDTH),
                  pl.BlockSpec((1, ML_WIDTH), lambda bi, ci: (0, 0)),
                  st4(c_shape), st4(n_shape), st4(m_shape)],
        out_specs=[tok(ML_WIDTH), st4(c_shape), st4(n_shape), st4(m_shape)],
        out_shape=[jax.ShapeDtypeStruct((b, s, ML_WIDTH), F32),
                   jax.ShapeDtypeStruct((b,) + c_shape[1:], F32),
                   jax.ShapeDtypeStruct((b,) + n_shape[1:], F32),
                   jax.ShapeDtypeStruct((b,) + m_shape[1:], F32)],
        scratch_shapes=[pltpu.VMEM(c_shape, F32), pltpu.VMEM(n_shape, F32), pltpu.VMEM(m_shape, F32)],
        compiler_params=pltpu.CompilerParams(dimension_semantics=("parallel", "arbitrary"),
                                             vmem_limit_bytes=VMEM_LIMIT),
        name="mlstm",
    )(mq, mk, mv, gates, mo, ml_norm, c0, n0, m0)


def _out_ffn_kernel(x_ref, att_ref, ml_ref, wo_ref, gf_ref, wu_ref, wd_ref, gl_ref, y_ref, *, ff_chunk):
    x1 = x_ref[...] \
        + jnp.dot(att_ref[...].astype(BF16), wo_ref[0:ATT_WIDTH, :], preferred_element_type=F32) \
        + jnp.dot(ml_ref[...].astype(BF16), wo_ref[ATT_WIDTH:, :], preferred_element_type=F32)
    xb = _rms(x1, gf_ref[...]).astype(BF16)
    ffn = None
    for c in range(D_FF // ff_chunk):
        sl = slice(c * ff_chunk, (c + 1) * ff_chunk)
        u = jnp.maximum(jnp.dot(xb, wu_ref[:, sl], preferred_element_type=F32), 0.0)
        d = jnp.dot((u * u).astype(BF16), wd_ref[sl, :], preferred_element_type=F32)
        ffn = d if ffn is None else ffn + d
    y_ref[...] = _rms(x1 + ffn, gl_ref[...])


def _out_ffn(x2, att, ml, w_out, norm_ffn, w_up, w_down, norm_final, tm, ff_chunk):
    n = x2.shape[0]
    const = lambda shape: pl.BlockSpec(shape, lambda i: (0, 0), pipeline_mode=pl.Buffered(1))
    row = lambda w: pl.BlockSpec((tm, w), lambda i: (i, 0))
    return pl.pallas_call(
        functools.partial(_out_ffn_kernel, ff_chunk=ff_chunk),
        grid=(n // tm,),
        in_specs=[row(D_MODEL), row(ATT_WIDTH), row(ML_WIDTH), const((D_MODEL, D_MODEL)), const((1, D_MODEL)),
                  const((D_MODEL, D_FF)), const((D_FF, D_MODEL)), const((1, D_MODEL))],
        out_specs=row(D_MODEL),
        out_shape=jax.ShapeDtypeStruct((n, D_MODEL), F32),
        compiler_params=pltpu.CompilerParams(dimension_semantics=("parallel",), vmem_limit_bytes=VMEM_LIMIT),
        name="out_proj_ffn",
    )(x2, att, ml, w_out, norm_ffn, w_up, w_down, norm_final)


def _layer_weights(l, norm_mix, w_in, b_ig, b_fg, ml_norm, w_out, norm_ffn, w_up, w_down):
    w_gate = jnp.pad(w_in[l, :, GATE_COL0:], ((0, 0), (0, LANES - 2 * N_ML_HEADS)))
    w_all = jnp.concatenate([w_in[l, :, :GATE_COL0].astype(BF16)] + _split_bf16(w_gate, 2), axis=1)
    b_gate = jnp.pad(jnp.concatenate([b_ig[l], b_fg[l]]), (0, LANES - 2 * N_ML_HEADS))[None, :]
    return dict(norm_mix=norm_mix[l][None, :], w_all=w_all, b_gate=b_gate,
                ml_norm=ml_norm[l][None, :], w_out=w_out[l].astype(BF16), norm_ffn=norm_ffn[l][None, :],
                w_up=w_up[l].astype(BF16), w_down=w_down[l].astype(BF16))


def _trunk_layer(x, w, attend, state0, norm_final, tiles):
    b, s, _ = x.shape
    x2 = x.reshape(b * s, D_MODEL)
    aq, ak, av, mq, mk, mv, mo, gates, *kv_t = _project(x2, w["norm_mix"], w["w_all"], w["b_gate"], tiles.tm, s)
    r3 = lambda a: a.reshape(b, s, a.shape[-1])
    if kv_t:
        k_out, v_out = (a.reshape(b, N_ATT_HEADS, ATT_HEAD_DIM, s).transpose(0, 3, 1, 2) for a in kv_t)
    else:
        k_out, v_out = (a.reshape(b, s, N_ATT_HEADS, ATT_HEAD_DIM) for a in (ak, av))
    att = attend(r3(aq), r3(ak), r3(av))
    c0, n0, m0 = state0
    ml, c_new, n_new, m_new = _mlstm(r3(mq), r3(mk), r3(mv), r3(gates), r3(mo), w["ml_norm"], c0, n0, m0, tiles.ml_chunk,
                                     tiles.ml_batch_rows)
    y = _out_ffn(x2, att.reshape(b * s, ATT_WIDTH), ml.reshape(b * s, ML_WIDTH), w["w_out"], w["norm_ffn"],
                 w["w_up"], w["w_down"], norm_final, tiles.tm, tiles.ff_chunk)
    return y.reshape(b, s, D_MODEL), k_out, v_out, c_new, n_new[:, :, :, 0], m_new[:, :, 0, 0]


def kernel(x_prompt, x_sample, cache_k, cache_v, state_C, state_n, state_m, page_table, rel_bias, norm_mix, w_in,
           b_ig, b_fg, ml_norm, w_out, norm_ffn, w_up, w_down, norm_final):
    depth = w_in.shape[0]
    assert depth == 1, "the fused final RMSNorm assumes a single layer"
    bp, sp, _ = x_prompt.shape
    db, dec_seq, _ = x_sample.shape
    n_phys = cache_k.shape[1]
    t0, t1, s_own, s_near, s_far = _bias_tiles(rel_bias, dec_seq)
    gl = norm_final[None, :]
    l = 0
    w = _layer_weights(l, norm_mix, w_in, b_ig, b_fg, ml_norm, w_out, norm_ffn, w_up, w_down)

    zero_state = (jnp.zeros((bp, N_ML_HEADS, ML_HEAD_DIM, ML_HEAD_DIM), F32),
                  jnp.zeros((bp, N_ML_HEADS, ML_HEAD_DIM, LANES), F32), jnp.zeros((bp, N_ML_HEADS, 1, LANES), F32))
    yp, kp, vp, cp, np_, mp = _trunk_layer(
        x_prompt, w, lambda q, k, v: _moba_prompt(q, k, v, t0, t1, rel_bias, PROMPT_KEY_GROUP, PROMPT_HEADS_PER_STEP),
        zero_state, gl,
        PROMPT_TILES)

    to_pool = lambda a: jnp.transpose(a, (0, 1, 3, 4, 2)).reshape(depth * n_phys, ATT_WIDTH, PAGE_SIZE)
    pool_kt, pool_vt = to_pool(cache_k), to_pool(cache_v)
    pt = page_table + l * n_phys

    def attend_sample(q, k, v):
        return _moba_sample(q, k, v, pool_kt, pool_vt, pt, s_own, s_near, s_far, SAMPLE_BLOCKS_PER_STEP)

    state0 = (state_C[l], jnp.broadcast_to(state_n[l][:, :, :, None], (db, N_ML_HEADS, ML_HEAD_DIM, LANES)),
              jnp.broadcast_to(state_m[l][:, :, None, None], (db, N_ML_HEADS, 1, LANES)))
    ys, ks, vs, cs, ns, ms = _trunk_layer(x_sample, w, attend_sample, state0, gl, SAMPLE_TILES)

    st = lambda a: a[None]
    return (yp, ys, st(kp), st(vp), st(cp), st(np_), st(mp), st(ks), st(vs), st(cs), st(ns), st(ms))
```

```python
import functools
import math
from typing import NamedTuple

import numpy as np
import jax
import jax.numpy as jnp
from jax import lax
from jax.experimental import pallas as pl
from jax.experimental.pallas import tpu as pltpu

F32 = jnp.float32
BF16 = jnp.bfloat16

D_MODEL = 1024
ATT_HEAD_DIM = 64
N_ATT_HEADS = 8
ATT_WIDTH = 512
N_ML_HEADS = 4
ML_HEAD_DIM = 128
ML_WIDTH = 512
SLAB = 512
N_SLABS = 7
GATE_COL0 = N_SLABS * SLAB
MOBA_BLOCK = 256
MOBA_TOPK = 3
PAGE_SIZE = 128
NUM_BUCKETS = 32
MAX_DISTANCE = 128
D_FF = 4 * D_MODEL
RMS_EPS = 1e-6
NEG = -1e30
LOG2E = math.log2(math.e)
LANES = 128
SUM_ROWS = 16
MAX_SLACK = 64.0
BOUND_INFLATE = 1.05
BF16_LOGIT_ERR = 2.0 ** -6
VMEM_LIMIT = 56 * 1024 * 1024


class _Tiles(NamedTuple):
    tm: int
    ml_chunk: int
    ml_batch_rows: int
    ff_chunk: int = 1024


PROMPT_TILES = _Tiles(tm=512, ml_chunk=512, ml_batch_rows=4)
SAMPLE_TILES = _Tiles(tm=256, ml_chunk=128, ml_batch_rows=4)
PROMPT_KEY_GROUP = 4
PROMPT_HEADS_PER_STEP = 4
SAMPLE_BLOCKS_PER_STEP = 16


def _nt_dot(a, b):
    return lax.dot_general(a, b, (((1,), (1,)), ((), ())), preferred_element_type=F32)


def _split_bf16(x, parts):
    out = []
    for _ in range(parts):
        p = x.astype(BF16)
        out.append(p)
        x = x - p.astype(F32)
    return out


def _dot_x3(a, b, contract):
    (ah, al), (bh, bl) = _split_bf16(a, 2), _split_bf16(b, 2)
    d = lambda x, y: lax.dot_general(x, y, (contract, ((), ())), preferred_element_type=F32)
    return d(ah, bh) + (d(ah, bl) + d(al, bh))


def _tn_dot(a, b):
    return lax.dot_general(a, b, (((0,), (0,)), ((), ())), preferred_element_type=F32)


def _bucket_upper_bounds():
    n = np.arange(0, MAX_DISTANCE + 1)
    max_exact = NUM_BUCKETS // 2
    nf = np.maximum(n, 1).astype(np.float32)
    large = max_exact + (np.log(nf / np.float32(max_exact)) / np.float32(math.log(MAX_DISTANCE / max_exact))
                         * np.float32(NUM_BUCKETS - max_exact)).astype(np.int32)
    large = np.minimum(large, NUM_BUCKETS - 1)
    bucket = np.where(n < max_exact, n, large)
    return np.array([int(n[bucket <= b].max()) for b in range(NUM_BUCKETS - 1)], dtype=np.int32)


_BUCKET_UPPER = _bucket_upper_bounds()


def _bias_of_dist(dist, h, rb_ref):
    acc = jnp.full(dist.shape, rb_ref[NUM_BUCKETS - 1, h], F32)
    for b in range(NUM_BUCKETS - 2, -1, -1):
        acc = jnp.where(dist <= int(_BUCKET_UPPER[b]), rb_ref[b, h], acc)
    return acc


def _bias_tiles_kernel(rb_ref, t0_ref, t1_ref, s_own_ref, s_near_ref, s_far_ref, *, dec_seq):
    h = pl.program_id(0)
    kk = lax.broadcasted_iota(jnp.int32, (MOBA_BLOCK, MOBA_BLOCK), 0)
    qq = lax.broadcasted_iota(jnp.int32, (MOBA_BLOCK, MOBA_BLOCK), 1)
    d0 = qq - kk
    t0_ref[0] = jnp.where(d0 >= 0, _bias_of_dist(jnp.maximum(d0, 0), h, rb_ref) * LOG2E, NEG)
    t1_ref[0] = _bias_of_dist(d0 + MOBA_BLOCK, h, rb_ref) * LOG2E
    t = lax.broadcasted_iota(jnp.int32, (dec_seq, LANES), 0)
    cc = lax.broadcasted_iota(jnp.int32, (dec_seq, LANES), 1)
    ds = t - cc
    s_own_ref[0] = jnp.where((ds >= 0) & (cc < dec_seq),
                             _bias_of_dist(jnp.maximum(ds, 0), h, rb_ref) * LOG2E, NEG)
    s_near_ref[0] = _bias_of_dist(ds + PAGE_SIZE, h, rb_ref) * LOG2E
    s_far_ref[0] = jnp.full((dec_seq, LANES), rb_ref[NUM_BUCKETS - 1, h] * LOG2E, F32)


def _bias_tiles(rel_bias, dec_seq):
    smem = pl.BlockSpec(memory_space=pltpu.SMEM)
    big = pl.BlockSpec((1, MOBA_BLOCK, MOBA_BLOCK), lambda h: (h, 0, 0))
    small = pl.BlockSpec((1, dec_seq, LANES), lambda h: (h, 0, 0))
    return pl.pallas_call(
        functools.partial(_bias_tiles_kernel, dec_seq=dec_seq),
        grid=(N_ATT_HEADS,),
        in_specs=[smem],
        out_specs=[big, big, small, small, small],
        out_shape=[jax.ShapeDtypeStruct((N_ATT_HEADS, MOBA_BLOCK, MOBA_BLOCK), F32)] * 2
        + [jax.ShapeDtypeStruct((N_ATT_HEADS, dec_seq, LANES), F32)] * 3,
        name="bias_tiles",
    )(rel_bias)


def _rms(x, g):
    return x * lax.rsqrt(jnp.mean(x * x, axis=-1, keepdims=True) + RMS_EPS) * g


def _log_sigmoid(z):
    return jnp.minimum(z, 0.0) - jnp.log1p(jnp.exp(-jnp.abs(z)))


def _proj_kernel(x_ref, g_ref, w_ref, bg_ref, aq_ref, ak_ref, av_ref, mq_ref, mk_ref, mv_ref, mo_ref,
                 gate_ref, *kv_t_refs):
    xn = _rms(x_ref[...], g_ref[...])
    xb = xn.astype(BF16)
    outs = (aq_ref, ak_ref, av_ref, mq_ref, mk_ref, mv_ref, mo_ref)
    scales = (ATT_HEAD_DIM ** -0.5 * LOG2E, None, None, None, ML_HEAD_DIM ** -0.5, None, None)
    for i, (o_ref, s) in enumerate(zip(outs, scales)):
        r = jnp.dot(xb, w_ref[:, i * SLAB:(i + 1) * SLAB], preferred_element_type=F32)
        o_ref[...] = r if s is None else r * s
        if kv_t_refs and i in (1, 2):
            kv_t_refs[i - 1][0] = r.T
    gate_hi_lo = jnp.dot(xb, w_ref[:, GATE_COL0:GATE_COL0 + 2 * LANES], preferred_element_type=F32)
    x_lo = (xn - xb.astype(F32)).astype(BF16)
    z = gate_hi_lo[:, :LANES] + (gate_hi_lo[:, LANES:]
                                 + jnp.dot(x_lo, w_ref[:, GATE_COL0:GATE_COL0 + LANES], preferred_element_type=F32))
    z = z + bg_ref[...]
    lane = lax.broadcasted_iota(jnp.int32, z.shape, 1)
    gate_ref[...] = jnp.where((lane >= N_ML_HEADS) & (lane < 2 * N_ML_HEADS), _log_sigmoid(z), z)


def _project(x2, norm_g, w_all, b_gate, tm, seq):
    n = x2.shape[0]
    const = lambda shape: pl.BlockSpec(shape, lambda i: (0, 0), pipeline_mode=pl.Buffered(1))
    slab = pl.BlockSpec((tm, SLAB), lambda i: (i, 0))
    out_specs = [slab] * N_SLABS + [pl.BlockSpec((tm, LANES), lambda i: (i, 0))]
    out_shape = [jax.ShapeDtypeStruct((n, SLAB), F32)] * N_SLABS + [jax.ShapeDtypeStruct((n, LANES), F32)]
    if seq % tm == 0:
        tiles = seq // tm
        out_specs += [pl.BlockSpec((1, SLAB, tm), lambda i: (i // tiles, 0, i % tiles))] * 2
        out_shape += [jax.ShapeDtypeStruct((n // seq, SLAB, seq), F32)] * 2
    return pl.pallas_call(
        _proj_kernel,
        grid=(n // tm,),
        in_specs=[pl.BlockSpec((tm, D_MODEL), lambda i: (i, 0)), const((1, D_MODEL)),
                  const((D_MODEL, N_SLABS * SLAB + 2 * LANES)), const((1, LANES))],
        out_specs=out_specs,
        out_shape=out_shape,
        compiler_params=pltpu.CompilerParams(dimension_semantics=("parallel",), vmem_limit_bytes=VMEM_LIMIT),
        name="rms_in_proj",
    )(x2, norm_g, w_all, b_gate)


def _topk_penalty(scores, n_valid):
    lane = lax.broadcasted_iota(jnp.int32, scores.shape, 1)
    rank = jnp.zeros(scores.shape, jnp.int32)
    for n in range(n_valid):
        col = scores[:, n:n + 1]
        rank = rank + jnp.where((col > scores) | ((col == scores) & (lane > n)), 1, 0)
    return jnp.where((rank < MOBA_TOPK) & (lane < n_valid), 0.0, NEG)


def _moba_prompt_kernel(rb_ref, q_ref, k_ref, v_ref, t0_ref, t1_ref, o_ref, km_ref, kb_ref, vt_ref, sc_ref,
                        pen_ref, sp_ref, rad_ref, cen_ref, m_ref, *, n_blocks, group, heads):
    hp = pl.program_id(1)
    own = pl.program_id(2)
    blk = MOBA_BLOCK
    nbp = km_ref.shape[0]
    gl = heads * ATT_HEAD_DIM

    sel_l = lax.broadcasted_iota(jnp.int32, (gl, LANES), 0)
    sel_c = lax.broadcasted_iota(jnp.int32, (gl, LANES), 1)
    head_sel = jnp.where(sel_l // ATT_HEAD_DIM == sel_c, 1.0, 0.0).astype(BF16)

    @pl.when(own == 0)
    def _():
        km_ref[...] = jnp.zeros(km_ref.shape, F32)
        rad_ref[...] = jnp.zeros(rad_ref.shape, F32)
        cen_ref[...] = jnp.zeros(cen_ref.shape, F32)
        for n in range(n_blocks):
            kn = k_ref[0, n * blk:(n + 1) * blk, :]
            cn = jnp.mean(kn, axis=0, keepdims=True)
            km_ref[n:n + 1, :] = cn
            kb_ref[n] = kn.astype(BF16)
            dk = kn - cn
            rad_ref[n:n + 1, :] = jnp.max(jnp.dot((dk * dk).astype(BF16), head_sel, preferred_element_type=F32),
                                          axis=0, keepdims=True)
            cen_ref[n:n + 1, :] = jnp.dot(jnp.broadcast_to(cn * cn, (8, gl)).astype(BF16), head_sel,
                                          preferred_element_type=F32)[0:1]
            vtn = v_ref[0, n * blk:(n + 1) * blk, :].T.astype(BF16)
            for hh in range(heads):
                vt_ref[n, hh, 0:ATT_HEAD_DIM, :] = vtn[hh * ATT_HEAD_DIM:(hh + 1) * ATT_HEAD_DIM]
                vt_ref[n, hh, ATT_HEAD_DIM:, :] = jnp.ones((SUM_ROWS, blk), BF16)

    q2 = q_ref[0]
    lane = lax.broadcasted_iota(jnp.int32, q2.shape, 1)
    q_hi, q_lo = _split_bf16(q2, 2)
    qcat_b = jnp.concatenate([jnp.where((lane // ATT_HEAD_DIM) == hh, q_hi, jnp.zeros_like(q_hi))
                              for hh in range(heads)], axis=0)
    wide = heads * blk
    lane_m = lax.broadcasted_iota(jnp.int32, (nbp, gl), 1)
    km2 = jnp.concatenate([jnp.where((lane_m // ATT_HEAD_DIM) == hh, km_ref[...], 0.0) for hh in range(heads)], axis=0)
    km_hi, km_lo = _split_bf16(km2, 2)
    hi_q = _nt_dot(jnp.concatenate([km_hi, km_lo], axis=0), q_hi)
    sc2 = hi_q[:heads * nbp] + (hi_q[heads * nbp:] + _nt_dot(km_hi, q_lo))
    sc_ref[...] = jnp.concatenate([sc2[hh * nbp:(hh + 1) * nbp] for hh in range(heads)], axis=1)

    sub = lax.broadcasted_iota(jnp.int32, (nbp, wide), 0)

    def rank_body(n, rank):
        sc = sc_ref[...]
        row = sc_ref[pl.ds(n, 1), :]
        return rank + jnp.where((row > sc) | ((row == sc) & (sub > n)), 1, 0)

    rank = lax.fori_loop(0, own, rank_body, jnp.zeros((nbp, wide), jnp.int32))
    pen_ref[...] = jnp.where((rank < MOBA_TOPK) & (sub < own), 0.0, NEG)

    far_bias = jnp.concatenate([jnp.full((1, blk), rb_ref[NUM_BUCKETS - 1, heads * hp + hh] * LOG2E, F32)
                                for hh in range(heads)], axis=1)

    def fold8(x, op):
        return op(x.reshape(blk // 8, 8, x.shape[1]), axis=0)

    def far_row(j):
        return jnp.where(j < own - 1, pen_ref[pl.ds(j, 1), :] + far_bias, NEG)

    grp = group
    n_groups = (jnp.maximum(own - 1, 0) + grp - 1) // grp
    jn = jnp.maximum(own - 1, 0)
    near_row = jnp.where(own >= 1, pen_ref[pl.ds(jn, 1), :], NEG)

    t0c = jnp.concatenate([t0_ref[hh] for hh in range(heads)], axis=1)
    t1c = jnp.concatenate([t1_ref[hh] for hh in range(heads)], axis=1)
    no_prev = own == 0
    s_pair = _nt_dot(kb_ref[pl.ds(jn, 2)].reshape(2 * blk, gl), qcat_b)
    s_first = s_pair[:blk] + jnp.where(no_prev, t0c, t1c + near_row)
    s_second = jnp.where(no_prev, NEG, s_pair[blk:] + t0c)
    sp_ref[0:blk, :] = s_first
    sp_ref[blk:, :] = s_second
    m8 = jnp.maximum(fold8(s_first, jnp.max), fold8(s_second, jnp.max))
    pair_max = jnp.max(m8, axis=0, keepdims=True)

    row8 = lax.broadcasted_iota(jnp.int32, (8, gl), 0)
    lane8 = lax.broadcasted_iota(jnp.int32, (8, gl), 1)
    q_sq = _nt_dot(jnp.where(lane8 // ATT_HEAD_DIM == row8, 1.0, 0.0).astype(BF16),
                   (q2 * q2).astype(BF16))
    q_norm = jnp.sqrt(jnp.concatenate([q_sq[hh:hh + 1] for hh in range(heads)], axis=1))
    per_head = lambda x: jnp.concatenate([jnp.broadcast_to(x[:, hh:hh + 1], (nbp, blk)) for hh in range(heads)], axis=1)
    radius, centre = per_head(jnp.sqrt(rad_ref[...])), per_head(jnp.sqrt(cen_ref[...]))
    far_pen = jnp.where(sub < own - 1, pen_ref[...], NEG)
    mean_logit = sc_ref[...] + far_bias + far_pen
    spread = q_norm * (BOUND_INFLATE * radius + BF16_LOGIT_ERR * (centre + radius))
    far_low = jnp.max(mean_logit, axis=0, keepdims=True)
    far_high = jnp.max(mean_logit + spread, axis=0, keepdims=True)
    m_ref[...] = jnp.maximum(pair_max, far_high)
    slack = jnp.maximum(pair_max, far_high) - jnp.maximum(pair_max, far_low)

    @pl.when(jnp.max(slack) > MAX_SLACK)
    def _():
        def max_body(g, m8c):
            j0 = pl.multiple_of(g * grp, grp)
            sg = _nt_dot(kb_ref[pl.ds(j0, grp)].reshape(grp * blk, gl), qcat_b)
            for i in range(grp):
                m8c = jnp.maximum(m8c, fold8(sg[i * blk:(i + 1) * blk], jnp.max) + far_row(j0 + i))
            return m8c

        m_ref[...] = jnp.max(lax.fori_loop(0, n_groups, max_body, m8), axis=0, keepdims=True)

    m = m_ref[...]

    def accumulate(j, p, accs):
        pb = p.astype(BF16)
        return tuple(accs[hh] + jnp.dot(vt_ref[j, hh], pb[:, hh * blk:(hh + 1) * blk], preferred_element_type=F32)
                     for hh in range(heads))

    pb_pair = jnp.exp2(sp_ref[...] - m).astype(BF16)
    vt_pair = vt_ref[pl.ds(jn, 2)]
    accs = tuple(jnp.dot(jnp.concatenate([vt_pair[i, hh] for i in range(2)], axis=1),
                         pb_pair[:, hh * blk:(hh + 1) * blk], preferred_element_type=F32) for hh in range(heads))

    def exp_body(g, accc):
        j0 = pl.multiple_of(g * grp, grp)
        sg = _nt_dot(kb_ref[pl.ds(j0, grp)].reshape(grp * blk, gl), qcat_b)
        for i in range(grp):
            accc = accumulate(j0 + i, jnp.exp2(sg[i * blk:(i + 1) * blk] + (far_row(j0 + i) - m)), accc)
        return accc

    accs = lax.fori_loop(0, n_groups, exp_body, accs)
    out_t = jnp.concatenate([a[:ATT_HEAD_DIM] / a[ATT_HEAD_DIM:ATT_HEAD_DIM + 1] for a in accs], axis=0)
    o_ref[0] = out_t.T


def _moba_prompt(aq, ak, av, t0, t1, rel_bias, group, heads):
    b, s, _ = aq.shape
    nb = s // MOBA_BLOCK
    nbp = -(-nb // 8) * 8
    n_groups = N_ATT_HEADS // heads
    gl = heads * ATT_HEAD_DIM
    wide = heads * MOBA_BLOCK
    assert nb % group == 0
    qspec = pl.BlockSpec((1, MOBA_BLOCK, gl), lambda bi, hp, qi: (bi, qi, hp))
    kvspec = pl.BlockSpec((1, s, gl), lambda bi, hp, qi: (bi, 0, hp))
    tspec = pl.BlockSpec((heads, MOBA_BLOCK, MOBA_BLOCK), lambda bi, hp, qi: (hp, 0, 0))
    return pl.pallas_call(
        functools.partial(_moba_prompt_kernel, n_blocks=nb, group=group, heads=heads),
        grid=(b, n_groups, nb),
        in_specs=[pl.BlockSpec(memory_space=pltpu.SMEM), qspec, kvspec, kvspec, tspec, tspec],
        out_specs=qspec,
        out_shape=jax.ShapeDtypeStruct((b, s, ATT_WIDTH), F32),
        scratch_shapes=[pltpu.VMEM((nbp, gl), F32), pltpu.VMEM((nb, MOBA_BLOCK, gl), BF16),
                        pltpu.VMEM((nb, heads, ATT_HEAD_DIM + SUM_ROWS, MOBA_BLOCK), BF16),
                        pltpu.VMEM((nbp, wide), F32), pltpu.VMEM((nbp, wide), F32),
                        pltpu.VMEM((2 * MOBA_BLOCK, wide), F32), pltpu.VMEM((nbp, LANES), F32),
                        pltpu.VMEM((nbp, LANES), F32), pltpu.VMEM((1, wide), F32)],
        compiler_params=pltpu.CompilerParams(dimension_semantics=("parallel", "parallel", "arbitrary"),
                                             vmem_limit_bytes=VMEM_LIMIT),
        name="moba_prompt",
    )(rel_bias, aq, ak, av, t0, t1)


def _moba_sample_kernel(pt_ref, *refs, blocks_per_step, n_blocks, dec_seq):
    ppb = MOBA_BLOCK // PAGE_SIZE
    pps = blocks_per_step * ppb
    k_pages, v_pages = refs[:pps], refs[pps:2 * pps]
    (q_ref, kn_ref, vn_ref, s_own_ref, s_near_ref, s_far_ref, o_ref,
     qbd_ref, kmt_ref, mm_ref, ll_ref, oo_ref, own_m_ref, own_l_ref, own_o_ref) = refs[2 * pps:]
    c = pl.program_id(1)
    rows = N_ATT_HEADS * dec_seq
    row_head = lax.broadcasted_iota(jnp.int32, (rows, ATT_WIDTH), 0) // dec_seq
    lane_head = lax.broadcasted_iota(jnp.int32, (rows, ATT_WIDTH), 1) // ATT_HEAD_DIM
    lane_r = lax.broadcasted_iota(jnp.int32, (rows, LANES), 1)
    lane_k = lax.broadcasted_iota(jnp.int32, (ATT_WIDTH, LANES), 1)

    def local_softmax(s):
        m = jnp.max(s, axis=-1, keepdims=True)
        p = jnp.exp2(s - m)
        return m, jnp.sum(p, axis=-1, keepdims=True), p.astype(BF16)

    @pl.when(c == 0)
    def _():
        qrep = jnp.concatenate([q_ref[0]] * N_ATT_HEADS, axis=0)
        qbd = jnp.where(row_head == lane_head, qrep, 0.0)
        qbd_ref[...] = qbd
        kmt_ref[...] = jnp.zeros(kmt_ref.shape, F32)
        mm_ref[...] = jnp.zeros(mm_ref.shape, F32)
        ll_ref[...] = jnp.zeros(ll_ref.shape, F32)
        pad = jnp.zeros((LANES - dec_seq, ATT_WIDTH), F32)
        kn = jnp.concatenate([kn_ref[0], pad], axis=0).astype(BF16)
        vn = jnp.concatenate([vn_ref[0], pad], axis=0).astype(BF16)
        m, l, p = local_softmax(_nt_dot(qbd.astype(BF16), kn) + s_own_ref[...].reshape(rows, LANES))
        own_m_ref[...] = jnp.broadcast_to(m, (rows, LANES))
        own_l_ref[...] = jnp.broadcast_to(l, (rows, LANES))
        own_o_ref[...] = jnp.dot(p, vn, preferred_element_type=F32)

    qb = qbd_ref[...].astype(BF16)
    s_near = s_near_ref[...].reshape(rows, LANES)
    s_far = s_far_ref[...].reshape(rows, LANES)
    n0 = c * blocks_per_step
    kt = jnp.concatenate([k_pages[i][0] for i in range(pps)], axis=1)
    s_all = jnp.dot(qb, kt.astype(BF16), preferred_element_type=F32)
    mm, ll, kmt = mm_ref[...], ll_ref[...], kmt_ref[...]
    for g in range(blocks_per_step):
        n = n0 + g
        bias = jnp.concatenate([s_far] * (ppb - 1) + [jnp.where(n == n_blocks - 1, s_near, s_far)], axis=1)
        m, l, p = local_softmax(s_all[:, g * MOBA_BLOCK:(g + 1) * MOBA_BLOCK] + bias)
        vt = jnp.concatenate([v_pages[ppb * g + i][0] for i in range(ppb)], axis=1).astype(BF16)
        oo_ref[n] = _nt_dot(p, vt)
        mm = jnp.where(lane_r == n, m, mm)
        ll = jnp.where(lane_r == n, l, ll)
        ksum = jnp.sum(kt[:, g * MOBA_BLOCK:(g + 1) * MOBA_BLOCK], axis=-1, keepdims=True)
        kmt = jnp.where(lane_k == n, ksum, kmt)
    mm_ref[...] = mm
    ll_ref[...] = ll
    kmt_ref[...] = kmt

    @pl.when(c == pl.num_programs(1) - 1)
    def _():
        kmean_t = kmt_ref[...] * (1.0 / MOBA_BLOCK)
        pen = _topk_penalty(_dot_x3(qbd_ref[...], kmean_t, ((1,), (0,))), n_blocks)
        mp = mm_ref[...] + pen
        m_own = own_m_ref[...]
        m_tot = jnp.maximum(jnp.max(mp, axis=-1, keepdims=True), m_own)
        w = jnp.exp2(mp - m_tot)
        w_own = jnp.exp2(m_own - m_tot)
        den = jnp.sum(w * ll_ref[...], axis=-1, keepdims=True) + w_own * own_l_ref[...]

        num = w_own[:, 0:1] * own_o_ref[...]
        for nn in range(n_blocks):
            num = num + w[:, nn:nn + 1] * oo_ref[nn]
        full = jnp.where(row_head == lane_head, num / den[:, 0:1], 0.0)
        out = full[0:dec_seq]
        for h in range(1, N_ATT_HEADS):
            out = out + full[h * dec_seq:(h + 1) * dec_seq]
        o_ref[0] = out


def _moba_sample(aq, ak, av, pool_kt, pool_vt, page_table, s_own, s_near, s_far, blocks_per_step):
    db, dec_seq, _ = aq.shape
    n_pages = page_table.shape[1]
    ppb = MOBA_BLOCK // PAGE_SIZE
    nb = n_pages // ppb
    pps = blocks_per_step * ppb
    assert nb % blocks_per_step == 0 and nb <= LANES
    rows = N_ATT_HEADS * dec_seq

    def page_spec(i):
        return pl.BlockSpec((1, ATT_WIDTH, PAGE_SIZE), lambda b, c, pt: (pt[b, c * pps + i], 0, 0))

    per_b = pl.BlockSpec((1, dec_seq, ATT_WIDTH), lambda b, c, pt: (b, 0, 0))
    tile = pl.BlockSpec((N_ATT_HEADS, dec_seq, LANES), lambda b, c, pt: (0, 0, 0))
    grid_spec = pltpu.PrefetchScalarGridSpec(
        num_scalar_prefetch=1, grid=(db, nb // blocks_per_step),
        in_specs=[page_spec(i) for i in range(pps)] * 2 + [per_b, per_b, per_b, tile, tile, tile],
        out_specs=per_b,
        scratch_shapes=[pltpu.VMEM((rows, ATT_WIDTH), F32), pltpu.VMEM((ATT_WIDTH, LANES), F32),
                        pltpu.VMEM((rows, LANES), F32), pltpu.VMEM((rows, LANES), F32),
                        pltpu.VMEM((nb, rows, ATT_WIDTH), F32), pltpu.VMEM((rows, LANES), F32),
                        pltpu.VMEM((rows, LANES), F32), pltpu.VMEM((rows, ATT_WIDTH), F32)])
    return pl.pallas_call(
        functools.partial(_moba_sample_kernel, blocks_per_step=blocks_per_step, n_blocks=nb, dec_seq=dec_seq),
        grid_spec=grid_spec,
        out_shape=jax.ShapeDtypeStruct((db, dec_seq, ATT_WIDTH), F32),
        compiler_params=pltpu.CompilerParams(dimension_semantics=("parallel", "arbitrary"),
                                             vmem_limit_bytes=VMEM_LIMIT),
        name="moba_sample",
    )(page_table, *([pool_kt] * pps), *([pool_vt] * pps), aq, ak, av, s_own, s_near, s_far)


def _mlstm_kernel(q_ref, k_ref, v_ref, g_ref, mo_ref, nw_ref, c0_ref, n0_ref, m0_ref,
                  h_ref, c_out, n_out, m_out, c_s, n_s, m_s, *, rows, chunk, batch_rows):
    ci = pl.program_id(1)
    L = chunk

    @pl.when(ci == 0)
    def _():
        c_s[...] = c0_ref[...]
        n_s[...] = n0_ref[...]
        m_s[...] = m0_ref[...]

    def padded(ref, bi, fill=None):
        x = ref[bi]
        if rows == L:
            return x
        tail = jnp.zeros((L - rows, x.shape[1]), F32) if fill is None else fill
        return jnp.concatenate([x, tail], axis=0)

    lane = lax.broadcasted_iota(jnp.int32, (L, LANES), 1)
    pad_lane = lax.broadcasted_iota(jnp.int32, (max(L - rows, 1), LANES), 1)

    is_lf = (lane >= N_ML_HEADS) & (lane < 2 * N_ML_HEADS)
    tri_r = lax.broadcasted_iota(jnp.int32, (L, L), 0)
    tri_c = lax.broadcasted_iota(jnp.int32, (L, L), 1)
    causal = tri_c <= tri_r
    ones_b = jnp.ones((L, LANES), BF16)
    reps = L // LANES
    wide = lambda x: x if reps == 1 else jnp.concatenate([x] * reps, axis=1)
    for bi in range(batch_rows):
        g = padded(g_ref, bi, jnp.where(pad_lane < N_ML_HEADS, NEG, 0.0))
        q_all, k_all, v_all, mo_all = padded(q_ref, bi), padded(k_ref, bi), padded(v_ref, bi), padded(mo_ref, bi)
        g3 = jnp.concatenate(_split_bf16(jnp.where(is_lf, g, 0.0), 3), axis=1)
        b3 = jnp.dot(causal.astype(BF16), g3, preferred_element_type=F32)
        bcum = b3[:, :LANES] + (b3[:, LANES:2 * LANES] + b3[:, 2 * LANES:])
        g_t = g.T
        b_t = bcum.T

        for h in range(N_ML_HEADS):
            sl = slice(h * ML_HEAD_DIM, (h + 1) * ML_HEAD_DIM)
            q, k, v = q_all[:, sl], k_all[:, sl], v_all[:, sl]
            ig = jnp.broadcast_to(g[:, h:h + 1], (L, LANES))
            b = jnp.broadcast_to(bcum[:, N_ML_HEADS + h:N_ML_HEADS + h + 1], (L, LANES))
            ig_row = g_t[h:h + 1, :]
            b_row = b_t[N_ML_HEADS + h:N_ML_HEADS + h + 1, :]
            m_prev = m_s[bi, h]
            C = c_s[bi, h]
            n_rep = n_s[bi, h]

            a = b + m_prev
            log_d = jnp.where(causal, wide(b) - b_row + ig_row, NEG)
            m_t = jnp.maximum(a, jnp.max(log_d, axis=-1, keepdims=True))
            d = jnp.exp(log_d - wide(m_t))
            w_state = jnp.exp(a - m_t)
            qb, kb, vb = q.astype(BF16), k.astype(BF16), v.astype(BF16)
            s = _nt_dot(qb, kb) * d
            v_ones = jnp.concatenate([vb, ones_b], axis=1)
            from_state = jnp.dot(qb, jnp.concatenate([C, n_rep], axis=1).astype(BF16), preferred_element_type=F32)
            from_chunk = jnp.dot(s.astype(BF16), v_ones, preferred_element_type=F32)
            num = w_state * from_state[:, :ML_HEAD_DIM] + from_chunk[:, :ML_HEAD_DIM]
            den = w_state * from_state[:, ML_HEAD_DIM:] + from_chunk[:, ML_HEAD_DIM:]
            hid = num / jnp.maximum(jnp.abs(den), jnp.exp(-m_t))

            hn = _rms(hid, nw_ref[:, sl])
            h_ref[bi, :, sl] = (hn * jax.nn.sigmoid(mo_all[:, sl]))[:rows]

            b_last = b[L - 1:L, :]
            m_new = m_t[L - 1:L, :]
            g_state = jnp.exp(b_last + m_prev - m_new)
            kg = k * jnp.exp(b_last - b + ig - m_new)
            added = _tn_dot(kg.astype(BF16), v_ones)
            c_s[bi, h] = g_state * C + added[:, :ML_HEAD_DIM]
            n_s[bi, h] = g_state * n_rep + added[:, ML_HEAD_DIM:]
            m_s[bi, h] = m_new

    @pl.when(ci == pl.num_programs(1) - 1)
    def _():
        c_out[...] = c_s[...]
        n_out[...] = n_s[...]
        m_out[...] = m_s[...]


def _mlstm(mq, mk, mv, gates, mo, ml_norm, c0, n0, m0, chunk, batch_rows):
    b, s, _ = mq.shape
    rows = min(s, chunk)
    n_chunks = s // rows
    assert b % batch_rows == 0
    bb = batch_rows
    tok = lambda w: pl.BlockSpec((bb, rows, w), lambda bi, ci: (bi, ci, 0))
    st4 = lambda shape: pl.BlockSpec(shape, lambda bi, ci: (bi, 0, 0, 0))
    c_shape, m_shape = (bb, N_ML_HEADS, ML_HEAD_DIM, ML_HEAD_DIM), (bb, N_ML_HEADS, 1, LANES)
    n_shape = (bb, N_ML_HEADS, ML_HEAD_DIM, LANES)
    return pl.pallas_call(
        functools.partial(_mlstm_kernel, rows=rows, chunk=chunk, batch_rows=bb),
        grid=(b // bb, n_chunks),
        in_specs=[tok(ML_WIDTH), tok(ML_WIDTH), tok(ML_WIDTH), tok(LANES), tok(ML_WIDTH),
                  pl.BlockSpec((1, ML_WIDTH), lambda bi, ci: (0, 0)),
                  st4(c_shape), st4(n_shape), st4(m_shape)],
        out_specs=[tok(ML_WIDTH), st4(c_shape), st4(n_shape), st4(m_shape)],
        out_shape=[jax.ShapeDtypeStruct((b, s, ML_WIDTH), F32),
                   jax.ShapeDtypeStruct((b,) + c_shape[1:], F32),
                   jax.ShapeDtypeStruct((b,) + n_shape[1:], F32),
                   jax.ShapeDtypeStruct((b,) + m_shape[1:], F32)],
        scratch_shapes=[pltpu.VMEM(c_shape, F32), pltpu.VMEM(n_shape, F32), pltpu.VMEM(m_shape, F32)],
        compiler_params=pltpu.CompilerParams(dimension_semantics=("parallel", "arbitrary"),
                                             vmem_limit_bytes=VMEM_LIMIT),
        name="mlstm",
    )(mq, mk, mv, gates, mo, ml_norm, c0, n0, m0)


def _out_ffn_kernel(x_ref, att_ref, ml_ref, wo_ref, gf_ref, wu_ref, wd_ref, gl_ref, y_ref, *, ff_chunk):
    x1 = x_ref[...] \
        + jnp.dot(att_ref[...].astype(BF16), wo_ref[0:ATT_WIDTH, :], preferred_element_type=F32) \
        + jnp.dot(ml_ref[...].astype(BF16), wo_ref[ATT_WIDTH:, :], preferred_element_type=F32)
    xb = _rms(x1, gf_ref[...]).astype(BF16)
    ffn = None
    for c in range(D_FF // ff_chunk):
        sl = slice(c * ff_chunk, (c + 1) * ff_chunk)
        u = jnp.maximum(jnp.dot(xb, wu_ref[:, sl], preferred_element_type=F32), 0.0)
        d = jnp.dot((u * u).astype(BF16), wd_ref[sl, :], preferred_element_type=F32)
        ffn = d if ffn is None else ffn + d
    y_ref[...] = _rms(x1 + ffn, gl_ref[...])


def _out_ffn(x2, att, ml, w_out, norm_ffn, w_up, w_down, norm_final, tm, ff_chunk):
    n = x2.shape[0]
    const = lambda shape: pl.BlockSpec(shape, lambda i: (0, 0), pipeline_mode=pl.Buffered(1))
    row = lambda w: pl.BlockSpec((tm, w), lambda i: (i, 0))
    return pl.pallas_call(
        functools.partial(_out_ffn_kernel, ff_chunk=ff_chunk),
        grid=(n // tm,),
        in_specs=[row(D_MODEL), row(ATT_WIDTH), row(ML_WIDTH), const((D_MODEL, D_MODEL)), const((1, D_MODEL)),
                  const((D_MODEL, D_FF)), const((D_FF, D_MODEL)), const((1, D_MODEL))],
        out_specs=row(D_MODEL),
        out_shape=jax.ShapeDtypeStruct((n, D_MODEL), F32),
        compiler_params=pltpu.CompilerParams(dimension_semantics=("parallel",), vmem_limit_bytes=VMEM_LIMIT),
        name="out_proj_ffn",
    )(x2, att, ml, w_out, norm_ffn, w_up, w_down, norm_final)


def _layer_weights(l, norm_mix, w_in, b_ig, b_fg, ml_norm, w_out, norm_ffn, w_up, w_down):
    w_gate = jnp.pad(w_in[l, :, GATE_COL0:], ((0, 0), (0, LANES - 2 * N_ML_HEADS)))
    w_all = jnp.concatenate([w_in[l, :, :GATE_COL0].astype(BF16)] + _split_bf16(w_gate, 2), axis=1)
    b_gate = jnp.pad(jnp.concatenate([b_ig[l], b_fg[l]]), (0, LANES - 2 * N_ML_HEADS))[None, :]
    return dict(norm_mix=norm_mix[l][None, :], w_all=w_all, b_gate=b_gate,
                ml_norm=ml_norm[l][None, :], w_out=w_out[l].astype(BF16), norm_ffn=norm_ffn[l][None, :],
                w_up=w_up[l].astype(BF16), w_down=w_down[l].astype(BF16))


def _trunk_layer(x, w, attend, state0, norm_final, tiles):
    b, s, _ = x.shape
    x2 = x.reshape(b * s, D_MODEL)
    aq, ak, av, mq, mk, mv, mo, gates, *kv_t = _project(x2, w["norm_mix"], w["w_all"], w["b_gate"], tiles.tm, s)
    r3 = lambda a: a.reshape(b, s, a.shape[-1])
    if kv_t:
        k_out, v_out = (a.reshape(b, N_ATT_HEADS, ATT_HEAD_DIM, s).transpose(0, 3, 1, 2) for a in kv_t)
    else:
        k_out, v_out = (a.reshape(b, s, N_ATT_HEADS, ATT_HEAD_DIM) for a in (ak, av))
    att = attend(r3(aq), r3(ak), r3(av))
    c0, n0, m0 = state0
    ml, c_new, n_new, m_new = _mlstm(r3(mq), r3(mk), r3(mv), r3(gates), r3(mo), w["ml_norm"], c0, n0, m0, tiles.ml_chunk,
                                     tiles.ml_batch_rows)
    y = _out_ffn(x2, att.reshape(b * s, ATT_WIDTH), ml.reshape(b * s, ML_WIDTH), w["w_out"], w["norm_ffn"],
                 w["w_up"], w["w_down"], norm_final, tiles.tm, tiles.ff_chunk)
    return y.reshape(b, s, D_MODEL), k_out, v_out, c_new, n_new[:, :, :, 0], m_new[:, :, 0, 0]


def kernel(x_prompt, x_sample, cache_k, cache_v, state_C, state_n, state_m, page_table, rel_bias, norm_mix, w_in,
           b_ig, b_fg, ml_norm, w_out, norm_ffn, w_up, w_down, norm_final):
    depth = w_in.shape[0]
    assert depth == 1, "the fused final RMSNorm assumes a single layer"
    bp, sp, _ = x_prompt.shape
    db, dec_seq, _ = x_sample.shape
    n_phys = cache_k.shape[1]
    t0, t1, s_own, s_near, s_far = _bias_tiles(rel_bias, dec_seq)
    gl = norm_final[None, :]
    l = 0
    w = _layer_weights(l, norm_mix, w_in, b_ig, b_fg, ml_norm, w_out, norm_ffn, w_up, w_down)

    zero_state = (jnp.zeros((bp, N_ML_HEADS, ML_HEAD_DIM, ML_HEAD_DIM), F32),
                  jnp.zeros((bp, N_ML_HEADS, ML_HEAD_DIM, LANES), F32), jnp.zeros((bp, N_ML_HEADS, 1, LANES), F32))
    yp, kp, vp, cp, np_, mp = _trunk_layer(
        x_prompt, w, lambda q, k, v: _moba_prompt(q, k, v, t0, t1, rel_bias, PROMPT_KEY_GROUP, PROMPT_HEADS_PER_STEP),
        zero_state, gl,
        PROMPT_TILES)

    to_pool = lambda a: jnp.transpose(a, (0, 1, 3, 4, 2)).reshape(depth * n_phys, ATT_WIDTH, PAGE_SIZE)
    pool_kt, pool_vt = to_pool(cache_k), to_pool(cache_v)
    pt = page_table + l * n_phys

    def attend_sample(q, k, v):
        return _moba_sample(q, k, v, pool_kt, pool_vt, pt, s_own, s_near, s_far, SAMPLE_BLOCKS_PER_STEP)

    state0 = (state_C[l], jnp.broadcast_to(state_n[l][:, :, :, None], (db, N_ML_HEADS, ML_HEAD_DIM, LANES)),
              jnp.broadcast_to(state_m[l][:, :, None, None], (db, N_ML_HEADS, 1, LANES)))
    ys, ks, vs, cs, ns, ms = _trunk_layer(x_sample, w, attend_sample, state0, gl, SAMPLE_TILES)

    st = lambda a: a[None]
    return (yp, ys, st(kp), st(vp), st(cp), st(np_), st(mp), st(ks), st(vs), st(cs), st(ns), st(ms))
```

```python
import functools
import math
from typing import NamedTuple

import numpy as np
import jax
import jax.numpy as jnp
from jax import lax
from jax.experimental import pallas as pl
from jax.experimental.pallas import tpu as pltpu

F32 = jnp.float32
BF16 = jnp.bfloat16

D_MODEL = 1024
ATT_HEAD_DIM = 64
N_ATT_HEADS = 8
ATT_WIDTH = 512
N_ML_HEADS = 4
ML_HEAD_DIM = 128
ML_WIDTH = 512
SLAB = 512
N_SLABS = 7
GATE_COL0 = N_SLABS * SLAB
MOBA_BLOCK = 256
MOBA_TOPK = 3
PAGE_SIZE = 128
NUM_BUCKETS = 32
MAX_DISTANCE = 128
D_FF = 4 * D_MODEL
RMS_EPS = 1e-6
NEG = -1e30
LOG2E = math.log2(math.e)
LANES = 128
SUM_ROWS = 16
MAX_SLACK = 64.0
BOUND_INFLATE = 1.05
BF16_LOGIT_ERR = 2.0 ** -6
VMEM_LIMIT = 56 * 1024 * 1024


class _Tiles(NamedTuple):
    tm: int
    ml_chunk: int
    ml_batch_rows: int
    ff_chunk: int = 1024


PROMPT_TILES = _Tiles(tm=512, ml_chunk=512, ml_batch_rows=4)
SAMPLE_TILES = _Tiles(tm=256, ml_chunk=128, ml_batch_rows=4)
PROMPT_KEY_GROUP = 4
PROMPT_HEADS_PER_STEP = 4
SAMPLE_BLOCKS_PER_STEP = 8
PAGE_RING = 3


def _nt_dot(a, b):
    return lax.dot_general(a, b, (((1,), (1,)), ((), ())), preferred_element_type=F32)


def _split_bf16(x, parts):
    out = []
    for _ in range(parts):
        p = x.astype(BF16)
        out.append(p)
        x = x - p.astype(F32)
    return out


def _dot_x3(a, b, contract):
    (ah, al), (bh, bl) = _split_bf16(a, 2), _split_bf16(b, 2)
    d = lambda x, y: lax.dot_general(x, y, (contract, ((), ())), preferred_element_type=F32)
    return d(ah, bh) + (d(ah, bl) + d(al, bh))


def _tn_dot(a, b):
    return lax.dot_general(a, b, (((0,), (0,)), ((), ())), preferred_element_type=F32)


def _bucket_upper_bounds():
    n = np.arange(0, MAX_DISTANCE + 1)
    max_exact = NUM_BUCKETS // 2
    nf = np.maximum(n, 1).astype(np.float32)
    large = max_exact + (np.log(nf / np.float32(max_exact)) / np.float32(math.log(MAX_DISTANCE / max_exact))
                         * np.float32(NUM_BUCKETS - max_exact)).astype(np.int32)
    large = np.minimum(large, NUM_BUCKETS - 1)
    bucket = np.where(n < max_exact, n, large)
    return np.array([int(n[bucket <= b].max()) for b in range(NUM_BUCKETS - 1)], dtype=np.int32)


_BUCKET_UPPER = _bucket_upper_bounds()


def _bias_of_dist(dist, h, rb_ref):
    acc = jnp.full(dist.shape, rb_ref[NUM_BUCKETS - 1, h], F32)
    for b in range(NUM_BUCKETS - 2, -1, -1):
        acc = jnp.where(dist <= int(_BUCKET_UPPER[b]), rb_ref[b, h], acc)
    return acc


def _bias_tiles_kernel(rb_ref, t0_ref, t1_ref, s_own_ref, s_near_ref, s_far_ref, *, dec_seq):
    h = pl.program_id(0)
    kk = lax.broadcasted_iota(jnp.int32, (MOBA_BLOCK, MOBA_BLOCK), 0)
    qq = lax.broadcasted_iota(jnp.int32, (MOBA_BLOCK, MOBA_BLOCK), 1)
    d0 = qq - kk
    t0_ref[0] = jnp.where(d0 >= 0, _bias_of_dist(jnp.maximum(d0, 0), h, rb_ref) * LOG2E, NEG)
    t1_ref[0] = _bias_of_dist(d0 + MOBA_BLOCK, h, rb_ref) * LOG2E
    t = lax.broadcasted_iota(jnp.int32, (dec_seq, LANES), 0)
    cc = lax.broadcasted_iota(jnp.int32, (dec_seq, LANES), 1)
    ds = t - cc
    s_own_ref[0] = jnp.where((ds >= 0) & (cc < dec_seq),
                             _bias_of_dist(jnp.maximum(ds, 0), h, rb_ref) * LOG2E, NEG)
    s_near_ref[0] = _bias_of_dist(ds + PAGE_SIZE, h, rb_ref) * LOG2E
    s_far_ref[0] = jnp.full((dec_seq, LANES), rb_ref[NUM_BUCKETS - 1, h] * LOG2E, F32)


def _bias_tiles(rel_bias, dec_seq):
    smem = pl.BlockSpec(memory_space=pltpu.SMEM)
    big = pl.BlockSpec((1, MOBA_BLOCK, MOBA_BLOCK), lambda h: (h, 0, 0))
    small = pl.BlockSpec((1, dec_seq, LANES), lambda h: (h, 0, 0))
    return pl.pallas_call(
        functools.partial(_bias_tiles_kernel, dec_seq=dec_seq),
        grid=(N_ATT_HEADS,),
        in_specs=[smem],
        out_specs=[big, big, small, small, small],
        out_shape=[jax.ShapeDtypeStruct((N_ATT_HEADS, MOBA_BLOCK, MOBA_BLOCK), F32)] * 2
        + [jax.ShapeDtypeStruct((N_ATT_HEADS, dec_seq, LANES), F32)] * 3,
        name="bias_tiles",
    )(rel_bias)


def _rms(x, g):
    return x * lax.rsqrt(jnp.mean(x * x, axis=-1, keepdims=True) + RMS_EPS) * g


def _log_sigmoid(z):
    return jnp.minimum(z, 0.0) - jnp.log1p(jnp.exp(-jnp.abs(z)))


def _proj_kernel(x_ref, g_ref, w_ref, bg_ref, aq_ref, ak_ref, av_ref, mq_ref, mk_ref, mv_ref, mo_ref,
                 gate_ref, *kv_t_refs):
    xn = _rms(x_ref[...], g_ref[...])
    xb = xn.astype(BF16)
    outs = (aq_ref, ak_ref, av_ref, mq_ref, mk_ref, mv_ref, mo_ref)
    scales = (ATT_HEAD_DIM ** -0.5 * LOG2E, None, None, None, ML_HEAD_DIM ** -0.5, None, None)
    for i, (o_ref, s) in enumerate(zip(outs, scales)):
        r = jnp.dot(xb, w_ref[:, i * SLAB:(i + 1) * SLAB], preferred_element_type=F32)
        o_ref[...] = r if s is None else r * s
        if kv_t_refs and i in (1, 2):
            kv_t_refs[i - 1][0] = r.T
    gate_hi_lo = jnp.dot(xb, w_ref[:, GATE_COL0:GATE_COL0 + 2 * LANES], preferred_element_type=F32)
    x_lo = (xn - xb.astype(F32)).astype(BF16)
    z = gate_hi_lo[:, :LANES] + (gate_hi_lo[:, LANES:]
                                 + jnp.dot(x_lo, w_ref[:, GATE_COL0:GATE_COL0 + LANES], preferred_element_type=F32))
    z = z + bg_ref[...]
    lane = lax.broadcasted_iota(jnp.int32, z.shape, 1)
    gate_ref[...] = jnp.where((lane >= N_ML_HEADS) & (lane < 2 * N_ML_HEADS), _log_sigmoid(z), z)


def _project(x2, norm_g, w_all, b_gate, tm, seq):
    n = x2.shape[0]
    const = lambda shape: pl.BlockSpec(shape, lambda i: (0, 0), pipeline_mode=pl.Buffered(1))
    slab = pl.BlockSpec((tm, SLAB), lambda i: (i, 0))
    out_specs = [slab] * N_SLABS + [pl.BlockSpec((tm, LANES), lambda i: (i, 0))]
    out_shape = [jax.ShapeDtypeStruct((n, SLAB), F32)] * N_SLABS + [jax.ShapeDtypeStruct((n, LANES), F32)]
    if seq % tm == 0:
        tiles = seq // tm
        out_specs += [pl.BlockSpec((1, SLAB, tm), lambda i: (i // tiles, 0, i % tiles))] * 2
        out_shape += [jax.ShapeDtypeStruct((n // seq, SLAB, seq), F32)] * 2
    return pl.pallas_call(
        _proj_kernel,
        grid=(n // tm,),
        in_specs=[pl.BlockSpec((tm, D_MODEL), lambda i: (i, 0)), const((1, D_MODEL)),
                  const((D_MODEL, N_SLABS * SLAB + 2 * LANES)), const((1, LANES))],
        out_specs=out_specs,
        out_shape=out_shape,
        compiler_params=pltpu.CompilerParams(dimension_semantics=("parallel",), vmem_limit_bytes=VMEM_LIMIT),
        name="rms_in_proj",
    )(x2, norm_g, w_all, b_gate)


def _topk_penalty(scores, n_valid):
    lane = lax.broadcasted_iota(jnp.int32, scores.shape, 1)
    rank = jnp.zeros(scores.shape, jnp.int32)
    for n in range(n_valid):
        col = scores[:, n:n + 1]
        rank = rank + jnp.where((col > scores) | ((col == scores) & (lane > n)), 1, 0)
    return jnp.where((rank < MOBA_TOPK) & (lane < n_valid), 0.0, NEG)


def _moba_prompt_kernel(rb_ref, q_ref, k_ref, v_ref, t0_ref, t1_ref, o_ref, km_ref, kb_ref, vt_ref, sc_ref,
                        pen_ref, sp_ref, rad_ref, cen_ref, m_ref, *, n_blocks, group, heads):
    hp = pl.program_id(1)
    own = pl.program_id(2)
    blk = MOBA_BLOCK
    nbp = km_ref.shape[0]
    gl = heads * ATT_HEAD_DIM

    sel_l = lax.broadcasted_iota(jnp.int32, (gl, LANES), 0)
    sel_c = lax.broadcasted_iota(jnp.int32, (gl, LANES), 1)
    head_sel = jnp.where(sel_l // ATT_HEAD_DIM == sel_c, 1.0, 0.0).astype(BF16)

    @pl.when(own == 0)
    def _():
        km_ref[...] = jnp.zeros(km_ref.shape, F32)
        rad_ref[...] = jnp.zeros(rad_ref.shape, F32)
        cen_ref[...] = jnp.zeros(cen_ref.shape, F32)
        for n in range(n_blocks):
            kn = k_ref[0, n * blk:(n + 1) * blk, :]
            cn = jnp.mean(kn, axis=0, keepdims=True)
            km_ref[n:n + 1, :] = cn
            kb_ref[n] = kn.astype(BF16)
            dk = kn - cn
            rad_ref[n:n + 1, :] = jnp.max(jnp.dot((dk * dk).astype(BF16), head_sel, preferred_element_type=F32),
                                          axis=0, keepdims=True)
            cen_ref[n:n + 1, :] = jnp.dot(jnp.broadcast_to(cn * cn, (8, gl)).astype(BF16), head_sel,
                                          preferred_element_type=F32)[0:1]
            vtn = v_ref[0, n * blk:(n + 1) * blk, :].T.astype(BF16)
            for hh in range(heads):
                vt_ref[n, hh, 0:ATT_HEAD_DIM, :] = vtn[hh * ATT_HEAD_DIM:(hh + 1) * ATT_HEAD_DIM]
                vt_ref[n, hh, ATT_HEAD_DIM:, :] = jnp.ones((SUM_ROWS, blk), BF16)

    q2 = q_ref[0]
    lane = lax.broadcasted_iota(jnp.int32, q2.shape, 1)
    q_hi, q_lo = _split_bf16(q2, 2)
    qcat_b = jnp.concatenate([jnp.where((lane // ATT_HEAD_DIM) == hh, q_hi, jnp.zeros_like(q_hi))
                              for hh in range(heads)], axis=0)
    wide = heads * blk
    lane_m = lax.broadcasted_iota(jnp.int32, (nbp, gl), 1)
    km2 = jnp.concatenate([jnp.where((lane_m // ATT_HEAD_DIM) == hh, km_ref[...], 0.0) for hh in range(heads)], axis=0)
    km_hi, km_lo = _split_bf16(km2, 2)
    hi_q = _nt_dot(jnp.concatenate([km_hi, km_lo], axis=0), q_hi)
    sc2 = hi_q[:heads * nbp] + (hi_q[heads * nbp:] + _nt_dot(km_hi, q_lo))
    sc_ref[...] = jnp.concatenate([sc2[hh * nbp:(hh + 1) * nbp] for hh in range(heads)], axis=1)

    sub = lax.broadcasted_iota(jnp.int32, (nbp, wide), 0)

    def rank_body(n, rank):
        sc = sc_ref[...]
        row = sc_ref[pl.ds(n, 1), :]
        return rank + jnp.where((row > sc) | ((row == sc) & (sub > n)), 1, 0)

    rank = lax.fori_loop(0, own, rank_body, jnp.zeros((nbp, wide), jnp.int32))
    pen_ref[...] = jnp.where((rank < MOBA_TOPK) & (sub < own), 0.0, NEG)

    far_bias = jnp.concatenate([jnp.full((1, blk), rb_ref[NUM_BUCKETS - 1, heads * hp + hh] * LOG2E, F32)
                                for hh in range(heads)], axis=1)

    def fold8(x, op):
        return op(x.reshape(blk // 8, 8, x.shape[1]), axis=0)

    def far_row(j):
        return jnp.where(j < own - 1, pen_ref[pl.ds(j, 1), :] + far_bias, NEG)

    grp = group
    n_groups = (jnp.maximum(own - 1, 0) + grp - 1) // grp
    jn = jnp.maximum(own - 1, 0)
    near_row = jnp.where(own >= 1, pen_ref[pl.ds(jn, 1), :], NEG)

    t0c = jnp.concatenate([t0_ref[hh] for hh in range(heads)], axis=1)
    t1c = jnp.concatenate([t1_ref[hh] for hh in range(heads)], axis=1)
    no_prev = own == 0
    s_pair = _nt_dot(kb_ref[pl.ds(jn, 2)].reshape(2 * blk, gl), qcat_b)
    s_first = s_pair[:blk] + jnp.where(no_prev, t0c, t1c + near_row)
    s_second = jnp.where(no_prev, NEG, s_pair[blk:] + t0c)
    sp_ref[0:blk, :] = s_first
    sp_ref[blk:, :] = s_second
    m8 = jnp.maximum(fold8(s_first, jnp.max), fold8(s_second, jnp.max))
    pair_max = jnp.max(m8, axis=0, keepdims=True)

    row8 = lax.broadcasted_iota(jnp.int32, (8, gl), 0)
    lane8 = lax.broadcasted_iota(jnp.int32, (8, gl), 1)
    q_sq = _nt_dot(jnp.where(lane8 // ATT_HEAD_DIM == row8, 1.0, 0.0).astype(BF16),
                   (q2 * q2).astype(BF16))
    q_norm = jnp.sqrt(jnp.concatenate([q_sq[hh:hh + 1] for hh in range(heads)], axis=1))
    per_head = lambda x: jnp.concatenate([jnp.broadcast_to(x[:, hh:hh + 1], (nbp, blk)) for hh in range(heads)], axis=1)
    radius, centre = per_head(jnp.sqrt(rad_ref[...])), per_head(jnp.sqrt(cen_ref[...]))
    far_pen = jnp.where(sub < own - 1, pen_ref[...], NEG)
    mean_logit = sc_ref[...] + far_bias + far_pen
    spread = q_norm * (BOUND_INFLATE * radius + BF16_LOGIT_ERR * (centre + radius))
    far_low = jnp.max(mean_logit, axis=0, keepdims=True)
    far_high = jnp.max(mean_logit + spread, axis=0, keepdims=True)
    m_ref[...] = jnp.maximum(pair_max, far_high)
    slack = jnp.maximum(pair_max, far_high) - jnp.maximum(pair_max, far_low)

    @pl.when(jnp.max(slack) > MAX_SLACK)
    def _():
        def max_body(g, m8c):
            j0 = pl.multiple_of(g * grp, grp)
            sg = _nt_dot(kb_ref[pl.ds(j0, grp)].reshape(grp * blk, gl), qcat_b)
            for i in range(grp):
                m8c = jnp.maximum(m8c, fold8(sg[i * blk:(i + 1) * blk], jnp.max) + far_row(j0 + i))
            return m8c

        m_ref[...] = jnp.max(lax.fori_loop(0, n_groups, max_body, m8), axis=0, keepdims=True)

    m = m_ref[...]

    def accumulate(j, p, accs):
        pb = p.astype(BF16)
        return tuple(accs[hh] + jnp.dot(vt_ref[j, hh], pb[:, hh * blk:(hh + 1) * blk], preferred_element_type=F32)
                     for hh in range(heads))

    pb_pair = jnp.exp2(sp_ref[...] - m).astype(BF16)
    vt_pair = vt_ref[pl.ds(jn, 2)]
    accs = tuple(jnp.dot(jnp.concatenate([vt_pair[i, hh] for i in range(2)], axis=1),
                         pb_pair[:, hh * blk:(hh + 1) * blk], preferred_element_type=F32) for hh in range(heads))

    def exp_body(g, accc):
        j0 = pl.multiple_of(g * grp, grp)
        sg = _nt_dot(kb_ref[pl.ds(j0, grp)].reshape(grp * blk, gl), qcat_b)
        for i in range(grp):
            accc = accumulate(j0 + i, jnp.exp2(sg[i * blk:(i + 1) * blk] + (far_row(j0 + i) - m)), accc)
        return accc

    accs = lax.fori_loop(0, n_groups, exp_body, accs)
    out_t = jnp.concatenate([a[:ATT_HEAD_DIM] / a[ATT_HEAD_DIM:ATT_HEAD_DIM + 1] for a in accs], axis=0)
    o_ref[0] = out_t.T


def _moba_prompt(aq, ak, av, t0, t1, rel_bias, group, heads):
    b, s, _ = aq.shape
    nb = s // MOBA_BLOCK
    nbp = -(-nb // 8) * 8
    n_groups = N_ATT_HEADS // heads
    gl = heads * ATT_HEAD_DIM
    wide = heads * MOBA_BLOCK
    assert nb % group == 0
    qspec = pl.BlockSpec((1, MOBA_BLOCK, gl), lambda bi, hp, qi: (bi, qi, hp))
    kvspec = pl.BlockSpec((1, s, gl), lambda bi, hp, qi: (bi, 0, hp))
    tspec = pl.BlockSpec((heads, MOBA_BLOCK, MOBA_BLOCK), lambda bi, hp, qi: (hp, 0, 0))
    return pl.pallas_call(
        functools.partial(_moba_prompt_kernel, n_blocks=nb, group=group, heads=heads),
        grid=(b, n_groups, nb),
        in_specs=[pl.BlockSpec(memory_space=pltpu.SMEM), qspec, kvspec, kvspec, tspec, tspec],
        out_specs=qspec,
        out_shape=jax.ShapeDtypeStruct((b, s, ATT_WIDTH), F32),
        scratch_shapes=[pltpu.VMEM((nbp, gl), F32), pltpu.VMEM((nb, MOBA_BLOCK, gl), BF16),
                        pltpu.VMEM((nb, heads, ATT_HEAD_DIM + SUM_ROWS, MOBA_BLOCK), BF16),
                        pltpu.VMEM((nbp, wide), F32), pltpu.VMEM((nbp, wide), F32),
                        pltpu.VMEM((2 * MOBA_BLOCK, wide), F32), pltpu.VMEM((nbp, LANES), F32),
                        pltpu.VMEM((nbp, LANES), F32), pltpu.VMEM((1, wide), F32)],
        compiler_params=pltpu.CompilerParams(dimension_semantics=("parallel", "parallel", "arbitrary"),
                                             vmem_limit_bytes=VMEM_LIMIT),
        name="moba_prompt",
    )(rel_bias, aq, ak, av, t0, t1)


def _moba_sample_kernel(pt_ref, pool_k_ref, pool_v_ref, q_ref, kn_ref, vn_ref, s_own_ref, s_near_ref, s_far_ref, o_ref,
                        qbd_ref, kmt_ref, mm_ref, ll_ref, oo_ref, own_m_ref, own_l_ref, own_o_ref,
                        kbuf_ref, vbuf_ref, sem_ref, *, blocks_per_step, n_blocks, dec_seq):
    ppb = MOBA_BLOCK // PAGE_SIZE
    pps = blocks_per_step * ppb
    c = pl.program_id(1)
    n_steps = pl.num_programs(1)
    step = pl.program_id(0) * n_steps + c
    total = pl.num_programs(0) * n_steps

    def page_copies(t):
        tb, tc, slot = t // n_steps, t % n_steps, t % PAGE_RING
        copies = []
        for i in range(pps):
            page = pt_ref[tb, tc * pps + i]
            copies.append(pltpu.make_async_copy(pool_k_ref.at[page], kbuf_ref.at[slot, i], sem_ref.at[0, slot]))
            copies.append(pltpu.make_async_copy(pool_v_ref.at[page], vbuf_ref.at[slot, i], sem_ref.at[1, slot]))
        return copies

    @pl.when(step == 0)
    def _():
        for t in range(PAGE_RING - 1):
            for cp in page_copies(t):
                cp.start()

    @pl.when(step + (PAGE_RING - 1) < total)
    def _():
        for cp in page_copies(step + (PAGE_RING - 1)):
            cp.start()

    for cp in page_copies(step):
        cp.wait()
    slot = step % PAGE_RING
    rows = N_ATT_HEADS * dec_seq
    row_head = lax.broadcasted_iota(jnp.int32, (rows, ATT_WIDTH), 0) // dec_seq
    lane_head = lax.broadcasted_iota(jnp.int32, (rows, ATT_WIDTH), 1) // ATT_HEAD_DIM
    lane_r = lax.broadcasted_iota(jnp.int32, (rows, LANES), 1)
    lane_k = lax.broadcasted_iota(jnp.int32, (ATT_WIDTH, LANES), 1)

    def local_softmax(s):
        m = jnp.max(s, axis=-1, keepdims=True)
        p = jnp.exp2(s - m)
        return m, jnp.sum(p, axis=-1, keepdims=True), p.astype(BF16)

    @pl.when(c == 0)
    def _():
        qrep = jnp.concatenate([q_ref[0]] * N_ATT_HEADS, axis=0)
        qbd = jnp.where(row_head == lane_head, qrep, 0.0)
        qbd_ref[...] = qbd
        kmt_ref[...] = jnp.zeros(kmt_ref.shape, F32)
        mm_ref[...] = jnp.zeros(mm_ref.shape, F32)
        ll_ref[...] = jnp.zeros(ll_ref.shape, F32)
        pad = jnp.zeros((LANES - dec_seq, ATT_WIDTH), F32)
        kn = jnp.concatenate([kn_ref[0], pad], axis=0).astype(BF16)
        vn = jnp.concatenate([vn_ref[0], pad], axis=0).astype(BF16)
        m, l, p = local_softmax(_nt_dot(qbd.astype(BF16), kn) + s_own_ref[...].reshape(rows, LANES))
        own_m_ref[...] = jnp.broadcast_to(m, (rows, LANES))
        own_l_ref[...] = jnp.broadcast_to(l, (rows, LANES))
        own_o_ref[...] = jnp.dot(p, vn, preferred_element_type=F32)

    qb = qbd_ref[...].astype(BF16)
    s_near = s_near_ref[...].reshape(rows, LANES)
    s_far = s_far_ref[...].reshape(rows, LANES)
    n0 = c * blocks_per_step
    kt = jnp.concatenate([kbuf_ref[slot, i] for i in range(pps)], axis=1)
    s_all = jnp.dot(qb, kt.astype(BF16), preferred_element_type=F32)
    mm, ll, kmt = mm_ref[...], ll_ref[...], kmt_ref[...]
    for g in range(blocks_per_step):
        n = n0 + g
        bias = jnp.concatenate([s_far] * (ppb - 1) + [jnp.where(n == n_blocks - 1, s_near, s_far)], axis=1)
        m, l, p = local_softmax(s_all[:, g * MOBA_BLOCK:(g + 1) * MOBA_BLOCK] + bias)
        vt = jnp.concatenate([vbuf_ref[slot, ppb * g + i] for i in range(ppb)], axis=1).astype(BF16)
        oo_ref[n] = _nt_dot(p, vt)
        mm = jnp.where(lane_r == n, m, mm)
        ll = jnp.where(lane_r == n, l, ll)
        ksum = jnp.sum(kt[:, g * MOBA_BLOCK:(g + 1) * MOBA_BLOCK], axis=-1, keepdims=True)
        kmt = jnp.where(lane_k == n, ksum, kmt)
    mm_ref[...] = mm
    ll_ref[...] = ll
    kmt_ref[...] = kmt

    @pl.when(c == pl.num_programs(1) - 1)
    def _():
        kmean_t = kmt_ref[...] * (1.0 / MOBA_BLOCK)
        pen = _topk_penalty(_dot_x3(qbd_ref[...], kmean_t, ((1,), (0,))), n_blocks)
        mp = mm_ref[...] + pen
        m_own = own_m_ref[...]
        m_tot = jnp.maximum(jnp.max(mp, axis=-1, keepdims=True), m_own)
        w = jnp.exp2(mp - m_tot)
        w_own = jnp.exp2(m_own - m_tot)
        den = jnp.sum(w * ll_ref[...], axis=-1, keepdims=True) + w_own * own_l_ref[...]

        num = w_own[:, 0:1] * own_o_ref[...]
        for nn in range(n_blocks):
            num = num + w[:, nn:nn + 1] * oo_ref[nn]
        full = jnp.where(row_head == lane_head, num / den[:, 0:1], 0.0)
        out = full[0:dec_seq]
        for h in range(1, N_ATT_HEADS):
            out = out + full[h * dec_seq:(h + 1) * dec_seq]
        o_ref[0] = out


def _moba_sample(aq, ak, av, pool_kt, pool_vt, page_table, s_own, s_near, s_far, blocks_per_step):
    db, dec_seq, _ = aq.shape
    n_pages = page_table.shape[1]
    ppb = MOBA_BLOCK // PAGE_SIZE
    nb = n_pages // ppb
    pps = blocks_per_step * ppb
    assert nb % blocks_per_step == 0 and nb <= LANES
    rows = N_ATT_HEADS * dec_seq

    per_b = pl.BlockSpec((1, dec_seq, ATT_WIDTH), lambda b, c, pt: (b, 0, 0))
    tile = pl.BlockSpec((N_ATT_HEADS, dec_seq, LANES), lambda b, c, pt: (0, 0, 0))
    grid_spec = pltpu.PrefetchScalarGridSpec(
        num_scalar_prefetch=1, grid=(db, nb // blocks_per_step),
        in_specs=[pl.BlockSpec(memory_space=pl.ANY)] * 2 + [per_b, per_b, per_b, tile, tile, tile],
        out_specs=per_b,
        scratch_shapes=[pltpu.VMEM((rows, ATT_WIDTH), F32), pltpu.VMEM((ATT_WIDTH, LANES), F32),
                        pltpu.VMEM((rows, LANES), F32), pltpu.VMEM((rows, LANES), F32),
                        pltpu.VMEM((nb, rows, ATT_WIDTH), F32), pltpu.VMEM((rows, LANES), F32),
                        pltpu.VMEM((rows, LANES), F32), pltpu.VMEM((rows, ATT_WIDTH), F32),
                        pltpu.VMEM((PAGE_RING, pps, ATT_WIDTH, PAGE_SIZE), F32),
                        pltpu.VMEM((PAGE_RING, pps, ATT_WIDTH, PAGE_SIZE), F32),
                        pltpu.SemaphoreType.DMA((2, PAGE_RING))])
    return pl.pallas_call(
        functools.partial(_moba_sample_kernel, blocks_per_step=blocks_per_step, n_blocks=nb, dec_seq=dec_seq),
        grid_spec=grid_spec,
        out_shape=jax.ShapeDtypeStruct((db, dec_seq, ATT_WIDTH), F32),
        compiler_params=pltpu.CompilerParams(dimension_semantics=("arbitrary", "arbitrary"),
                                             vmem_limit_bytes=VMEM_LIMIT),
        name="moba_sample",
    )(page_table, pool_kt, pool_vt, aq, ak, av, s_own, s_near, s_far)


def _mlstm_kernel(q_ref, k_ref, v_ref, g_ref, mo_ref, nw_ref, c0_ref, n0_ref, m0_ref,
                  h_ref, c_out, n_out, m_out, c_s, n_s, m_s, *, rows, chunk, batch_rows):
    ci = pl.program_id(1)
    L = chunk

    @pl.when(ci == 0)
    def _():
        c_s[...] = c0_ref[...]
        n_s[...] = n0_ref[...]
        m_s[...] = m0_ref[...]

    def padded(ref, bi, fill=None):
        x = ref[bi]
        if rows == L:
            return x
        tail = jnp.zeros((L - rows, x.shape[1]), F32) if fill is None else fill
        return jnp.concatenate([x, tail], axis=0)

    lane = lax.broadcasted_iota(jnp.int32, (L, LANES), 1)
    pad_lane = lax.broadcasted_iota(jnp.int32, (max(L - rows, 1), LANES), 1)

    is_lf = (lane >= N_ML_HEADS) & (lane < 2 * N_ML_HEADS)
    tri_r = lax.broadcasted_iota(jnp.int32, (L, L), 0)
    tri_c = lax.broadcasted_iota(jnp.int32, (L, L), 1)
    causal = tri_c <= tri_r
    ones_b = jnp.ones((L, LANES), BF16)
    reps = L // LANES
    wide = lambda x: x if reps == 1 else jnp.concatenate([x] * reps, axis=1)
    for bi in range(batch_rows):
        g = padded(g_ref, bi, jnp.where(pad_lane < N_ML_HEADS, NEG, 0.0))
        q_all, k_all, v_all, mo_all = padded(q_ref, bi), padded(k_ref, bi), padded(v_ref, bi), padded(mo_ref, bi)
        g3 = jnp.concatenate(_split_bf16(jnp.where(is_lf, g, 0.0), 3), axis=1)
        b3 = jnp.dot(causal.astype(BF16), g3, preferred_element_type=F32)
        bcum = b3[:, :LANES] + (b3[:, LANES:2 * LANES] + b3[:, 2 * LANES:])
        g_t = g.T
        b_t = bcum.T

        for h in range(N_ML_HEADS):
            sl = slice(h * ML_HEAD_DIM, (h + 1) * ML_HEAD_DIM)
            q, k, v = q_all[:, sl], k_all[:, sl], v_all[:, sl]
            ig = jnp.broadcast_to(g[:, h:h + 1], (L, LANES))
            b = jnp.broadcast_to(bcum[:, N_ML_HEADS + h:N_ML_HEADS + h + 1], (L, LANES))
            ig_row = g_t[h:h + 1, :]
            b_row = b_t[N_ML_HEADS + h:N_ML_HEADS + h + 1, :]
            m_prev = m_s[bi, h]
            C = c_s[bi, h]
            n_rep = n_s[bi, h]

            a = b + m_prev
            log_d = jnp.where(causal, wide(b) - b_row + ig_row, NEG)
            m_t = jnp.maximum(a, jnp.max(log_d, axis=-1, keepdims=True))
            d = jnp.exp(log_d - wide(m_t))
            w_state = jnp.exp(a - m_t)
            qb, kb, vb = q.astype(BF16), k.astype(BF16), v.astype(BF16)
            s = _nt_dot(qb, kb) * d
            v_ones = jnp.concatenate([vb, ones_b], axis=1)
            from_state = jnp.dot(qb, jnp.concatenate([C, n_rep], axis=1).astype(BF16), preferred_element_type=F32)
            from_chunk = jnp.dot(s.astype(BF16), v_ones, preferred_element_type=F32)
            num = w_state * from_state[:, :ML_HEAD_DIM] + from_chunk[:, :ML_HEAD_DIM]
            den = w_state * from_state[:, ML_HEAD_DIM:] + from_chunk[:, ML_HEAD_DIM:]
            hid = num / jnp.maximum(jnp.abs(den), jnp.exp(-m_t))

            hn = _rms(hid, nw_ref[:, sl])
            h_ref[bi, :, sl] = (hn * jax.nn.sigmoid(mo_all[:, sl]))[:rows]

            b_last = b[L - 1:L, :]
            m_new = m_t[L - 1:L, :]
            g_state = jnp.exp(b_last + m_prev - m_new)
            kg = k * jnp.exp(b_last - b + ig - m_new)
            added = _tn_dot(kg.astype(BF16), v_ones)
            c_s[bi, h] = g_state * C + added[:, :ML_HEAD_DIM]
            n_s[bi, h] = g_state * n_rep + added[:, ML_HEAD_DIM:]
            m_s[bi, h] = m_new

    @pl.when(ci == pl.num_programs(1) - 1)
    def _():
        c_out[...] = c_s[...]
        n_out[...] = n_s[...]
        m_out[...] = m_s[...]


def _mlstm(mq, mk, mv, gates, mo, ml_norm, c0, n0, m0, chunk, batch_rows):
    b, s, _ = mq.shape
    rows = min(s, chunk)
    n_chunks = s // rows
    assert b % batch_rows == 0
    bb = batch_rows
    tok = lambda w: pl.BlockSpec((bb, rows, w), lambda bi, ci: (bi, ci, 0))
    st4 = lambda shape: pl.BlockSpec(shape, lambda bi, ci: (bi, 0, 0, 0))
    c_shape, m_shape = (bb, N_ML_HEADS, ML_HEAD_DIM, ML_HEAD_DIM), (bb, N_ML_HEADS, 1, LANES)
    n_shape = (bb, N_ML_HEADS, ML_HEAD_DIM, LANES)
    return pl.pallas_call(
        functools.partial(_mlstm_kernel, rows=rows, chunk=chunk, batch_rows=bb),
        grid=(b // bb, n_chunks),
        in_specs=[tok(ML_WIDTH), tok(ML_WIDTH), tok(ML_WIDTH), tok(LANES), tok(ML_WIDTH),
                  pl.BlockSpec((1, ML_WIDTH), lambda bi, ci: (0, 0)),
                  st4(c_shape), st4(n_shape), st4(m_shape)],
        out_specs=[tok(ML_WIDTH), st4(c_shape), st4(n_shape), st4(m_shape)],
        out_shape=[jax.ShapeDtypeStruct((b, s, ML_WIDTH), F32),
                   jax.ShapeDtypeStruct((b,) + c_shape[1:], F32),
                   jax.ShapeDtypeStruct((b,) + n_shape[1:], F32),
                   jax.ShapeDtypeStruct((b,) + m_shape[1:], F32)],
        scratch_shapes=[pltpu.VMEM(c_shape, F32), pltpu.VMEM(n_shape, F32), pltpu.VMEM(m_shape, F32)],
        compiler_params=pltpu.CompilerParams(dimension_semantics=("parallel", "arbitrary"),
                                             vmem_limit_bytes=VMEM_LIMIT),
        name="mlstm",
    )(mq, mk, mv, gates, mo, ml_norm, c0, n0, m0)


def _out_ffn_kernel(x_ref, att_ref, ml_ref, wo_ref, gf_ref, wu_ref, wd_ref, gl_ref, y_ref, *, ff_chunk):
    x1 = x_ref[...] \
        + jnp.dot(att_ref[...].astype(BF16), wo_ref[0:ATT_WIDTH, :], preferred_element_type=F32) \
        + jnp.dot(ml_ref[...].astype(BF16), wo_ref[ATT_WIDTH:, :], preferred_element_type=F32)
    xb = _rms(x1, gf_ref[...]).astype(BF16)
    ffn = None
    for c in range(D_FF // ff_chunk):
        sl = slice(c * ff_chunk, (c + 1) * ff_chunk)
        u = jnp.maximum(jnp.dot(xb, wu_ref[:, sl], preferred_element_type=F32), 0.0)
        d = jnp.dot((u * u).astype(BF16), wd_ref[sl, :], preferred_element_type=F32)
        ffn = d if ffn is None else ffn + d
    y_ref[...] = _rms(x1 + ffn, gl_ref[...])


def _out_ffn(x2, att, ml, w_out, norm_ffn, w_up, w_down, norm_final, tm, ff_chunk):
    n = x2.shape[0]
    const = lambda shape: pl.BlockSpec(shape, lambda i: (0, 0), pipeline_mode=pl.Buffered(1))
    row = lambda w: pl.BlockSpec((tm, w), lambda i: (i, 0))
    return pl.pallas_call(
        functools.partial(_out_ffn_kernel, ff_chunk=ff_chunk),
        grid=(n // tm,),
        in_specs=[row(D_MODEL), row(ATT_WIDTH), row(ML_WIDTH), const((D_MODEL, D_MODEL)), const((1, D_MODEL)),
                  const((D_MODEL, D_FF)), const((D_FF, D_MODEL)), const((1, D_MODEL))],
        out_specs=row(D_MODEL),
        out_shape=jax.ShapeDtypeStruct((n, D_MODEL), F32),
        compiler_params=pltpu.CompilerParams(dimension_semantics=("parallel",), vmem_limit_bytes=VMEM_LIMIT),
        name="out_proj_ffn",
    )(x2, att, ml, w_out, norm_ffn, w_up, w_down, norm_final)


def _layer_weights(l, norm_mix, w_in, b_ig, b_fg, ml_norm, w_out, norm_ffn, w_up, w_down):
    w_gate = jnp.pad(w_in[l, :, GATE_COL0:], ((0, 0), (0, LANES - 2 * N_ML_HEADS)))
    w_all = jnp.concatenate([w_in[l, :, :GATE_COL0].astype(BF16)] + _split_bf16(w_gate, 2), axis=1)
    b_gate = jnp.pad(jnp.concatenate([b_ig[l], b_fg[l]]), (0, LANES - 2 * N_ML_HEADS))[None, :]
    return dict(norm_mix=norm_mix[l][None, :], w_all=w_all, b_gate=b_gate,
                ml_norm=ml_norm[l][None, :], w_out=w_out[l].astype(BF16), norm_ffn=norm_ffn[l][None, :],
                w_up=w_up[l].astype(BF16), w_down=w_down[l].astype(BF16))


def _trunk_layer(x, w, attend, state0, norm_final, tiles):
    b, s, _ = x.shape
    x2 = x.reshape(b * s, D_MODEL)
    aq, ak, av, mq, mk, mv, mo, gates, *kv_t = _project(x2, w["norm_mix"], w["w_all"], w["b_gate"], tiles.tm, s)
    r3 = lambda a: a.reshape(b, s, a.shape[-1])
    if kv_t:
        k_out, v_out = (a.reshape(b, N_ATT_HEADS, ATT_HEAD_DIM, s).transpose(0, 3, 1, 2) for a in kv_t)
    else:
        k_out, v_out = (a.reshape(b, s, N_ATT_HEADS, ATT_HEAD_DIM) for a in (ak, av))
    att = attend(r3(aq), r3(ak), r3(av))
    c0, n0, m0 = state0
    ml, c_new, n_new, m_new = _mlstm(r3(mq), r3(mk), r3(mv), r3(gates), r3(mo), w["ml_norm"], c0, n0, m0, tiles.ml_chunk,
                                     tiles.ml_batch_rows)
    y = _out_ffn(x2, att.reshape(b * s, ATT_WIDTH), ml.reshape(b * s, ML_WIDTH), w["w_out"], w["norm_ffn"],
                 w["w_up"], w["w_down"], norm_final, tiles.tm, tiles.ff_chunk)
    return y.reshape(b, s, D_MODEL), k_out, v_out, c_new, n_new[:, :, :, 0], m_new[:, :, 0, 0]


def kernel(x_prompt, x_sample, cache_k, cache_v, state_C, state_n, state_m, page_table, rel_bias, norm_mix, w_in,
           b_ig, b_fg, ml_norm, w_out, norm_ffn, w_up, w_down, norm_final):
    depth = w_in.shape[0]
    assert depth == 1, "the fused final RMSNorm assumes a single layer"
    bp, sp, _ = x_prompt.shape
    db, dec_seq, _ = x_sample.shape
    n_phys = cache_k.shape[1]
    t0, t1, s_own, s_near, s_far = _bias_tiles(rel_bias, dec_seq)
    gl = norm_final[None, :]
    l = 0
    w = _layer_weights(l, norm_mix, w_in, b_ig, b_fg, ml_norm, w_out, norm_ffn, w_up, w_down)

    zero_state = (jnp.zeros((bp, N_ML_HEADS, ML_HEAD_DIM, ML_HEAD_DIM), F32),
                  jnp.zeros((bp, N_ML_HEADS, ML_HEAD_DIM, LANES), F32), jnp.zeros((bp, N_ML_HEADS, 1, LANES), F32))
    yp, kp, vp, cp, np_, mp = _trunk_layer(
        x_prompt, w, lambda q, k, v: _moba_prompt(q, k, v, t0, t1, rel_bias, PROMPT_KEY_GROUP, PROMPT_HEADS_PER_STEP),
        zero_state, gl,
        PROMPT_TILES)

    to_pool = lambda a: jnp.transpose(a, (0, 1, 3, 4, 2)).reshape(depth * n_phys, ATT_WIDTH, PAGE_SIZE)
    pool_kt, pool_vt = to_pool(cache_k), to_pool(cache_v)
    pt = page_table + l * n_phys

    def attend_sample(q, k, v):
        return _moba_sample(q, k, v, pool_kt, pool_vt, pt, s_own, s_near, s_far, SAMPLE_BLOCKS_PER_STEP)

    state0 = (state_C[l], jnp.broadcast_to(state_n[l][:, :, :, None], (db, N_ML_HEADS, ML_HEAD_DIM, LANES)),
              jnp.broadcast_to(state_m[l][:, :, None, None], (db, N_ML_HEADS, 1, LANES)))
    ys, ks, vs, cs, ns, ms = _trunk_layer(x_sample, w, attend_sample, state0, gl, SAMPLE_TILES)

    st = lambda a: a[None]
    return (yp, ys, st(kp), st(vp), st(cp), st(np_), st(mp), st(ks), st(vs), st(cs), st(ns), st(ms))
```

```python
import functools
import math
from typing import NamedTuple

import numpy as np
import jax
import jax.numpy as jnp
from jax import lax
from jax.experimental import pallas as pl
from jax.experimental.pallas import tpu as pltpu

F32 = jnp.float32
BF16 = jnp.bfloat16

D_MODEL = 1024
ATT_HEAD_DIM = 64
N_ATT_HEADS = 8
ATT_WIDTH = 512
N_ML_HEADS = 4
ML_HEAD_DIM = 128
ML_WIDTH = 512
SLAB = 512
N_SLABS = 7
GATE_COL0 = N_SLABS * SLAB
MOBA_BLOCK = 256
MOBA_TOPK = 3
PAGE_SIZE = 128
NUM_BUCKETS = 32
MAX_DISTANCE = 128
D_FF = 4 * D_MODEL
RMS_EPS = 1e-6
NEG = -1e30
LOG2E = math.log2(math.e)
LANES = 128
SUM_ROWS = 16
MAX_SLACK = 64.0
BOUND_INFLATE = 1.05
BF16_LOGIT_ERR = 2.0 ** -6
VMEM_LIMIT = 56 * 1024 * 1024


class _Tiles(NamedTuple):
    tm: int
    ml_chunk: int
    ml_batch_rows: int
    ff_chunk: int = 1024


PROMPT_TILES = _Tiles(tm=512, ml_chunk=512, ml_batch_rows=4)
SAMPLE_TILES = _Tiles(tm=256, ml_chunk=128, ml_batch_rows=4)
PROMPT_KEY_GROUP = 4
PROMPT_HEADS_PER_STEP = 4
SAMPLE_BLOCKS_PER_STEP = 8
PAGE_RING = 3


def _nt_dot(a, b):
    return lax.dot_general(a, b, (((1,), (1,)), ((), ())), preferred_element_type=F32)


def _split_bf16(x, parts):
    out = []
    for _ in range(parts):
        p = x.astype(BF16)
        out.append(p)
        x = x - p.astype(F32)
    return out


def _dot_x3(a, b, contract):
    (ah, al), (bh, bl) = _split_bf16(a, 2), _split_bf16(b, 2)
    d = lambda x, y: lax.dot_general(x, y, (contract, ((), ())), preferred_element_type=F32)
    return d(ah, bh) + (d(ah, bl) + d(al, bh))


def _tn_dot(a, b):
    return lax.dot_general(a, b, (((0,), (0,)), ((), ())), preferred_element_type=F32)


def _bucket_upper_bounds():
    n = np.arange(0, MAX_DISTANCE + 1)
    max_exact = NUM_BUCKETS // 2
    nf = np.maximum(n, 1).astype(np.float32)
    large = max_exact + (np.log(nf / np.float32(max_exact)) / np.float32(math.log(MAX_DISTANCE / max_exact))
                         * np.float32(NUM_BUCKETS - max_exact)).astype(np.int32)
    large = np.minimum(large, NUM_BUCKETS - 1)
    bucket = np.where(n < max_exact, n, large)
    return np.array([int(n[bucket <= b].max()) for b in range(NUM_BUCKETS - 1)], dtype=np.int32)


_BUCKET_UPPER = _bucket_upper_bounds()


def _bias_of_dist(dist, h, rb_ref):
    acc = jnp.full(dist.shape, rb_ref[NUM_BUCKETS - 1, h], F32)
    for b in range(NUM_BUCKETS - 2, -1, -1):
        acc = jnp.where(dist <= int(_BUCKET_UPPER[b]), rb_ref[b, h], acc)
    return acc


def _bias_tiles_kernel(rb_ref, t0_ref, t1_ref, s_own_ref, s_near_ref, s_far_ref, *, dec_seq):
    h = pl.program_id(0)
    kk = lax.broadcasted_iota(jnp.int32, (MOBA_BLOCK, MOBA_BLOCK), 0)
    qq = lax.broadcasted_iota(jnp.int32, (MOBA_BLOCK, MOBA_BLOCK), 1)
    d0 = qq - kk
    t0_ref[0] = jnp.where(d0 >= 0, _bias_of_dist(jnp.maximum(d0, 0), h, rb_ref) * LOG2E, NEG)
    t1_ref[0] = _bias_of_dist(d0 + MOBA_BLOCK, h, rb_ref) * LOG2E
    t = lax.broadcasted_iota(jnp.int32, (dec_seq, LANES), 0)
    cc = lax.broadcasted_iota(jnp.int32, (dec_seq, LANES), 1)
    ds = t - cc
    s_own_ref[0] = jnp.where((ds >= 0) & (cc < dec_seq),
                             _bias_of_dist(jnp.maximum(ds, 0), h, rb_ref) * LOG2E, NEG)
    s_near_ref[0] = _bias_of_dist(ds + PAGE_SIZE, h, rb_ref) * LOG2E
    s_far_ref[0] = jnp.full((dec_seq, LANES), rb_ref[NUM_BUCKETS - 1, h] * LOG2E, F32)


def _bias_tiles(rel_bias, dec_seq):
    smem = pl.BlockSpec(memory_space=pltpu.SMEM)
    big = pl.BlockSpec((1, MOBA_BLOCK, MOBA_BLOCK), lambda h: (h, 0, 0))
    small = pl.BlockSpec((1, dec_seq, LANES), lambda h: (h, 0, 0))
    return pl.pallas_call(
        functools.partial(_bias_tiles_kernel, dec_seq=dec_seq),
        grid=(N_ATT_HEADS,),
        in_specs=[smem],
        out_specs=[big, big, small, small, small],
        out_shape=[jax.ShapeDtypeStruct((N_ATT_HEADS, MOBA_BLOCK, MOBA_BLOCK), F32)] * 2
        + [jax.ShapeDtypeStruct((N_ATT_HEADS, dec_seq, LANES), F32)] * 3,
        name="bias_tiles",
    )(rel_bias)


def _rms(x, g):
    return x * lax.rsqrt(jnp.mean(x * x, axis=-1, keepdims=True) + RMS_EPS) * g


def _log_sigmoid(z):
    return jnp.minimum(z, 0.0) - jnp.log1p(jnp.exp(-jnp.abs(z)))


def _proj_kernel(x_ref, g_ref, w_ref, bg_ref, aq_ref, ak_ref, av_ref, mq_ref, mk_ref, mv_ref, mo_ref,
                 gate_ref, *kv_t_refs):
    xn = _rms(x_ref[...], g_ref[...])
    xb = xn.astype(BF16)
    outs = (aq_ref, ak_ref, av_ref, mq_ref, mk_ref, mv_ref, mo_ref)
    scales = (ATT_HEAD_DIM ** -0.5 * LOG2E, None, None, None, ML_HEAD_DIM ** -0.5, None, None)
    for i, (o_ref, s) in enumerate(zip(outs, scales)):
        r = jnp.dot(xb, w_ref[:, i * SLAB:(i + 1) * SLAB], preferred_element_type=F32)
        o_ref[...] = r if s is None else r * s
        if kv_t_refs and i in (1, 2):
            kv_t_refs[i - 1][0] = r.T
    gate_hi_lo = jnp.dot(xb, w_ref[:, GATE_COL0:GATE_COL0 + 2 * LANES], preferred_element_type=F32)
    x_lo = (xn - xb.astype(F32)).astype(BF16)
    z = gate_hi_lo[:, :LANES] + (gate_hi_lo[:, LANES:]
                                 + jnp.dot(x_lo, w_ref[:, GATE_COL0:GATE_COL0 + LANES], preferred_element_type=F32))
    z = z + bg_ref[...]
    lane = lax.broadcasted_iota(jnp.int32, z.shape, 1)
    gate_ref[...] = jnp.where((lane >= N_ML_HEADS) & (lane < 2 * N_ML_HEADS), _log_sigmoid(z), z)


def _project(x2, norm_g, w_all, b_gate, tm, seq):
    n = x2.shape[0]
    const = lambda shape: pl.BlockSpec(shape, lambda i: (0, 0), pipeline_mode=pl.Buffered(1))
    slab = pl.BlockSpec((tm, SLAB), lambda i: (i, 0))
    out_specs = [slab] * N_SLABS + [pl.BlockSpec((tm, LANES), lambda i: (i, 0))]
    out_shape = [jax.ShapeDtypeStruct((n, SLAB), F32)] * N_SLABS + [jax.ShapeDtypeStruct((n, LANES), F32)]
    if seq % tm == 0:
        tiles = seq // tm
        out_specs += [pl.BlockSpec((1, SLAB, tm), lambda i: (i // tiles, 0, i % tiles))] * 2
        out_shape += [jax.ShapeDtypeStruct((n // seq, SLAB, seq), F32)] * 2
    return pl.pallas_call(
        _proj_kernel,
        grid=(n // tm,),
        in_specs=[pl.BlockSpec((tm, D_MODEL), lambda i: (i, 0)), const((1, D_MODEL)),
                  const((D_MODEL, N_SLABS * SLAB + 2 * LANES)), const((1, LANES))],
        out_specs=out_specs,
        out_shape=out_shape,
        compiler_params=pltpu.CompilerParams(dimension_semantics=("parallel",), vmem_limit_bytes=VMEM_LIMIT),
        name="rms_in_proj",
    )(x2, norm_g, w_all, b_gate)


def _topk_penalty(scores, n_valid):
    lane = lax.broadcasted_iota(jnp.int32, scores.shape, 1)
    rank = jnp.zeros(scores.shape, jnp.int32)
    for n in range(n_valid):
        col = scores[:, n:n + 1]
        rank = rank + jnp.where((col > scores) | ((col == scores) & (lane > n)), 1, 0)
    return jnp.where((rank < MOBA_TOPK) & (lane < n_valid), 0.0, NEG)


def _moba_prompt_kernel(rb_ref, q_ref, k_ref, v_ref, t0_ref, t1_ref, o_ref, km_ref, kb_ref, vt_ref, sc_ref,
                        pen_ref, sp_ref, rad_ref, cen_ref, m_ref, *, n_blocks, group, heads):
    hp = pl.program_id(1)
    own = pl.program_id(2)
    blk = MOBA_BLOCK
    nbp = km_ref.shape[0]
    gl = heads * ATT_HEAD_DIM

    sel_l = lax.broadcasted_iota(jnp.int32, (gl, LANES), 0)
    sel_c = lax.broadcasted_iota(jnp.int32, (gl, LANES), 1)
    head_sel = jnp.where(sel_l // ATT_HEAD_DIM == sel_c, 1.0, 0.0).astype(BF16)

    @pl.when(own == 0)
    def _():
        km_ref[...] = jnp.zeros(km_ref.shape, F32)
        rad_ref[...] = jnp.zeros(rad_ref.shape, F32)
        cen_ref[...] = jnp.zeros(cen_ref.shape, F32)
        for n in range(n_blocks):
            kn = k_ref[0, n * blk:(n + 1) * blk, :]
            cn = jnp.mean(kn, axis=0, keepdims=True)
            km_ref[n:n + 1, :] = cn
            kb_ref[n] = kn.astype(BF16)
            dk = kn - cn
            rad_ref[n:n + 1, :] = jnp.max(jnp.dot((dk * dk).astype(BF16), head_sel, preferred_element_type=F32),
                                          axis=0, keepdims=True)
            cen_ref[n:n + 1, :] = jnp.dot(jnp.broadcast_to(cn * cn, (8, gl)).astype(BF16), head_sel,
                                          preferred_element_type=F32)[0:1]
            vtn = v_ref[0, n * blk:(n + 1) * blk, :].T.astype(BF16)
            for hh in range(heads):
                vt_ref[n, hh, 0:ATT_HEAD_DIM, :] = vtn[hh * ATT_HEAD_DIM:(hh + 1) * ATT_HEAD_DIM]
                vt_ref[n, hh, ATT_HEAD_DIM:, :] = jnp.ones((SUM_ROWS, blk), BF16)

    q2 = q_ref[0]
    lane = lax.broadcasted_iota(jnp.int32, q2.shape, 1)
    q_hi, q_lo = _split_bf16(q2, 2)
    qcat_b = jnp.concatenate([jnp.where((lane // ATT_HEAD_DIM) == hh, q_hi, jnp.zeros_like(q_hi))
                              for hh in range(heads)], axis=0)
    wide = heads * blk
    lane_m = lax.broadcasted_iota(jnp.int32, (nbp, gl), 1)
    km2 = jnp.concatenate([jnp.where((lane_m // ATT_HEAD_DIM) == hh, km_ref[...], 0.0) for hh in range(heads)], axis=0)
    km_hi, km_lo = _split_bf16(km2, 2)
    hi_q = _nt_dot(jnp.concatenate([km_hi, km_lo], axis=0), q_hi)
    sc2 = hi_q[:heads * nbp] + (hi_q[heads * nbp:] + _nt_dot(km_hi, q_lo))
    sc_ref[...] = jnp.concatenate([sc2[hh * nbp:(hh + 1) * nbp] for hh in range(heads)], axis=1)

    sub = lax.broadcasted_iota(jnp.int32, (nbp, wide), 0)

    def rank_body(n, rank):
        sc = sc_ref[...]
        row = sc_ref[pl.ds(n, 1), :]
        return rank + jnp.where((row > sc) | ((row == sc) & (sub > n)), 1, 0)

    rank = lax.fori_loop(0, own, rank_body, jnp.zeros((nbp, wide), jnp.int32))
    pen_ref[...] = jnp.where((rank < MOBA_TOPK) & (sub < own), 0.0, NEG)

    far_bias = jnp.concatenate([jnp.full((1, blk), rb_ref[NUM_BUCKETS - 1, heads * hp + hh] * LOG2E, F32)
                                for hh in range(heads)], axis=1)

    def fold8(x, op):
        return op(x.reshape(blk // 8, 8, x.shape[1]), axis=0)

    def far_row(j):
        return jnp.where(j < own - 1, pen_ref[pl.ds(j, 1), :] + far_bias, NEG)

    grp = group
    n_groups = (jnp.maximum(own - 1, 0) + grp - 1) // grp
    jn = jnp.maximum(own - 1, 0)
    near_row = jnp.where(own >= 1, pen_ref[pl.ds(jn, 1), :], NEG)

    t0c = jnp.concatenate([t0_ref[hh] for hh in range(heads)], axis=1)
    t1c = jnp.concatenate([t1_ref[hh] for hh in range(heads)], axis=1)
    no_prev = own == 0
    s_pair = _nt_dot(kb_ref[pl.ds(jn, 2)].reshape(2 * blk, gl), qcat_b)
    s_first = s_pair[:blk] + jnp.where(no_prev, t0c, t1c + near_row)
    s_second = jnp.where(no_prev, NEG, s_pair[blk:] + t0c)
    sp_ref[0:blk, :] = s_first
    sp_ref[blk:, :] = s_second
    m8 = jnp.maximum(fold8(s_first, jnp.max), fold8(s_second, jnp.max))
    pair_max = jnp.max(m8, axis=0, keepdims=True)

    row8 = lax.broadcasted_iota(jnp.int32, (8, gl), 0)
    lane8 = lax.broadcasted_iota(jnp.int32, (8, gl), 1)
    q_sq = _nt_dot(jnp.where(lane8 // ATT_HEAD_DIM == row8, 1.0, 0.0).astype(BF16),
                   (q2 * q2).astype(BF16))
    q_norm = jnp.sqrt(jnp.concatenate([q_sq[hh:hh + 1] for hh in range(heads)], axis=1))
    per_head = lambda x: jnp.concatenate([jnp.broadcast_to(x[:, hh:hh + 1], (nbp, blk)) for hh in range(heads)], axis=1)
    radius, centre = per_head(jnp.sqrt(rad_ref[...])), per_head(jnp.sqrt(cen_ref[...]))
    far_pen = jnp.where(sub < own - 1, pen_ref[...], NEG)
    mean_logit = sc_ref[...] + far_bias + far_pen
    spread = q_norm * (BOUND_INFLATE * radius + BF16_LOGIT_ERR * (centre + radius))
    far_low = jnp.max(mean_logit, axis=0, keepdims=True)
    far_high = jnp.max(mean_logit + spread, axis=0, keepdims=True)
    m_ref[...] = jnp.maximum(pair_max, far_high)
    slack = jnp.maximum(pair_max, far_high) - jnp.maximum(pair_max, far_low)

    @pl.when(jnp.max(slack) > MAX_SLACK)
    def _():
        def max_body(g, m8c):
            j0 = pl.multiple_of(g * grp, grp)
            sg = _nt_dot(kb_ref[pl.ds(j0, grp)].reshape(grp * blk, gl), qcat_b)
            for i in range(grp):
                m8c = jnp.maximum(m8c, fold8(sg[i * blk:(i + 1) * blk], jnp.max) + far_row(j0 + i))
            return m8c

        m_ref[...] = jnp.max(lax.fori_loop(0, n_groups, max_body, m8), axis=0, keepdims=True)

    m = m_ref[...]

    def accumulate(j, p, accs):
        pb = p.astype(BF16)
        return tuple(accs[hh] + jnp.dot(vt_ref[j, hh], pb[:, hh * blk:(hh + 1) * blk], preferred_element_type=F32)
                     for hh in range(heads))

    pb_pair = jnp.exp2(sp_ref[...] - m).astype(BF16)
    vt_pair = vt_ref[pl.ds(jn, 2)]
    accs = tuple(jnp.dot(jnp.concatenate([vt_pair[i, hh] for i in range(2)], axis=1),
                         pb_pair[:, hh * blk:(hh + 1) * blk], preferred_element_type=F32) for hh in range(heads))

    def exp_body(g, accc):
        j0 = pl.multiple_of(g * grp, grp)
        sg = _nt_dot(kb_ref[pl.ds(j0, grp)].reshape(grp * blk, gl), qcat_b)
        for i in range(grp):
            accc = accumulate(j0 + i, jnp.exp2(sg[i * blk:(i + 1) * blk] + (far_row(j0 + i) - m)), accc)
        return accc

    accs = lax.fori_loop(0, n_groups, exp_body, accs)
    out_t = jnp.concatenate([a[:ATT_HEAD_DIM] / a[ATT_HEAD_DIM:ATT_HEAD_DIM + 1] for a in accs], axis=0)
    o_ref[0] = out_t.T


def _moba_prompt(aq, ak, av, t0, t1, rel_bias, group, heads):
    b, s, _ = aq.shape
    nb = s // MOBA_BLOCK
    nbp = -(-nb // 8) * 8
    n_groups = N_ATT_HEADS // heads
    gl = heads * ATT_HEAD_DIM
    wide = heads * MOBA_BLOCK
    assert nb % group == 0
    qspec = pl.BlockSpec((1, MOBA_BLOCK, gl), lambda bi, hp, qi: (bi, qi, hp))
    kvspec = pl.BlockSpec((1, s, gl), lambda bi, hp, qi: (bi, 0, hp))
    tspec = pl.BlockSpec((heads, MOBA_BLOCK, MOBA_BLOCK), lambda bi, hp, qi: (hp, 0, 0))
    return pl.pallas_call(
        functools.partial(_moba_prompt_kernel, n_blocks=nb, group=group, heads=heads),
        grid=(b, n_groups, nb),
        in_specs=[pl.BlockSpec(memory_space=pltpu.SMEM), qspec, kvspec, kvspec, tspec, tspec],
        out_specs=qspec,
        out_shape=jax.ShapeDtypeStruct((b, s, ATT_WIDTH), F32),
        scratch_shapes=[pltpu.VMEM((nbp, gl), F32), pltpu.VMEM((nb, MOBA_BLOCK, gl), BF16),
                        pltpu.VMEM((nb, heads, ATT_HEAD_DIM + SUM_ROWS, MOBA_BLOCK), BF16),
                        pltpu.VMEM((nbp, wide), F32), pltpu.VMEM((nbp, wide), F32),
                        pltpu.VMEM((2 * MOBA_BLOCK, wide), F32), pltpu.VMEM((nbp, LANES), F32),
                        pltpu.VMEM((nbp, LANES), F32), pltpu.VMEM((1, wide), F32)],
        compiler_params=pltpu.CompilerParams(dimension_semantics=("parallel", "parallel", "arbitrary"),
                                             vmem_limit_bytes=VMEM_LIMIT),
        name="moba_prompt",
    )(rel_bias, aq, ak, av, t0, t1)


def _moba_sample_kernel(pt_ref, pool_k_ref, pool_v_ref, q_ref, kn_ref, vn_ref, s_own_ref, s_near_ref, s_far_ref, o_ref,
                        qbd_ref, kmt_ref, mm_ref, ll_ref, oo_ref, own_m_ref, own_l_ref, own_o_ref,
                        kbuf_ref, vbuf_ref, sem_ref, *, blocks_per_step, n_blocks, dec_seq):
    ppb = MOBA_BLOCK // PAGE_SIZE
    pps = blocks_per_step * ppb
    c = pl.program_id(1)
    n_steps = pl.num_programs(1)
    step = pl.program_id(0) * n_steps + c
    total = pl.num_programs(0) * n_steps

    def page_copies(t):
        tb, tc, slot = t // n_steps, t % n_steps, t % PAGE_RING
        copies = []
        for i in range(pps):
            page = pt_ref[tb, tc * pps + i]
            copies.append(pltpu.make_async_copy(pool_k_ref.at[page], kbuf_ref.at[slot, i], sem_ref.at[0, slot]))
            copies.append(pltpu.make_async_copy(pool_v_ref.at[page], vbuf_ref.at[slot, i], sem_ref.at[1, slot]))
        return copies

    @pl.when(step == 0)
    def _():
        for t in range(PAGE_RING - 1):
            for cp in page_copies(t):
                cp.start()

    @pl.when(step + (PAGE_RING - 1) < total)
    def _():
        for cp in page_copies(step + (PAGE_RING - 1)):
            cp.start()

    for cp in page_copies(step):
        cp.wait()
    slot = step % PAGE_RING
    rows = N_ATT_HEADS * dec_seq
    row_head = lax.broadcasted_iota(jnp.int32, (rows, ATT_WIDTH), 0) // dec_seq
    lane_head = lax.broadcasted_iota(jnp.int32, (rows, ATT_WIDTH), 1) // ATT_HEAD_DIM
    lane_r = lax.broadcasted_iota(jnp.int32, (rows, LANES), 1)
    lane_k = lax.broadcasted_iota(jnp.int32, (ATT_WIDTH, LANES), 1)

    def local_softmax(s):
        m = jnp.max(s, axis=-1, keepdims=True)
        p = jnp.exp2(s - m)
        return m, jnp.sum(p, axis=-1, keepdims=True), p.astype(BF16)

    @pl.when(c == 0)
    def _():
        qrep = jnp.concatenate([q_ref[0]] * N_ATT_HEADS, axis=0)
        qbd = jnp.where(row_head == lane_head, qrep, 0.0)
        qbd_ref[...] = qbd
        kmt_ref[...] = jnp.zeros(kmt_ref.shape, F32)
        mm_ref[...] = jnp.zeros(mm_ref.shape, F32)
        ll_ref[...] = jnp.zeros(ll_ref.shape, F32)
        pad = jnp.zeros((LANES - dec_seq, ATT_WIDTH), F32)
        kn = jnp.concatenate([kn_ref[0], pad], axis=0).astype(BF16)
        vn = jnp.concatenate([vn_ref[0], pad], axis=0).astype(BF16)
        m, l, p = local_softmax(_nt_dot(qbd.astype(BF16), kn) + s_own_ref[...].reshape(rows, LANES))
        own_m_ref[...] = jnp.broadcast_to(m, (rows, LANES))
        own_l_ref[...] = jnp.broadcast_to(l, (rows, LANES))
        own_o_ref[...] = jnp.dot(p, vn, preferred_element_type=F32)

    qb = qbd_ref[...].astype(BF16)
    s_near = s_near_ref[...].reshape(rows, LANES)
    s_far = s_far_ref[...].reshape(rows, LANES)
    n0 = c * blocks_per_step
    kt = jnp.concatenate([kbuf_ref[slot, i] for i in range(pps)], axis=1)
    s_all = jnp.dot(qb, kt.astype(BF16), preferred_element_type=F32)
    mm, ll, kmt = mm_ref[...], ll_ref[...], kmt_ref[...]
    for g in range(blocks_per_step):
        n = n0 + g
        bias = jnp.concatenate([s_far] * (ppb - 1) + [jnp.where(n == n_blocks - 1, s_near, s_far)], axis=1)
        m, l, p = local_softmax(s_all[:, g * MOBA_BLOCK:(g + 1) * MOBA_BLOCK] + bias)
        vt = jnp.concatenate([vbuf_ref[slot, ppb * g + i] for i in range(ppb)], axis=1).astype(BF16)
        oo_ref[n] = _nt_dot(p, vt)
        mm = jnp.where(lane_r == n, m, mm)
        ll = jnp.where(lane_r == n, l, ll)
        ksum = jnp.sum(kt[:, g * MOBA_BLOCK:(g + 1) * MOBA_BLOCK], axis=-1, keepdims=True)
        kmt = jnp.where(lane_k == n, ksum, kmt)
    mm_ref[...] = mm
    ll_ref[...] = ll
    kmt_ref[...] = kmt

    @pl.when(c == pl.num_programs(1) - 1)
    def _():
        kmean_t = kmt_ref[...] * (1.0 / MOBA_BLOCK)
        pen = _topk_penalty(_dot_x3(qbd_ref[...], kmean_t, ((1,), (0,))), n_blocks)
        mp = mm_ref[...] + pen
        m_own = own_m_ref[...]
        m_tot = jnp.maximum(jnp.max(mp, axis=-1, keepdims=True), m_own)
        w = jnp.exp2(mp - m_tot)
        w_own = jnp.exp2(m_own - m_tot)
        den = jnp.sum(w * ll_ref[...], axis=-1, keepdims=True) + w_own * own_l_ref[...]

        num = w_own[:, 0:1] * own_o_ref[...]
        for nn in range(n_blocks):
            num = num + w[:, nn:nn + 1] * oo_ref[nn]
        full = jnp.where(row_head == lane_head, num / den[:, 0:1], 0.0)
        out = full[0:dec_seq]
        for h in range(1, N_ATT_HEADS):
            out = out + full[h * dec_seq:(h + 1) * dec_seq]
        o_ref[0] = out


def _moba_sample(aq, ak, av, pool_kt, pool_vt, page_table, s_own, s_near, s_far, blocks_per_step):
    db, dec_seq, _ = aq.shape
    n_pages = page_table.shape[1]
    ppb = MOBA_BLOCK // PAGE_SIZE
    nb = n_pages // ppb
    pps = blocks_per_step * ppb
    assert nb % blocks_per_step == 0 and nb <= LANES
    rows = N_ATT_HEADS * dec_seq

    per_b = pl.BlockSpec((1, dec_seq, ATT_WIDTH), lambda b, c, pt: (b, 0, 0))
    tile = pl.BlockSpec((N_ATT_HEADS, dec_seq, LANES), lambda b, c, pt: (0, 0, 0))
    grid_spec = pltpu.PrefetchScalarGridSpec(
        num_scalar_prefetch=1, grid=(db, nb // blocks_per_step),
        in_specs=[pl.BlockSpec(memory_space=pl.ANY)] * 2 + [per_b, per_b, per_b, tile, tile, tile],
        out_specs=per_b,
        scratch_shapes=[pltpu.VMEM((rows, ATT_WIDTH), F32), pltpu.VMEM((ATT_WIDTH, LANES), F32),
                        pltpu.VMEM((rows, LANES), F32), pltpu.VMEM((rows, LANES), F32),
                        pltpu.VMEM((nb, rows, ATT_WIDTH), F32), pltpu.VMEM((rows, LANES), F32),
                        pltpu.VMEM((rows, LANES), F32), pltpu.VMEM((rows, ATT_WIDTH), F32),
                        pltpu.VMEM((PAGE_RING, pps, ATT_WIDTH, PAGE_SIZE), F32),
                        pltpu.VMEM((PAGE_RING, pps, ATT_WIDTH, PAGE_SIZE), F32),
                        pltpu.SemaphoreType.DMA((2, PAGE_RING))])
    return pl.pallas_call(
        functools.partial(_moba_sample_kernel, blocks_per_step=blocks_per_step, n_blocks=nb, dec_seq=dec_seq),
        grid_spec=grid_spec,
        out_shape=jax.ShapeDtypeStruct((db, dec_seq, ATT_WIDTH), F32),
        compiler_params=pltpu.CompilerParams(dimension_semantics=("arbitrary", "arbitrary"),
                                             vmem_limit_bytes=VMEM_LIMIT),
        name="moba_sample",
    )(page_table, pool_kt, pool_vt, aq, ak, av, s_own, s_near, s_far)


def _mlstm_kernel(q_ref, k_ref, v_ref, g_ref, mo_ref, nw_ref, c0_ref, n0_ref, m0_ref,
                  h_ref, c_out, n_out, m_out, c_s, n_s, m_s, *, rows, chunk, batch_rows):
    ci = pl.program_id(1)
    L = chunk

    @pl.when(ci == 0)
    def _():
        c_s[...] = c0_ref[...]
        n_s[...] = n0_ref[...]
        m_s[...] = m0_ref[...] * LOG2E

    def padded(ref, bi, fill=None):
        x = ref[bi]
        if rows == L:
            return x
        tail = jnp.zeros((L - rows, x.shape[1]), F32) if fill is None else fill
        return jnp.concatenate([x, tail], axis=0)

    lane = lax.broadcasted_iota(jnp.int32, (L, LANES), 1)
    pad_lane = lax.broadcasted_iota(jnp.int32, (max(L - rows, 1), LANES), 1)

    is_lf = (lane >= N_ML_HEADS) & (lane < 2 * N_ML_HEADS)
    tri_r = lax.broadcasted_iota(jnp.int32, (L, L), 0)
    tri_c = lax.broadcasted_iota(jnp.int32, (L, L), 1)
    causal = tri_c <= tri_r
    ones_b = jnp.ones((L, LANES), BF16)
    reps = L // LANES
    wide = lambda x: x if reps == 1 else jnp.concatenate([x] * reps, axis=1)
    for bi in range(batch_rows):
        g = padded(g_ref, bi, jnp.where(pad_lane < N_ML_HEADS, NEG, 0.0)) * LOG2E
        q_all, k_all, v_all, mo_all = padded(q_ref, bi), padded(k_ref, bi), padded(v_ref, bi), padded(mo_ref, bi)
        g3 = jnp.concatenate(_split_bf16(jnp.where(is_lf, g, 0.0), 3), axis=1)
        b3 = jnp.dot(causal.astype(BF16), g3, preferred_element_type=F32)
        bcum = b3[:, :LANES] + (b3[:, LANES:2 * LANES] + b3[:, 2 * LANES:])
        g_t = g.T
        b_t = bcum.T

        for h in range(N_ML_HEADS):
            sl = slice(h * ML_HEAD_DIM, (h + 1) * ML_HEAD_DIM)
            q, k, v = q_all[:, sl], k_all[:, sl], v_all[:, sl]
            ig = jnp.broadcast_to(g[:, h:h + 1], (L, LANES))
            b = jnp.broadcast_to(bcum[:, N_ML_HEADS + h:N_ML_HEADS + h + 1], (L, LANES))
            ig_row = g_t[h:h + 1, :]
            b_row = b_t[N_ML_HEADS + h:N_ML_HEADS + h + 1, :]
            m_prev = m_s[bi, h]
            C = c_s[bi, h]
            n_rep = n_s[bi, h]

            a = b + m_prev
            log_d = jnp.where(causal, wide(b) - b_row + ig_row, NEG)
            m_t = jnp.maximum(a, jnp.max(log_d, axis=-1, keepdims=True))
            d = jnp.exp2(log_d - wide(m_t))
            w_state = jnp.exp2(a - m_t)
            qb, kb, vb = q.astype(BF16), k.astype(BF16), v.astype(BF16)
            s = _nt_dot(qb, kb) * d
            v_ones = jnp.concatenate([vb, ones_b], axis=1)
            from_state = jnp.dot(qb, jnp.concatenate([C, n_rep], axis=1).astype(BF16), preferred_element_type=F32)
            from_chunk = jnp.dot(s.astype(BF16), v_ones, preferred_element_type=F32)
            num = w_state * from_state[:, :ML_HEAD_DIM] + from_chunk[:, :ML_HEAD_DIM]
            den = w_state * from_state[:, ML_HEAD_DIM:] + from_chunk[:, ML_HEAD_DIM:]
            hid = num / jnp.maximum(jnp.abs(den), jnp.exp2(-m_t))

            hn = _rms(hid, nw_ref[:, sl])
            h_ref[bi, :, sl] = (hn * jax.nn.sigmoid(mo_all[:, sl]))[:rows]

            b_last = b[L - 1:L, :]
            m_new = m_t[L - 1:L, :]
            g_state = jnp.exp2(b_last + m_prev - m_new)
            kg = k * jnp.exp2(b_last - b + ig - m_new)
            added = _tn_dot(kg.astype(BF16), v_ones)
            c_s[bi, h] = g_state * C + added[:, :ML_HEAD_DIM]
            n_s[bi, h] = g_state * n_rep + added[:, ML_HEAD_DIM:]
            m_s[bi, h] = m_new

    @pl.when(ci == pl.num_programs(1) - 1)
    def _():
        c_out[...] = c_s[...]
        n_out[...] = n_s[...]
        m_out[...] = m_s[...] * (1.0 / LOG2E)


def _mlstm(mq, mk, mv, gates, mo, ml_norm, c0, n0, m0, chunk, batch_rows):
    b, s, _ = mq.shape
    rows = min(s, chunk)
    n_chunks = s // rows
    assert b % batch_rows == 0
    bb = batch_rows
    tok = lambda w: pl.BlockSpec((bb, rows, w), lambda bi, ci: (bi, ci, 0))
    st4 = lambda shape: pl.BlockSpec(shape, lambda bi, ci: (bi, 0, 0, 0))
    c_shape, m_shape = (bb, N_ML_HEADS, ML_HEAD_DIM, ML_HEAD_DIM), (bb, N_ML_HEADS, 1, LANES)
    n_shape = (bb, N_ML_HEADS, ML_HEAD_DIM, LANES)
    return pl.pallas_call(
        functools.partial(_mlstm_kernel, rows=rows, chunk=chunk, batch_rows=bb),
        grid=(b // bb, n_chunks),
        in_specs=[tok(ML_WIDTH), tok(ML_WIDTH), tok(ML_WIDTH), tok(LANES), tok(ML_WIDTH),
                  pl.BlockSpec((1, ML_WIDTH), lambda bi, ci: (0, 0)),
                  st4(c_shape), st4(n_shape), st4(m_shape)],
        out_specs=[tok(ML_WIDTH), st4(c_shape), st4(n_shape), st4(m_shape)],
        out_shape=[jax.ShapeDtypeStruct((b, s, ML_WIDTH), F32),
                   jax.ShapeDtypeStruct((b,) + c_shape[1:], F32),
                   jax.ShapeDtypeStruct((b,) + n_shape[1:], F32),
                   jax.ShapeDtypeStruct((b,) + m_shape[1:], F32)],
        scratch_shapes=[pltpu.VMEM(c_shape, F32), pltpu.VMEM(n_shape, F32), pltpu.VMEM(m_shape, F32)],
        compiler_params=pltpu.CompilerParams(dimension_semantics=("parallel", "arbitrary"),
                                             vmem_limit_bytes=VMEM_LIMIT),
        name="mlstm",
    )(mq, mk, mv, gates, mo, ml_norm, c0, n0, m0)


def _out_ffn_kernel(x_ref, att_ref, ml_ref, wo_ref, gf_ref, wu_ref, wd_ref, gl_ref, y_ref, *, ff_chunk):
    x1 = x_ref[...] \
        + jnp.dot(att_ref[...].astype(BF16), wo_ref[0:ATT_WIDTH, :], preferred_element_type=F32) \
        + jnp.dot(ml_ref[...].astype(BF16), wo_ref[ATT_WIDTH:, :], preferred_element_type=F32)
    xb = _rms(x1, gf_ref[...]).astype(BF16)
    ffn = None
    for c in range(D_FF // ff_chunk):
        sl = slice(c * ff_chunk, (c + 1) * ff_chunk)
        u = jnp.maximum(jnp.dot(xb, wu_ref[:, sl], preferred_element_type=F32), 0.0)
        d = jnp.dot((u * u).astype(BF16), wd_ref[sl, :], preferred_element_type=F32)
        ffn = d if ffn is None else ffn + d
    y_ref[...] = _rms(x1 + ffn, gl_ref[...])


def _out_ffn(x2, att, ml, w_out, norm_ffn, w_up, w_down, norm_final, tm, ff_chunk):
    n = x2.shape[0]
    const = lambda shape: pl.BlockSpec(shape, lambda i: (0, 0), pipeline_mode=pl.Buffered(1))
    row = lambda w: pl.BlockSpec((tm, w), lambda i: (i, 0))
    return pl.pallas_call(
        functools.partial(_out_ffn_kernel, ff_chunk=ff_chunk),
        grid=(n // tm,),
        in_specs=[row(D_MODEL), row(ATT_WIDTH), row(ML_WIDTH), const((D_MODEL, D_MODEL)), const((1, D_MODEL)),
                  const((D_MODEL, D_FF)), const((D_FF, D_MODEL)), const((1, D_MODEL))],
        out_specs=row(D_MODEL),
        out_shape=jax.ShapeDtypeStruct((n, D_MODEL), F32),
        compiler_params=pltpu.CompilerParams(dimension_semantics=("parallel",), vmem_limit_bytes=VMEM_LIMIT),
        name="out_proj_ffn",
    )(x2, att, ml, w_out, norm_ffn, w_up, w_down, norm_final)


def _layer_weights(l, norm_mix, w_in, b_ig, b_fg, ml_norm, w_out, norm_ffn, w_up, w_down):
    w_gate = jnp.pad(w_in[l, :, GATE_COL0:], ((0, 0), (0, LANES - 2 * N_ML_HEADS)))
    w_all = jnp.concatenate([w_in[l, :, :GATE_COL0].astype(BF16)] + _split_bf16(w_gate, 2), axis=1)
    b_gate = jnp.pad(jnp.concatenate([b_ig[l], b_fg[l]]), (0, LANES - 2 * N_ML_HEADS))[None, :]
    return dict(norm_mix=norm_mix[l][None, :], w_all=w_all, b_gate=b_gate,
                ml_norm=ml_norm[l][None, :], w_out=w_out[l].astype(BF16), norm_ffn=norm_ffn[l][None, :],
                w_up=w_up[l].astype(BF16), w_down=w_down[l].astype(BF16))


def _trunk_layer(x, w, attend, state0, norm_final, tiles):
    b, s, _ = x.shape
    x2 = x.reshape(b * s, D_MODEL)
    aq, ak, av, mq, mk, mv, mo, gates, *kv_t = _project(x2, w["norm_mix"], w["w_all"], w["b_gate"], tiles.tm, s)
    r3 = lambda a: a.reshape(b, s, a.shape[-1])
    if kv_t:
        k_out, v_out = (a.reshape(b, N_ATT_HEADS, ATT_HEAD_DIM, s).transpose(0, 3, 1, 2) for a in kv_t)
    else:
        k_out, v_out = (a.reshape(b, s, N_ATT_HEADS, ATT_HEAD_DIM) for a in (ak, av))
    att = attend(r3(aq), r3(ak), r3(av))
    c0, n0, m0 = state0
    ml, c_new, n_new, m_new = _mlstm(r3(mq), r3(mk), r3(mv), r3(gates), r3(mo), w["ml_norm"], c0, n0, m0, tiles.ml_chunk,
                                     tiles.ml_batch_rows)
    y = _out_ffn(x2, att.reshape(b * s, ATT_WIDTH), ml.reshape(b * s, ML_WIDTH), w["w_out"], w["norm_ffn"],
                 w["w_up"], w["w_down"], norm_final, tiles.tm, tiles.ff_chunk)
    return y.reshape(b, s, D_MODEL), k_out, v_out, c_new, n_new[:, :, :, 0], m_new[:, :, 0, 0]


def kernel(x_prompt, x_sample, cache_k, cache_v, state_C, state_n, state_m, page_table, rel_bias, norm_mix, w_in,
           b_ig, b_fg, ml_norm, w_out, norm_ffn, w_up, w_down, norm_final):
    depth = w_in.shape[0]
    assert depth == 1, "the fused final RMSNorm assumes a single layer"
    bp, sp, _ = x_prompt.shape
    db, dec_seq, _ = x_sample.shape
    n_phys = cache_k.shape[1]
    t0, t1, s_own, s_near, s_far = _bias_tiles(rel_bias, dec_seq)
    gl = norm_final[None, :]
    l = 0
    w = _layer_weights(l, norm_mix, w_in, b_ig, b_fg, ml_norm, w_out, norm_ffn, w_up, w_down)

    zero_state = (jnp.zeros((bp, N_ML_HEADS, ML_HEAD_DIM, ML_HEAD_DIM), F32),
                  jnp.zeros((bp, N_ML_HEADS, ML_HEAD_DIM, LANES), F32), jnp.zeros((bp, N_ML_HEADS, 1, LANES), F32))
    yp, kp, vp, cp, np_, mp = _trunk_layer(
        x_prompt, w, lambda q, k, v: _moba_prompt(q, k, v, t0, t1, rel_bias, PROMPT_KEY_GROUP, PROMPT_HEADS_PER_STEP),
        zero_state, gl,
        PROMPT_TILES)

    to_pool = lambda a: jnp.transpose(a, (0, 1, 3, 4, 2)).reshape(depth * n_phys, ATT_WIDTH, PAGE_SIZE)
    pool_kt, pool_vt = to_pool(cache_k), to_pool(cache_v)
    pt = page_table + l * n_phys

    def attend_sample(q, k, v):
        return _moba_sample(q, k, v, pool_kt, pool_vt, pt, s_own, s_near, s_far, SAMPLE_BLOCKS_PER_STEP)

    state0 = (state_C[l], jnp.broadcast_to(state_n[l][:, :, :, None], (db, N_ML_HEADS, ML_HEAD_DIM, LANES)),
              jnp.broadcast_to(state_m[l][:, :, None, None], (db, N_ML_HEADS, 1, LANES)))
    ys, ks, vs, cs, ns, ms = _trunk_layer(x_sample, w, attend_sample, state0, gl, SAMPLE_TILES)

    st = lambda a: a[None]
    return (yp, ys, st(kp), st(vp), st(cp), st(np_), st(mp), st(ks), st(vs), st(cs), st(ns), st(ms))
```
